```python
import jax, jax.numpy as jnp
from jax import lax
import numpy as np

D_MODEL = 1024
BATCH = 4
SEQ = 4096
DEPTH = 2
DEC_BATCH = 128
DEC_SEQ = 8
PAST_LEN = 16384
PAGE_SIZE = 128

HEAD_DIM = 64
N_Q_HEADS = 8
N_KV_HEADS = 2
GQA_GROUP = N_Q_HEADS // N_KV_HEADS
D_ATTN = N_Q_HEADS * HEAD_DIM
D_KV = N_KV_HEADS * HEAD_DIM
WINDOW = 128
BLOCK = WINDOW
ATTN_SCALE = HEAD_DIM ** -0.5
D_CONV = D_MODEL // 2
CONV_WIDTH = 3
D_POOL = D_MODEL // 2
POOL_WINDOWS = (2, 4, 8, 16)
N_POOL_GROUPS = len(POOL_WINDOWS)
POOL_GROUP = D_POOL // N_POOL_GROUPS
POOL_OUT_GROUP = D_MODEL // N_POOL_GROUPS
POOL_BUF = max(POOL_WINDOWS) - 1
N_BRANCH = 3
OFF_Q = 0
OFF_K = OFF_Q + D_ATTN
OFF_V = OFF_K + D_KV
OFF_CX = OFF_V + D_KV
OFF_CB = OFF_CX + D_CONV
OFF_CC = OFF_CB + D_CONV
OFF_P = OFF_CC + D_CONV
OFF_G = OFF_P + D_POOL
D_IN = OFF_G + N_BRANCH * D_MODEL
D_FF = 2816
N_EXPERTS = 8
TOP_K = 2
D_EXPERT = 3584
N_DENSE = (DEPTH + 1) // 2
N_MOE = DEPTH // 2
EPS = 1e-5

kernel_name = 'hybrid_gated_swa_conv_pool_decoder_step'


def rms_norm(x, g):
    xf = x.astype(jnp.float32)
    y = xf * lax.rsqrt(jnp.mean(xf * xf, axis=-1, keepdims=True) + EPS)
    return (y * g.astype(jnp.float32)).astype(x.dtype)


def alibi_slopes():
    h = jnp.arange(1, N_Q_HEADS + 1, dtype=jnp.float32)
    return jnp.exp2(-8.0 * h / N_Q_HEADS).reshape(N_KV_HEADS, GQA_GROUP)


def softmax_with_sink(s, sink):
    m = jnp.maximum(jnp.max(s, axis=-1, keepdims=True), sink)
    p = jnp.exp(s - m)
    return p / (jnp.sum(p, axis=-1, keepdims=True) + jnp.exp(sink - m))


def window_probs(s, dist, valid, sink):
    bias = -alibi_slopes()[:, :, None, None] * dist.astype(jnp.float32)
    s = jnp.where(valid, s * ATTN_SCALE + bias, -jnp.inf)
    return softmax_with_sink(s, sink)


def window_attention_prompt(q, k, v, sink):
    B, T = q.shape[0], q.shape[1]
    nb = T // BLOCK
    qb = q.reshape(B, nb, BLOCK, N_KV_HEADS, GQA_GROUP, HEAD_DIM)

    def band(a):
        ap = jnp.concatenate([jnp.zeros((B, BLOCK, N_KV_HEADS, HEAD_DIM), a.dtype), a], axis=1)
        prev = ap[:, :T].reshape(B, nb, BLOCK, N_KV_HEADS, HEAD_DIM)
        cur = ap[:, BLOCK:].reshape(B, nb, BLOCK, N_KV_HEADS, HEAD_DIM)
        return jnp.concatenate([prev, cur], axis=2)

    kb, vb = band(k), band(v)
    s = jnp.einsum('bnqkgd,bnskd->bnkgqs', qb, kb, preferred_element_type=jnp.float32)
    si = jnp.arange(2 * BLOCK)[None, :]
    dist = (jnp.arange(BLOCK)[:, None] + BLOCK) - si
    key_pos = jnp.arange(nb)[:, None, None] * BLOCK - BLOCK + si[None]
    valid = (dist >= 0) & (dist <= WINDOW) & (key_pos >= 0)
    p = window_probs(s, dist, valid[:, None, None], sink)
    o = jnp.einsum('bnkgqs,bnskd->bnqkgd', p.astype(v.dtype), vb)
    return o.reshape(B, T, D_ATTN), k[:, -WINDOW:], v[:, -WINDOW:]


def window_attention_sample(q, k, v, k_buf, v_buf, sink):
    B, T = q.shape[0], q.shape[1]
    w = k_buf.shape[1]
    kc = jnp.concatenate([k_buf.astype(k.dtype), k], axis=1)
    vc = jnp.concatenate([v_buf.astype(v.dtype), v], axis=1)
    qg = q.reshape(B, T, N_KV_HEADS, GQA_GROUP, HEAD_DIM)
    s = jnp.einsum('bqkgd,bskd->bkgqs', qg, kc, preferred_element_type=jnp.float32)
    dist = (w + jnp.arange(T))[:, None] - jnp.arange(w + T)[None, :]
    valid = (dist >= 0) & (dist <= WINDOW)
    p = window_probs(s, dist, valid, sink)
    o = jnp.einsum('bkgqs,bskd->bqkgd', p.astype(v.dtype), vc)
    return o.reshape(B, T, D_ATTN), kc[:, -w:], vc[:, -w:]


def short_conv(u, prefix, conv_w_l):
    T = u.shape[1]
    ext = jnp.concatenate([prefix.astype(u.dtype), u], axis=1)
    y = conv_w_l[0] * ext[:, 0:T]
    for j in range(1, CONV_WIDTH):
        y = y + conv_w_l[j] * ext[:, j:j + T]
    return y, ext[:, -(CONV_WIDTH - 1):]


def pool_mixer(p_in, prefix, w_pool_l, scale_l):
    B, T = p_in.shape[0], p_in.shape[1]
    P = prefix.shape[1]
    ext = jnp.concatenate([prefix.astype(p_in.dtype), p_in], axis=1)
    extf = ext.astype(jnp.float32)
    cs = jnp.concatenate([jnp.zeros((B, 1, D_POOL), jnp.float32), jnp.cumsum(extf, axis=1)], axis=1)
    e = jnp.arange(P, P + T)
    outs = []
    for g, w in enumerate(POOL_WINDOWS):
        csg = cs[..., g * POOL_GROUP:(g + 1) * POOL_GROUP]
        lo = jnp.maximum(e + 1 - w, 0)
        win_sum = csg[:, P + 1:] - jnp.take(csg, lo, axis=1)
        cnt = jnp.minimum(e + 1, w).astype(jnp.float32)[None, :, None]
        outs.append(win_sum / cnt - extf[:, P:, g * POOL_GROUP:(g + 1) * POOL_GROUP])
    d = jnp.stack(outs, axis=2).astype(p_in.dtype)
    y = jnp.einsum('btgc,gcd->btgd', d, w_pool_l).reshape(B, T, D_MODEL) * scale_l
    return y, ext[:, -POOL_BUF:]


def mixing_layer(x, l, k_buf, v_buf, conv_prefix, pool_prefix, mp):
    (norm_mix, w_in, attn_sinks, w_attn_out, conv_w, w_conv_out, w_pool, pool_scale, w_out) = mp
    B, T = x.shape[0], x.shape[1]
    z = rms_norm(x, norm_mix[l]) @ w_in[l]
    q = z[..., OFF_Q:OFF_K].reshape(B, T, N_Q_HEADS, HEAD_DIM)
    k = z[..., OFF_K:OFF_V].reshape(B, T, N_KV_HEADS, HEAD_DIM)
    v = z[..., OFF_V:OFF_CX].reshape(B, T, N_KV_HEADS, HEAD_DIM)
    sink = attn_sinks[l].astype(jnp.float32).reshape(N_KV_HEADS, GQA_GROUP)[:, :, None, None]
    if k_buf is None:
        att, new_k, new_v = window_attention_prompt(q, k, v, sink)
    else:
        att, new_k, new_v = window_attention_sample(q, k, v, k_buf, v_buf, sink)
    att_o = att @ w_attn_out[l]
    u = z[..., OFF_CC:OFF_P] * z[..., OFF_CX:OFF_CB]
    yc, new_conv = short_conv(u, conv_prefix, conv_w[l])
    conv_o = (z[..., OFF_CB:OFF_CC] * yc) @ w_conv_out[l]
    pool_o, new_pool = pool_mixer(z[..., OFF_P:OFF_G], pool_prefix, w_pool[l], pool_scale[l])
    gates = jax.nn.sigmoid(z[..., OFF_G:].reshape(B, T, N_BRANCH, D_MODEL))
    merged = gates[..., 0, :] * att_o + gates[..., 1, :] * conv_o + gates[..., 2, :] * pool_o
    return merged @ w_out[l], new_k, new_v, new_conv, new_pool


def swiglu(x, wg, wu, wd):
    return (jax.nn.silu(x @ wg) * (x @ wu)) @ wd


def moe_swiglu(x, router, wg, wu, wd):
    B, T, D = x.shape
    xt = x.reshape(B * T, D)
    logits = (xt @ router).astype(jnp.float32)
    top_v, top_i = lax.top_k(logits, TOP_K)
    top_w = jax.nn.softmax(top_v, axis=-1)
    gate = jnp.sum(jax.nn.one_hot(top_i, N_EXPERTS, dtype=jnp.float32) * top_w[..., None], axis=1)
    y = jnp.zeros((B * T, D), jnp.float32)
    for e in range(N_EXPERTS):
        y = y + gate[:, e:e + 1] * swiglu(xt, wg[e], wu[e], wd[e]).astype(jnp.float32)
    return y.astype(x.dtype).reshape(B, T, D)


def run_trunk(x, win_k, win_v, conv_buf, pool_buf, mp, fp, norm_final):
    (norm_ffn, ffn_w_gate, ffn_w_up, ffn_w_down, moe_router, moe_w_gate, moe_w_up, moe_w_down) = fp
    B = x.shape[0]
    ks, vs, cs, ps = [], [], [], []
    for l in range(DEPTH):
        if win_k is None:
            lk, lv = None, None
            cp = jnp.zeros((B, CONV_WIDTH - 1, D_CONV), x.dtype)
            pp = jnp.zeros((B, 0, D_POOL), x.dtype)
        else:
            lk, lv, cp, pp = win_k[l], win_v[l], conv_buf[l], pool_buf[l]
        mix, nk, nv, nc, npool = mixing_layer(x, l, lk, lv, cp, pp, mp)
        x = x + mix
        hn = rms_norm(x, norm_ffn[l])
        if l % 2 == 0:
            i = l // 2
            x = x + swiglu(hn, ffn_w_gate[i], ffn_w_up[i], ffn_w_down[i])
        else:
            i = l // 2
            x = x + moe_swiglu(hn, moe_router[i], moe_w_gate[i], moe_w_up[i], moe_w_down[i])
        ks.append(nk)
        vs.append(nv)
        cs.append(nc)
        ps.append(npool)
    y = rms_norm(x, norm_final)
    return y, jnp.stack(ks), jnp.stack(vs), jnp.stack(cs), jnp.stack(ps)


def setup_inputs(seed: int = 0) -> dict:
    key = jax.random.key(seed)
    kk = jax.random.split(key, 24)
    f32 = jnp.float32
    win_buf = min(WINDOW, PAST_LEN)

    def nrm(k, shape, scale):
        return jax.random.normal(k, shape, f32) * scale

    return {
        'x_prompt': nrm(kk[0], (BATCH, SEQ, D_MODEL), 1.0),
        'x_sample': nrm(kk[1], (DEC_BATCH, DEC_SEQ, D_MODEL), 1.0),
        'cache_win_k': nrm(kk[2], (DEPTH, DEC_BATCH, win_buf, N_KV_HEADS, HEAD_DIM), 1.0),
        'cache_win_v': nrm(kk[3], (DEPTH, DEC_BATCH, win_buf, N_KV_HEADS, HEAD_DIM), 1.0),
        'state_conv': nrm(kk[4], (DEPTH, DEC_BATCH, CONV_WIDTH - 1, D_CONV), 1.0),
        'state_pool': nrm(kk[5], (DEPTH, DEC_BATCH, POOL_BUF, D_POOL), 1.0),
        'norm_mix': 1.0 + nrm(kk[6], (DEPTH, D_MODEL), 0.02),
        'w_in': nrm(kk[7], (DEPTH, D_MODEL, D_IN), D_MODEL ** -0.5),
        'attn_sinks': nrm(kk[8], (DEPTH, N_Q_HEADS), 0.5),
        'w_attn_out': nrm(kk[9], (DEPTH, D_ATTN, D_MODEL), D_ATTN ** -0.5),
        'conv_w': nrm(kk[10], (DEPTH, CONV_WIDTH, D_CONV), CONV_WIDTH ** -0.5),
        'w_conv_out': nrm(kk[11], (DEPTH, D_CONV, D_MODEL), D_CONV ** -0.5),
        'w_pool': nrm(kk[12], (DEPTH, N_POOL_GROUPS, POOL_GROUP, POOL_OUT_GROUP), POOL_GROUP ** -0.5),
        'pool_scale': 1.0 + nrm(kk[13], (DEPTH, D_MODEL), 0.02),
        'w_out': nrm(kk[14], (DEPTH, D_MODEL, D_MODEL), D_MODEL ** -0.5),
        'norm_ffn': 1.0 + nrm(kk[15], (DEPTH, D_MODEL), 0.02),
        'ffn_w_gate': nrm(kk[16], (N_DENSE, D_MODEL, D_FF), D_MODEL ** -0.5),
        'ffn_w_up': nrm(kk[17], (N_DENSE, D_MODEL, D_FF), D_MODEL ** -0.5),
        'ffn_w_down': nrm(kk[18], (N_DENSE, D_FF, D_MODEL), D_FF ** -0.5),
        'moe_router': nrm(kk[19], (N_MOE, D_MODEL, N_EXPERTS), D_MODEL ** -0.5),
        'moe_w_gate': nrm(kk[20], (N_MOE, N_EXPERTS, D_MODEL, D_EXPERT), D_MODEL ** -0.5),
        'moe_w_up': nrm(kk[21], (N_MOE, N_EXPERTS, D_MODEL, D_EXPERT), D_MODEL ** -0.5),
        'moe_w_down': nrm(kk[22], (N_MOE, N_EXPERTS, D_EXPERT, D_MODEL), D_EXPERT ** -0.5),
        'norm_final': 1.0 + nrm(kk[23], (D_MODEL,), 0.02),
    }


def reference(x_prompt, x_sample, cache_win_k, cache_win_v, state_conv, state_pool,
              norm_mix, w_in, attn_sinks, w_attn_out, conv_w, w_conv_out, w_pool, pool_scale, w_out,
              norm_ffn, ffn_w_gate, ffn_w_up, ffn_w_down,
              moe_router, moe_w_gate, moe_w_up, moe_w_down, norm_final):
    mp = (norm_mix, w_in, attn_sinks, w_attn_out, conv_w, w_conv_out, w_pool, pool_scale, w_out)
    fp = (norm_ffn, ffn_w_gate, ffn_w_up, ffn_w_down, moe_router, moe_w_gate, moe_w_up, moe_w_down)
    y_prompt, p_k, p_v, p_conv, p_pool = run_trunk(x_prompt, None, None, None, None, mp, fp, norm_final)
    y_sample, s_k, s_v, s_conv, s_pool = run_trunk(x_sample, cache_win_k, cache_win_v, state_conv,
                                                   state_pool, mp, fp, norm_final)
    return (y_prompt, y_sample, p_k, p_v, p_conv, p_pool, s_k, s_v, s_conv, s_pool)
```

```python
import functools

import jax
import jax.numpy as jnp
from jax import lax
from jax.experimental import pallas as pl
from jax.experimental.pallas import tpu as pltpu

F32 = jnp.float32
BF16 = jnp.bfloat16

D_MODEL = 1024
BATCH = 4
SEQ = 4096
DEPTH = 2
DEC_BATCH = 128
DEC_SEQ = 8
HEAD_DIM = 64
N_Q_HEADS = 8
N_KV_HEADS = 2
GQA_GROUP = N_Q_HEADS // N_KV_HEADS
D_ATTN = N_Q_HEADS * HEAD_DIM
D_KV = N_KV_HEADS * HEAD_DIM
WINDOW = 128
ATTN_SCALE = HEAD_DIM ** -0.5
D_CONV = D_MODEL // 2
CONV_WIDTH = 3
D_POOL = D_MODEL // 2
POOL_WINDOWS = (2, 4, 8, 16)
POOL_GROUP = D_POOL // len(POOL_WINDOWS)
POOL_OUT_GROUP = D_MODEL // len(POOL_WINDOWS)
POOL_BUF = max(POOL_WINDOWS) - 1
OFF_Q = 0
OFF_K = OFF_Q + D_ATTN
OFF_V = OFF_K + D_KV
OFF_CX = OFF_V + D_KV
OFF_CB = OFF_CX + D_CONV
OFF_CC = OFF_CB + D_CONV
OFF_P = OFF_CC + D_CONV
OFF_G = OFF_P + D_POOL
D_IN = OFF_G + 3 * D_MODEL
D_FF = 2816
N_EXPERTS = 8
D_EXPERT = 3584
EPS = 1e-5

N_PROMPT = BATCH * SEQ
N_SAMPLE = DEC_BATCH * DEC_SEQ
N_TOK = N_PROMPT + N_SAMPLE

PROMPT_TILE = 512
SAMPLE_SEQS = 32
SAMPLE_ROWS = SAMPLE_SEQS * DEC_SEQ
ATT_GROUP = 8
FFN_TILE = 1024
FF_CHUNK = D_FF // 2
EXP_CHUNK = D_EXPERT // 4
ROW_BLOCK = 128
VMEM_LIMIT = 56 * 1024 * 1024


def _dot(a, b):
    return jnp.dot(a, b, preferred_element_type=F32)


def _dot_nt(a, b):
    return lax.dot_general(a, b, (((1,), (1,)), ((), ())), preferred_element_type=F32)


def _rms(x, g):
    return x * lax.rsqrt(jnp.mean(x * x, axis=-1, keepdims=True) + EPS) * g


def _sigmoid(x):
    return 1.0 / (1.0 + jnp.exp(-x))


def _head_slope(h):
    return float(2.0 ** (-8.0 * (h + 1) / N_Q_HEADS))


def _softmax_pv(parts, sink):
    m = sink
    for s, _ in parts:
        m = jnp.maximum(m, jnp.max(s, axis=-1, keepdims=True))
    den = jnp.exp(sink - m)
    o = None
    for s, v in parts:
        p = jnp.exp(s - m)
        den = den + jnp.sum(p, axis=-1, keepdims=True)
        pv = _dot(p.astype(BF16), v)
        o = pv if o is None else o + pv
    return o / den


def _mix_tail(nb, x, att_bf, cx, cb, yc, d_groups, win_ref, wao_ref, wco_ref, wp_ref, ps_ref, wo_ref):
    att_o = _dot(att_bf, wao_ref[...])
    merged = _sigmoid(_dot(nb, win_ref[:, OFF_G:OFF_G + D_MODEL])) * att_o
    conv_o = _dot((cb * yc).astype(BF16), wco_ref[...])
    merged = merged + _sigmoid(_dot(nb, win_ref[:, OFF_G + D_MODEL:OFF_G + 2 * D_MODEL])) * conv_o
    pool_o = jnp.concatenate([_dot(d.astype(BF16), wp_ref[g]) for g, d in enumerate(d_groups)], axis=-1)
    pool_o = pool_o * ps_ref[...]
    merged = merged + _sigmoid(_dot(nb, win_ref[:, OFF_G + 2 * D_MODEL:OFF_G + 3 * D_MODEL])) * pool_o
    return x + _dot(merged.astype(BF16), wo_ref[...])


def _prompt_mix_kernel(sinks_ref, x_ref, nw_ref, win_ref, wao_ref, cw_ref, wco_ref, wp_ref, ps_ref, wo_ref,
                       h_ref, kw_ref, vw_ref, cs_ref, pst_ref,
                       qbuf, kbuf, vbuf, att_ref, ubuf, pbuf):
    tm = PROMPT_TILE
    t = pl.program_id(1)
    x = x_ref[...]
    nb = _rms(x, nw_ref[...]).astype(BF16)

    @pl.when(t == 0)
    def _():
        kbuf[0:WINDOW, :] = jnp.zeros((WINDOW, D_KV), BF16)
        vbuf[0:WINDOW, :] = jnp.zeros((WINDOW, D_KV), BF16)
        ubuf[0:8, :] = jnp.zeros((8, D_CONV), F32)
        pbuf[0:16, :] = jnp.zeros((16, D_POOL), F32)

    @pl.when(t > 0)
    def _():
        kbuf[0:WINDOW, :] = kbuf[tm:tm + WINDOW, :]
        vbuf[0:WINDOW, :] = vbuf[tm:tm + WINDOW, :]
        ubuf[0:8, :] = ubuf[tm:tm + 8, :]
        pbuf[0:16, :] = pbuf[tm:tm + 16, :]

    qkv = _dot(nb, win_ref[:, OFF_Q:OFF_CX])
    qbuf[...] = (qkv[:, OFF_Q:OFF_K] * ATTN_SCALE).astype(BF16)
    k = qkv[:, OFF_K:OFF_V]
    v = qkv[:, OFF_V:OFF_CX]
    kbuf[WINDOW:WINDOW + tm, :] = k.astype(BF16)
    vbuf[WINDOW:WINDOW + tm, :] = v.astype(BF16)
    kw_ref[...] = k[tm - WINDOW:, :]
    vw_ref[...] = v[tm - WINDOW:, :]

    qi = lax.broadcasted_iota(jnp.int32, (WINDOW, 2 * WINDOW), 0)
    si = lax.broadcasted_iota(jnp.int32, (WINDOW, 2 * WINDOW), 1)
    dist_i = qi + WINDOW - si
    dist = dist_i.astype(F32)
    band = jnp.logical_and(dist_i >= 0, dist_i <= WINDOW)

    def att_block(j, carry):
        r0 = pl.multiple_of(j * WINDOW, WINDOW)
        qj = qbuf[pl.ds(r0, WINDOW), :]
        kj = kbuf[pl.ds(r0, 2 * WINDOW), :]
        vj = vbuf[pl.ds(r0, 2 * WINDOW), :]
        first_key = jnp.where(jnp.logical_and(t == 0, j == 0), WINDOW, 0)
        valid = jnp.logical_and(band, si >= first_key)
        outs = []
        for h in range(N_Q_HEADS):
            kv = h // GQA_GROUP
            s = _dot_nt(qj[:, h * HEAD_DIM:(h + 1) * HEAD_DIM], kj[:, kv * HEAD_DIM:(kv + 1) * HEAD_DIM])
            s = jnp.where(valid, s - _head_slope(h) * dist, -jnp.inf)
            outs.append(_softmax_pv([(s, vj[:, kv * HEAD_DIM:(kv + 1) * HEAD_DIM])], sinks_ref[h]))
        att_ref[pl.ds(r0, WINDOW), :] = jnp.concatenate(outs, axis=-1).astype(BF16)
        return carry

    lax.fori_loop(0, tm // WINDOW, att_block, 0)

    conv = _dot(nb, win_ref[:, OFF_CX:OFF_P])
    cx = conv[:, 0:D_CONV]
    cb = conv[:, D_CONV:2 * D_CONV]
    ubuf[8:8 + tm, :] = conv[:, 2 * D_CONV:3 * D_CONV] * cx
    yc = (cw_ref[0:1, :] * ubuf[6:6 + tm, :] + cw_ref[1:2, :] * ubuf[7:7 + tm, :]
          + cw_ref[2:3, :] * ubuf[8:8 + tm, :])
    cs_ref[...] = ubuf[tm + 6:tm + 8, :]

    pbuf[16:16 + tm, :] = _dot(nb, win_ref[:, OFF_P:OFF_G])
    pos = lax.broadcasted_iota(jnp.int32, (tm, 1), 0) + t * tm + 1
    d_groups = []
    for g, w in enumerate(POOL_WINDOWS):
        c0 = g * POOL_GROUP
        cur = pbuf[16:16 + tm, c0:c0 + POOL_GROUP]
        tot = cur
        for j in range(1, w):
            tot = tot + pbuf[16 - j:16 - j + tm, c0:c0 + POOL_GROUP]
        cnt = jnp.minimum(pos, w).astype(F32)
        d_groups.append(tot / cnt - cur)
    pst_ref[...] = pbuf[tm + 1:tm + 16, :]

    h_ref[...] = _mix_tail(nb, x, att_ref[...], cx, cb, yc, d_groups,
                           win_ref, wao_ref, wco_ref, wp_ref, ps_ref, wo_ref)


def _weight_specs(l, grid_rank):
    def const(*idx):
        if grid_rank == 1:
            return lambda i: idx
        return lambda b, t: idx

    return [
        pl.BlockSpec((None, 1, D_MODEL), const(l, 0, 0)),
        pl.BlockSpec((None, D_MODEL, D_IN), const(l, 0, 0)),
        pl.BlockSpec((None, D_ATTN, D_MODEL), const(l, 0, 0)),
        pl.BlockSpec((None, CONV_WIDTH, D_CONV), const(l, 0, 0)),
        pl.BlockSpec((None, D_CONV, D_MODEL), const(l, 0, 0)),
        pl.BlockSpec((None, len(POOL_WINDOWS), POOL_GROUP, POOL_OUT_GROUP), const(l, 0, 0, 0)),
        pl.BlockSpec((None, 1, D_MODEL), const(l, 0, 0)),
        pl.BlockSpec((None, D_MODEL, D_MODEL), const(l, 0, 0)),
    ]


def _prompt_mix(l, x, sinks, weights):
    tm = PROMPT_TILE
    nt = SEQ // tm
    in_specs = [pl.BlockSpec(memory_space=pltpu.SMEM),
                pl.BlockSpec((tm, D_MODEL), lambda b, t: (b * nt + t, 0))] + _weight_specs(l, 2)
    out_specs = [
        pl.BlockSpec((tm, D_MODEL), lambda b, t: (b * nt + t, 0)),
        pl.BlockSpec((None, WINDOW, D_KV), lambda b, t: (b, 0, 0)),
        pl.BlockSpec((None, WINDOW, D_KV), lambda b, t: (b, 0, 0)),
        pl.BlockSpec((None, CONV_WIDTH - 1, D_CONV), lambda b, t: (b, 0, 0)),
        pl.BlockSpec((None, POOL_BUF, D_POOL), lambda b, t: (b, 0, 0)),
    ]
    out_shape = [
        jax.ShapeDtypeStruct((N_PROMPT, D_MODEL), F32),
        jax.ShapeDtypeStruct((BATCH, WINDOW, D_KV), F32),
        jax.ShapeDtypeStruct((BATCH, WINDOW, D_KV), F32),
        jax.ShapeDtypeStruct((BATCH, CONV_WIDTH - 1, D_CONV), F32),
        jax.ShapeDtypeStruct((BATCH, POOL_BUF, D_POOL), F32),
    ]
    scratch = [
        pltpu.VMEM((tm, D_ATTN), BF16),
        pltpu.VMEM((WINDOW + tm, D_KV), BF16),
        pltpu.VMEM((WINDOW + tm, D_KV), BF16),
        pltpu.VMEM((tm, D_ATTN), BF16),
        pltpu.VMEM((8 + tm, D_CONV), F32),
        pltpu.VMEM((16 + tm, D_POOL), F32),
    ]
    return pl.pallas_call(
        _prompt_mix_kernel,
        grid=(BATCH, nt),
        in_specs=in_specs,
        out_specs=out_specs,
        out_shape=out_shape,
        scratch_shapes=scratch,
        compiler_params=pltpu.CompilerParams(
            dimension_semantics=("arbitrary", "arbitrary"), vmem_limit_bytes=VMEM_LIMIT),
        name=f"prompt_mix_l{l}",
    )(sinks, x, *weights)


def _sample_mix_kernel(sinks_ref, x_ref, kc_ref, vc_ref, cst_ref, pin_ref,
                       nw_ref, win_ref, wao_ref, cw_ref, wco_ref, wp_ref, ps_ref, wo_ref,
                       h_ref, ko_ref, vo_ref, co_ref, po_ref,
                       qbuf, knb, vnb, att_ref, cbuf, ebuf):
    ns, nr, T = SAMPLE_SEQS, SAMPLE_ROWS, DEC_SEQ
    x = x_ref[...]
    nb = _rms(x, nw_ref[...]).astype(BF16)

    qkv = _dot(nb, win_ref[:, OFF_Q:OFF_CX])
    qbuf[...] = (qkv[:, OFF_Q:OFF_K] * ATTN_SCALE).astype(BF16)
    k = qkv[:, OFF_K:OFF_V]
    v = qkv[:, OFF_V:OFF_CX]
    knb[...] = k.astype(BF16)
    vnb[...] = v.astype(BF16)
    ko_ref[:, 0:WINDOW - T, :] = kc_ref[:, T:WINDOW, :]
    vo_ref[:, 0:WINDOW - T, :] = vc_ref[:, T:WINDOW, :]
    ko_ref[:, WINDOW - T:WINDOW, :] = k.reshape(ns, T, D_KV)
    vo_ref[:, WINDOW - T:WINDOW, :] = v.reshape(ns, T, D_KV)

    gr = ATT_GROUP * T
    gc = ATT_GROUP * WINDOW
    rq = lax.broadcasted_iota(jnp.int32, (gr, gc), 0)
    cq = lax.broadcasted_iota(jnp.int32, (gr, gc), 1)
    tq = rq & (T - 1)
    sc_pos = cq & (WINDOW - 1)
    valid_c = jnp.logical_and((rq >> 3) == (cq >> 7), sc_pos >= tq)
    dist_c = (WINDOW + tq - sc_pos).astype(F32)
    rn = lax.broadcasted_iota(jnp.int32, (gr, gr), 0)
    cn = lax.broadcasted_iota(jnp.int32, (gr, gr), 1)
    tn = rn & (T - 1)
    jn = cn & (T - 1)
    valid_n = jnp.logical_and((rn >> 3) == (cn >> 3), jn <= tn)
    dist_n = (tn - jn).astype(F32)

    def att_group(gi, carry):
        r0 = pl.multiple_of(gi * gr, gr)
        s0 = pl.multiple_of(gi * ATT_GROUP, ATT_GROUP)
        qg = qbuf[pl.ds(r0, gr), :]
        kcg = kc_ref[pl.ds(s0, ATT_GROUP), :, :].reshape(gc, D_KV).astype(BF16)
        vcg = vc_ref[pl.ds(s0, ATT_GROUP), :, :].reshape(gc, D_KV).astype(BF16)
        kng = knb[pl.ds(r0, gr), :]
        vng = vnb[pl.ds(r0, gr), :]
        outs = []
        for h in range(N_Q_HEADS):
            kv = h // GQA_GROUP
            lo, hi = kv * HEAD_DIM, (kv + 1) * HEAD_DIM
            qh = qg[:, h * HEAD_DIM:(h + 1) * HEAD_DIM]
            slope = _head_slope(h)
            s_c = jnp.where(valid_c, _dot_nt(qh, kcg[:, lo:hi]) - slope * dist_c, -jnp.inf)
            s_n = jnp.where(valid_n, _dot_nt(qh, kng[:, lo:hi]) - slope * dist_n, -jnp.inf)
            outs.append(_softmax_pv([(s_c, vcg[:, lo:hi]), (s_n, vng[:, lo:hi])], sinks_ref[h]))
        att_ref[pl.ds(r0, gr), :] = jnp.concatenate(outs, axis=-1).astype(BF16)
        return carry

    lax.fori_loop(0, ns // ATT_GROUP, att_group, 0)

    conv = _dot(nb, win_ref[:, OFF_CX:OFF_P])
    cx = conv[:, 0:D_CONV]
    cb = conv[:, D_CONV:2 * D_CONV]
    u = conv[:, 2 * D_CONV:3 * D_CONV] * cx
    cbuf[:, 6:8, :] = cst_ref[...]
    cbuf[:, 8:16, :] = u.reshape(ns, T, D_CONV)
    w0 = cw_ref[0:1, :].reshape(1, 1, D_CONV)
    w1 = cw_ref[1:2, :].reshape(1, 1, D_CONV)
    w2 = cw_ref[2:3, :].reshape(1, 1, D_CONV)
    yc = (w0 * cbuf[:, 6:14, :] + w1 * cbuf[:, 7:15, :] + w2 * cbuf[:, 8:16, :]).reshape(nr, D_CONV)
    co_ref[...] = cbuf[:, 14:16, :]

    ebuf[:, 1:16, :] = pin_ref[...]
    ebuf[:, 16:24, :] = _dot(nb, win_ref[:, OFF_P:OFF_G]).reshape(ns, T, D_POOL)
    d_groups = []
    for g, w in enumerate(POOL_WINDOWS):
        c0 = g * POOL_GROUP
        cur = ebuf[:, 16:24, c0:c0 + POOL_GROUP]
        tot = cur
        for j in range(1, w):
            tot = tot + ebuf[:, 16 - j:24 - j, c0:c0 + POOL_GROUP]
        d_groups.append((tot / float(w) - cur).reshape(nr, POOL_GROUP))
    po_ref[...] = ebuf[:, 9:24, :]

    h_ref[...] = _mix_tail(nb, x, att_ref[...], cx, cb, yc, d_groups,
                           win_ref, wao_ref, wco_ref, wp_ref, ps_ref, wo_ref)


def _sample_mix(l, x, sinks, kc, vc, cst, pst, weights):
    ns, nr = SAMPLE_SEQS, SAMPLE_ROWS
    in_specs = [
        pl.BlockSpec(memory_space=pltpu.SMEM),
        pl.BlockSpec((nr, D_MODEL), lambda i: (i, 0)),
        pl.BlockSpec((None, ns, WINDOW, D_KV), lambda i: (l, i, 0, 0)),
        pl.BlockSpec((None, ns, WINDOW, D_KV), lambda i: (l, i, 0, 0)),
        pl.BlockSpec((None, ns, CONV_WIDTH - 1, D_CONV), lambda i: (l, i, 0, 0)),
        pl.BlockSpec((None, ns, POOL_BUF, D_POOL), lambda i: (l, i, 0, 0)),
    ] + _weight_specs(l, 1)
    out_specs = [
        pl.BlockSpec((nr, D_MODEL), lambda i: (i, 0)),
        pl.BlockSpec((ns, WINDOW, D_KV), lambda i: (i, 0, 0)),
        pl.BlockSpec((ns, WINDOW, D_KV), lambda i: (i, 0, 0)),
        pl.BlockSpec((ns, CONV_WIDTH - 1, D_CONV), lambda i: (i, 0, 0)),
        pl.BlockSpec((ns, POOL_BUF, D_POOL), lambda i: (i, 0, 0)),
    ]
    out_shape = [
        jax.ShapeDtypeStruct((N_SAMPLE, D_MODEL), F32),
        jax.ShapeDtypeStruct((DEC_BATCH, WINDOW, D_KV), F32),
        jax.ShapeDtypeStruct((DEC_BATCH, WINDOW, D_KV), F32),
        jax.ShapeDtypeStruct((DEC_BATCH, CONV_WIDTH - 1, D_CONV), F32),
        jax.ShapeDtypeStruct((DEC_BATCH, POOL_BUF, D_POOL), F32),
    ]
    scratch = [
        pltpu.VMEM((nr, D_ATTN), BF16),
        pltpu.VMEM((nr, D_KV), BF16),
        pltpu.VMEM((nr, D_KV), BF16),
        pltpu.VMEM((nr, D_ATTN), BF16),
        pltpu.VMEM((ns, 16, D_CONV), F32),
        pltpu.VMEM((ns, 24, D_POOL), F32),
    ]
    return pl.pallas_call(
        _sample_mix_kernel,
        grid=(DEC_BATCH // ns,),
        in_specs=in_specs,
        out_specs=out_specs,
        out_shape=out_shape,
        scratch_shapes=scratch,
        compiler_params=pltpu.CompilerParams(
            dimension_semantics=("arbitrary",), vmem_limit_bytes=VMEM_LIMIT),
        name=f"sample_mix_l{l}",
    )(sinks, x, kc, vc, cst, pst, *weights)


def _ffn_kernel(h_ref, nw_ref, wg_ref, wu_ref, wd_ref, o_ref, hn_ref, acc_ref):
    c = pl.program_id(1)

    @pl.when(c == 0)
    def _():
        hn_ref[...] = _rms(h_ref[...], nw_ref[...]).astype(BF16)
        acc_ref[...] = jnp.zeros_like(acc_ref)

    hn = hn_ref[...]
    g = _dot(hn, wg_ref[...])
    u = _dot(hn, wu_ref[...])
    acc_ref[...] += _dot((g * _sigmoid(g) * u).astype(BF16), wd_ref[...])

    @pl.when(c == pl.num_programs(1) - 1)
    def _():
        o_ref[...] = h_ref[...] + acc_ref[...]


def _ffn_dense(h, nw, wg, wu, wd):
    tm = FFN_TILE
    nc = D_FF // FF_CHUNK
    return pl.pallas_call(
        _ffn_kernel,
        grid=(N_TOK // tm, nc),
        in_specs=[
            pl.BlockSpec((tm, D_MODEL), lambda i, c: (i, 0)),
            pl.BlockSpec((1, D_MODEL), lambda i, c: (0, 0)),
            pl.BlockSpec((None, D_MODEL, FF_CHUNK), lambda i, c: (0, 0, c)),
            pl.BlockSpec((None, D_MODEL, FF_CHUNK), lambda i, c: (0, 0, c)),
            pl.BlockSpec((None, FF_CHUNK, D_MODEL), lambda i, c: (0, c, 0)),
        ],
        out_specs=pl.BlockSpec((tm, D_MODEL), lambda i, c: (i, 0)),
        out_shape=jax.ShapeDtypeStruct((N_TOK, D_MODEL), F32),
        scratch_shapes=[pltpu.VMEM((tm, D_MODEL), BF16), pltpu.VMEM((tm, D_MODEL), F32)],
        compiler_params=pltpu.CompilerParams(
            dimension_semantics=("arbitrary", "arbitrary"), vmem_limit_bytes=VMEM_LIMIT),
        name="ffn_dense",
    )(h, nw, wg, wu, wd)


def _router_kernel(h_ref, nw_ref, rt_ref, hn_ref, slot_ref, gate_ref, cnt_ref):
    tm = FFN_TILE
    hn = _rms(h_ref[...], nw_ref[...])
    hn_ref[...] = hn.astype(BF16)
    logits = lax.dot_general(rt_ref[...], hn, (((1,), (1,)), ((), ())),
                             precision=lax.Precision.HIGHEST, preferred_element_type=F32)
    eidx = lax.broadcasted_iota(jnp.int32, (N_EXPERTS, tm), 0).astype(F32)
    none = float(N_EXPERTS)
    m1 = jnp.max(logits, axis=0, keepdims=True)
    i1 = jnp.min(jnp.where(logits == m1, eidx, none), axis=0, keepdims=True)
    rest = jnp.where(eidx == i1, -jnp.inf, logits)
    m2 = jnp.max(rest, axis=0, keepdims=True)
    i2 = jnp.min(jnp.where(rest == m2, eidx, none), axis=0, keepdims=True)
    e2 = jnp.exp(m2 - m1)
    w1 = 1.0 / (1.0 + e2)
    w2 = e2 / (1.0 + e2)
    sel1 = eidx == i1
    sel2 = eidx == i2
    gate_ref[...] = jnp.where(sel1, w1, jnp.where(sel2, w2, 0.0))
    chosen = jnp.logical_or(sel1, sel2)
    mask = jnp.where(chosen, 1.0, 0.0)
    srow = lax.broadcasted_iota(jnp.int32, (tm, tm), 0)
    scol = lax.broadcasted_iota(jnp.int32, (tm, tm), 1)
    upper = jnp.where(srow < scol, 1.0, 0.0).astype(BF16)
    mask16 = jnp.concatenate([mask, jnp.zeros_like(mask)], axis=0).astype(BF16)
    before = _dot(mask16, upper)[0:N_EXPERTS, :]
    slot_ref[...] = jnp.where(chosen, before.astype(jnp.int32), -1)
    cnt_ref[...] = jnp.broadcast_to(jnp.sum(mask, axis=1, keepdims=True), (N_EXPERTS, 128))


def _router(h, nw, router_t):
    tm = FFN_TILE
    nt = N_TOK // tm
    return pl.pallas_call(
        _router_kernel,
        grid=(nt,),
        in_specs=[
            pl.BlockSpec((tm, D_MODEL), lambda i: (i, 0)),
            pl.BlockSpec((1, D_MODEL), lambda i: (0, 0)),
            pl.BlockSpec((N_EXPERTS, D_MODEL), lambda i: (0, 0)),
        ],
        out_specs=[
            pl.BlockSpec((tm, D_MODEL), lambda i: (i, 0)),
            pl.BlockSpec((N_EXPERTS, tm), lambda i: (0, i)),
            pl.BlockSpec((N_EXPERTS, tm), lambda i: (0, i)),
            pl.BlockSpec((None, N_EXPERTS, 128), lambda i: (i, 0, 0)),
        ],
        out_shape=[
            jax.ShapeDtypeStruct((N_TOK, D_MODEL), BF16),
            jax.ShapeDtypeStruct((N_EXPERTS, N_TOK), jnp.int32),
            jax.ShapeDtypeStruct((N_EXPERTS, N_TOK), F32),
            jax.ShapeDtypeStruct((nt, N_EXPERTS, 128), F32),
        ],
        compiler_params=pltpu.CompilerParams(
            dimension_semantics=("arbitrary",), vmem_limit_bytes=VMEM_LIMIT),
        name="moe_router",
    )(h, nw, router_t)


def _moe_kernel(nblk_ref, hn_ref, h_ref, srow_ref, scol_ref, gcol_ref, wg_ref, wu_ref, wd_ref, nf_ref,
                o_ref, xs_ref, ys_ref, acc_ref):
    tm, rb = FFN_TILE, ROW_BLOCK
    t, e, c = pl.program_id(0), pl.program_id(1), pl.program_id(2)
    nc = pl.num_programs(2)
    nb = nblk_ref[t * N_EXPERTS + e]

    @pl.when(jnp.logical_and(e == 0, c == 0))
    def _():
        acc_ref[...] = jnp.zeros_like(acc_ref)

    @pl.when(c == 0)
    def _():
        slot_row = srow_ref[...]

        def gather(b, carry):
            r0 = pl.multiple_of(b * rb, rb)
            rid = lax.broadcasted_iota(jnp.int32, (rb, tm), 0) + b * rb
            onehot = jnp.where(slot_row == rid, 1.0, 0.0).astype(BF16)
            xs_ref[pl.ds(r0, rb), :] = _dot(onehot, hn_ref[...]).astype(BF16)
            ys_ref[pl.ds(r0, rb), :] = jnp.zeros((rb, D_MODEL), F32)
            return carry

        lax.fori_loop(0, nb, gather, 0)

    def expert(b, carry):
        r0 = pl.multiple_of(b * rb, rb)
        xb = xs_ref[pl.ds(r0, rb), :]
        g = _dot(xb, wg_ref[...])
        u = _dot(xb, wu_ref[...])
        ys_ref[pl.ds(r0, rb), :] += _dot((g * _sigmoid(g) * u).astype(BF16), wd_ref[...])
        return carry

    lax.fori_loop(0, nb, expert, 0)

    @pl.when(c == nc - 1)
    def _():
        lane = lax.broadcasted_iota(jnp.int32, (tm, N_EXPERTS), 1)
        slot_col = jnp.sum(jnp.where(lane == e, scol_ref[...], 0.0), axis=1, keepdims=True)
        gate_col = jnp.sum(jnp.where(lane == e, gcol_ref[...], 0.0), axis=1, keepdims=True)

        def scatter(b, carry):
            r0 = pl.multiple_of(b * rb, rb)
            cid = (lax.broadcasted_iota(jnp.int32, (tm, rb), 1) + b * rb).astype(F32)
            weighted = jnp.where(slot_col == cid, gate_col, 0.0).astype(BF16)
            acc_ref[...] += _dot(weighted, ys_ref[pl.ds(r0, rb), :].astype(BF16))
            return carry

        lax.fori_loop(0, nb, scatter, 0)

    @pl.when(jnp.logical_and(e == N_EXPERTS - 1, c == nc - 1))
    def _():
        o_ref[...] = _rms(h_ref[...] + acc_ref[...], nf_ref[...])


def _moe(nblk, hn, h, slot_row, slot_col, gate_col, wg, wu, wd, nf):
    tm = FFN_TILE
    nt = N_TOK // tm
    nc = D_EXPERT // EXP_CHUNK
    grid_spec = pltpu.PrefetchScalarGridSpec(
        num_scalar_prefetch=1,
        grid=(nt, N_EXPERTS, nc),
        in_specs=[
            pl.BlockSpec((tm, D_MODEL), lambda t, e, c, nb: (t, 0)),
            pl.BlockSpec((tm, D_MODEL), lambda t, e, c, nb: (t, 0)),
            pl.BlockSpec((None, 1, tm), lambda t, e, c, nb: (e, 0, t)),
            pl.BlockSpec((tm, N_EXPERTS), lambda t, e, c, nb: (t, 0)),
            pl.BlockSpec((tm, N_EXPERTS), lambda t, e, c, nb: (t, 0)),
            pl.BlockSpec((None, None, D_MODEL, EXP_CHUNK), lambda t, e, c, nb: (0, e, 0, c)),
            pl.BlockSpec((None, None, D_MODEL, EXP_CHUNK), lambda t, e, c, nb: (0, e, 0, c)),
            pl.BlockSpec((None, None, EXP_CHUNK, D_MODEL), lambda t, e, c, nb: (0, e, c, 0)),
            pl.BlockSpec((1, D_MODEL), lambda t, e, c, nb: (0, 0)),
        ],
        out_specs=pl.BlockSpec((tm, D_MODEL), lambda t, e, c, nb: (t, 0)),
        scratch_shapes=[
            pltpu.VMEM((tm, D_MODEL), BF16),
            pltpu.VMEM((tm, D_MODEL), F32),
            pltpu.VMEM((tm, D_MODEL), F32),
        ],
    )
    return pl.pallas_call(
        _moe_kernel,
        grid_spec=grid_spec,
        out_shape=jax.ShapeDtypeStruct((N_TOK, D_MODEL), F32),
        compiler_params=pltpu.CompilerParams(
            dimension_semantics=("arbitrary", "arbitrary", "arbitrary"), vmem_limit_bytes=VMEM_LIMIT),
        name="moe_experts",
    )(nblk, hn, h, slot_row, slot_col, gate_col, wg, wu, wd, nf)


def kernel(x_prompt, x_sample, cache_win_k, cache_win_v, state_conv, state_pool, norm_mix, w_in, attn_sinks,
           w_attn_out, conv_w, w_conv_out, w_pool, pool_scale, w_out, norm_ffn, ffn_w_gate, ffn_w_up, ffn_w_down,
           moe_router, moe_w_gate, moe_w_up, moe_w_down, norm_final):
    mix_weights = (norm_mix.reshape(DEPTH, 1, D_MODEL), w_in.astype(BF16), w_attn_out.astype(BF16), conv_w,
                   w_conv_out.astype(BF16), w_pool.astype(BF16), pool_scale.reshape(DEPTH, 1, D_MODEL),
                   w_out.astype(BF16))
    kc = cache_win_k.reshape(DEPTH, DEC_BATCH, WINDOW, D_KV)
    vc = cache_win_v.reshape(DEPTH, DEC_BATCH, WINDOW, D_KV)

    xp = x_prompt.reshape(N_PROMPT, D_MODEL)
    xs = x_sample.reshape(N_SAMPLE, D_MODEL)
    states = []
    y = None
    for l in range(DEPTH):
        sinks = attn_sinks[l]
        hp, pk, pv, pc, pp = _prompt_mix(l, xp, sinks, mix_weights)
        hs, sk, sv, sc, sp = _sample_mix(l, xs, sinks, kc, vc, state_conv, state_pool, mix_weights)
        states.append((pk, pv, pc, pp, sk, sv, sc, sp))
        h = jnp.concatenate([hp, hs], axis=0)
        if l % 2 == 0:
            i = l // 2
            xn = _ffn_dense(h, norm_ffn[l].reshape(1, D_MODEL), ffn_w_gate[i:i + 1].astype(BF16),
                            ffn_w_up[i:i + 1].astype(BF16), ffn_w_down[i:i + 1].astype(BF16))
            xp, xs = xn[:N_PROMPT], xn[N_PROMPT:]
        else:
            i = l // 2
            hn, slot_row, gate_row, cnt = _router(h, norm_ffn[l].reshape(1, D_MODEL), moe_router[i].T)
            nblk = ((cnt[:, :, 0].astype(jnp.int32) + ROW_BLOCK - 1) // ROW_BLOCK).reshape(-1)
            y = _moe(nblk, hn, h, slot_row.reshape(N_EXPERTS, 1, N_TOK), slot_row.T.astype(F32), gate_row.T,
                     moe_w_gate[i:i + 1].astype(BF16), moe_w_up[i:i + 1].astype(BF16),
                     moe_w_down[i:i + 1].astype(BF16), norm_final.reshape(1, D_MODEL))

    pk, pv, pc, pp, sk, sv, sc, sp = (jnp.stack(a) for a in zip(*states))
    kv_shape_p = (DEPTH, BATCH, WINDOW, N_KV_HEADS, HEAD_DIM)
    kv_shape_s = (DEPTH, DEC_BATCH, WINDOW, N_KV_HEADS, HEAD_DIM)
    return (y[:N_PROMPT].reshape(BATCH, SEQ, D_MODEL), y[N_PROMPT:].reshape(DEC_BATCH, DEC_SEQ, D_MODEL),
            pk.reshape(kv_shape_p), pv.reshape(kv_shape_p), pc, pp,
            sk.reshape(kv_shape_s), sv.reshape(kv_shape_s), sc, sp)
```

```python
import functools
import math

import jax
import jax.numpy as jnp
from jax import lax
from jax.experimental import pallas as pl
from jax.experimental.pallas import tpu as pltpu

F32 = jnp.float32
BF16 = jnp.bfloat16

D_MODEL = 1024
BATCH = 4
SEQ = 4096
DEPTH = 2
DEC_BATCH = 128
DEC_SEQ = 8
HEAD_DIM = 64
N_Q_HEADS = 8
N_KV_HEADS = 2
GQA_GROUP = N_Q_HEADS // N_KV_HEADS
D_ATTN = N_Q_HEADS * HEAD_DIM
D_KV = N_KV_HEADS * HEAD_DIM
WINDOW = 128
ATTN_SCALE = HEAD_DIM ** -0.5
D_CONV = D_MODEL // 2
CONV_WIDTH = 3
D_POOL = D_MODEL // 2
POOL_WINDOWS = (2, 4, 8, 16)
POOL_GROUP = D_POOL // len(POOL_WINDOWS)
POOL_OUT_GROUP = D_MODEL // len(POOL_WINDOWS)
POOL_BUF = max(POOL_WINDOWS) - 1
OFF_Q = 0
OFF_K = OFF_Q + D_ATTN
OFF_V = OFF_K + D_KV
OFF_CX = OFF_V + D_KV
OFF_CB = OFF_CX + D_CONV
OFF_CC = OFF_CB + D_CONV
OFF_P = OFF_CC + D_CONV
OFF_G = OFF_P + D_POOL
D_IN = OFF_G + 3 * D_MODEL
D_FF = 2816
N_EXPERTS = 8
D_EXPERT = 3584
EPS = 1e-5
LOG2E = math.log2(math.e)

N_PROMPT = BATCH * SEQ
N_SAMPLE = DEC_BATCH * DEC_SEQ
N_TOK = N_PROMPT + N_SAMPLE

PROMPT_TILE = 512
SAMPLE_SEQS = 32
SAMPLE_ROWS = SAMPLE_SEQS * DEC_SEQ
ATT_GROUP = 8
FFN_TILE = 1024
FF_CHUNK = D_FF // 2
MOE_SUBS = 2
MOE_TILE = MOE_SUBS * FFN_TILE
EXP_CHUNK = D_EXPERT // 4
ROW_BLOCK = 128
ROW_ALIGN = 16
N_PAD = -(-N_TOK // MOE_TILE) * MOE_TILE
N_META = 8
VMEM_LIMIT = 58 * 1024 * 1024


def _dot(a, b):
    return jnp.dot(a, b, preferred_element_type=F32)


def _dot_nt(a, b):
    return lax.dot_general(a, b, (((1,), (1,)), ((), ())), preferred_element_type=F32)


def _rms(x, g):
    return x * lax.rsqrt(jnp.mean(x * x, axis=-1, keepdims=True) + EPS) * g


def _sigmoid(x):
    return 1.0 / (1.0 + jnp.exp(-x))


def _head_slope(h):
    return float(2.0 ** (-8.0 * (h + 1) / N_Q_HEADS))


def _softmax_pv(parts, sink):
    m = sink
    for s, _ in parts:
        m = jnp.maximum(m, jnp.max(s, axis=-1, keepdims=True))
    den = jnp.exp(sink - m)
    o = None
    for s, v in parts:
        p = jnp.exp(s - m)
        den = den + jnp.sum(p, axis=-1, keepdims=True)
        pv = _dot(p.astype(BF16), v)
        o = pv if o is None else o + pv
    return o / den


def _mix_tail(nb, x, att_bf, cx, cb, yc, d_groups, win_ref, wao_ref, wco_ref, wp_ref, ps_ref, wo_ref):
    att_o = _dot(att_bf, wao_ref[...])
    merged = _sigmoid(_dot(nb, win_ref[:, OFF_G:OFF_G + D_MODEL])) * att_o
    conv_o = _dot((cb * yc).astype(BF16), wco_ref[...])
    merged = merged + _sigmoid(_dot(nb, win_ref[:, OFF_G + D_MODEL:OFF_G + 2 * D_MODEL])) * conv_o
    pool_o = jnp.concatenate([_dot(d.astype(BF16), wp_ref[g]) for g, d in enumerate(d_groups)], axis=-1)
    pool_o = pool_o * ps_ref[...]
    merged = merged + _sigmoid(_dot(nb, win_ref[:, OFF_G + 2 * D_MODEL:OFF_G + 3 * D_MODEL])) * pool_o
    return x + _dot(merged.astype(BF16), wo_ref[...])


def _prompt_mix_kernel(*refs, n_alias):
    (sinks_ref, x_ref, nw_ref, win_ref, wao_ref, cw_ref, wco_ref, wp_ref, ps_ref, wo_ref) = refs[:10]
    (h_ref, kw_ref, vw_ref, cs_ref, pst_ref,
     qbuf, kbuf, vbuf, att_ref, ubuf, pbuf, bias_ref) = refs[10 + n_alias:]
    tm = PROMPT_TILE
    b = pl.program_id(0)
    t = pl.program_id(1)
    x = x_ref[...]
    nb = _rms(x, nw_ref[...]).astype(BF16)
    lane = lax.broadcasted_iota(jnp.int32, (tm, D_KV), 1)
    low = lane < HEAD_DIM

    @pl.when(jnp.logical_and(b == 0, t == 0))
    def _():
        qi = lax.broadcasted_iota(jnp.int32, (WINDOW, 2 * WINDOW), 0)
        si = lax.broadcasted_iota(jnp.int32, (WINDOW, 2 * WINDOW), 1)
        dist_i = qi + WINDOW - si
        dist = dist_i.astype(F32)
        band = jnp.logical_and(dist_i >= 0, dist_i <= WINDOW)
        for h in range(N_Q_HEADS):
            bias = jnp.where(band, (-_head_slope(h) * LOG2E) * dist, -jnp.inf)
            bias_ref[0, h] = bias
            bias_ref[1, h] = jnp.where(si >= WINDOW, bias, -jnp.inf)

    @pl.when(t == 0)
    def _():
        for i in range(4):
            kbuf[i, 0:WINDOW, :] = jnp.zeros((WINDOW, D_KV), BF16)
            vbuf[i, 0:WINDOW, :] = jnp.zeros((WINDOW, D_KV), BF16)
        ubuf[0:8, :] = jnp.zeros((8, D_CONV), F32)
        pbuf[0:16, :] = jnp.zeros((16, D_POOL), F32)

    @pl.when(t > 0)
    def _():
        for i in range(4):
            kbuf[i, 0:WINDOW, :] = kbuf[i, tm:tm + WINDOW, :]
            vbuf[i, 0:WINDOW, :] = vbuf[i, tm:tm + WINDOW, :]
        ubuf[0:8, :] = ubuf[tm:tm + 8, :]
        pbuf[0:16, :] = pbuf[tm:tm + 16, :]

    qkv = _dot(nb, win_ref[:, OFF_Q:OFF_CX])
    qbuf[...] = (qkv[:, OFF_Q:OFF_K] * (ATTN_SCALE * LOG2E)).astype(BF16)
    k = qkv[:, OFF_K:OFF_V]
    v = qkv[:, OFF_V:OFF_CX]
    kw_ref[...] = k[tm - WINDOW:, :]
    vw_ref[...] = v[tm - WINDOW:, :]
    k_sw = pltpu.roll(k, HEAD_DIM, 1)
    v_sw = pltpu.roll(v, HEAD_DIM, 1)
    k_var = (jnp.where(low, k, 0.0), jnp.where(low, 0.0, k_sw), jnp.where(low, k_sw, 0.0), jnp.where(low, 0.0, k))
    v_var = (jnp.where(low, v, 1.0), jnp.where(low, 1.0, v_sw), jnp.where(low, v_sw, 1.0), jnp.where(low, 1.0, v))
    for i in range(4):
        kbuf[i, WINDOW:WINDOW + tm, :] = k_var[i].astype(BF16)
        vbuf[i, WINDOW:WINDOW + tm, :] = v_var[i].astype(BF16)

    lane_q = lax.broadcasted_iota(jnp.int32, (WINDOW, 2 * HEAD_DIM), 1)
    low_q = lane_q < HEAD_DIM

    def att_block(j, carry):
        r0 = pl.multiple_of(j * WINDOW, WINDOW)
        first = jnp.where(jnp.logical_and(t == 0, j == 0), 1, 0)
        for pair in range(N_Q_HEADS // 2):
            qp = qbuf[pl.ds(r0, WINDOW), pair * 2 * HEAD_DIM:(pair + 1) * 2 * HEAD_DIM]
            halves = []
            for half in range(2):
                h = 2 * pair + half
                var = 2 * (h // GQA_GROUP) + half
                s = _dot_nt(qp, kbuf[var, pl.ds(r0, 2 * WINDOW), :]) + bias_ref[first, h]
                sink = sinks_ref[h] * LOG2E
                m = jnp.maximum(jnp.max(s, axis=-1, keepdims=True), sink)
                p = jnp.exp2(s - m).astype(BF16)
                o = _dot(p, vbuf[var, pl.ds(r0, 2 * WINDOW), :])
                den = (o[:, HEAD_DIM:HEAD_DIM + 1] if half == 0 else o[:, 0:1]) + jnp.exp2(sink - m)
                halves.append(o / den)
            att_ref[pl.ds(r0, WINDOW), pair * 2 * HEAD_DIM:(pair + 1) * 2 * HEAD_DIM] = (
                jnp.where(low_q, halves[0], halves[1]).astype(BF16))
        return carry

    lax.fori_loop(0, tm // WINDOW, att_block, 0)

    conv = _dot(nb, win_ref[:, OFF_CX:OFF_P])
    cx = conv[:, 0:D_CONV]
    cb = conv[:, D_CONV:2 * D_CONV]
    ubuf[8:8 + tm, :] = conv[:, 2 * D_CONV:3 * D_CONV] * cx
    yc = (cw_ref[0:1, :] * ubuf[6:6 + tm, :] + cw_ref[1:2, :] * ubuf[7:7 + tm, :]
          + cw_ref[2:3, :] * ubuf[8:8 + tm, :])
    cs_ref[...] = ubuf[tm + 6:tm + 8, :]

    pbuf[16:16 + tm, :] = _dot(nb, win_ref[:, OFF_P:OFF_G])
    pos = lax.broadcasted_iota(jnp.int32, (tm, 1), 0) + t * tm + 1
    d_groups = []
    for g, w in enumerate(POOL_WINDOWS):
        c0 = g * POOL_GROUP
        cur = pbuf[16:16 + tm, c0:c0 + POOL_GROUP]
        tot = cur
        for j in range(1, w):
            tot = tot + pbuf[16 - j:16 - j + tm, c0:c0 + POOL_GROUP]
        cnt = jnp.minimum(pos, w).astype(F32)
        d_groups.append(tot / cnt - cur)
    pst_ref[...] = pbuf[tm + 1:tm + 16, :]

    h_ref[...] = _mix_tail(nb, x, att_ref[...], cx, cb, yc, d_groups,
                           win_ref, wao_ref, wco_ref, wp_ref, ps_ref, wo_ref)


def _weight_specs(l, grid_rank):
    def const(*idx):
        if grid_rank == 1:
            return lambda i: idx
        return lambda b, t: idx

    return [
        pl.BlockSpec((None, 1, D_MODEL), const(l, 0, 0)),
        pl.BlockSpec((None, D_MODEL, D_IN), const(l, 0, 0)),
        pl.BlockSpec((None, D_ATTN, D_MODEL), const(l, 0, 0)),
        pl.BlockSpec((None, CONV_WIDTH, D_CONV), const(l, 0, 0)),
        pl.BlockSpec((None, D_CONV, D_MODEL), const(l, 0, 0)),
        pl.BlockSpec((None, len(POOL_WINDOWS), POOL_GROUP, POOL_OUT_GROUP), const(l, 0, 0, 0)),
        pl.BlockSpec((None, 1, D_MODEL), const(l, 0, 0)),
        pl.BlockSpec((None, D_MODEL, D_MODEL), const(l, 0, 0)),
    ]


def _prompt_mix(l, x, sinks, weights, prev_states):
    tm = PROMPT_TILE
    nt = SEQ // tm
    n_alias = len(prev_states)
    in_specs = ([pl.BlockSpec(memory_space=pltpu.SMEM),
                 pl.BlockSpec((tm, D_MODEL), lambda b, t: (b * nt + t, 0))] + _weight_specs(l, 2)
                + [pl.BlockSpec(memory_space=pl.ANY)] * n_alias)
    out_specs = [
        pl.BlockSpec((tm, D_MODEL), lambda b, t: (b * nt + t, 0)),
        pl.BlockSpec((None, None, WINDOW, D_KV), lambda b, t: (l, b, 0, 0)),
        pl.BlockSpec((None, None, WINDOW, D_KV), lambda b, t: (l, b, 0, 0)),
        pl.BlockSpec((None, None, CONV_WIDTH - 1, D_CONV), lambda b, t: (l, b, 0, 0)),
        pl.BlockSpec((None, None, POOL_BUF, D_POOL), lambda b, t: (l, b, 0, 0)),
    ]
    out_shape = [
        jax.ShapeDtypeStruct((N_PAD, D_MODEL), F32),
        jax.ShapeDtypeStruct((DEPTH, BATCH, WINDOW, D_KV), F32),
        jax.ShapeDtypeStruct((DEPTH, BATCH, WINDOW, D_KV), F32),
        jax.ShapeDtypeStruct((DEPTH, BATCH, CONV_WIDTH - 1, D_CONV), F32),
        jax.ShapeDtypeStruct((DEPTH, BATCH, POOL_BUF, D_POOL), F32),
    ]
    scratch = [
        pltpu.VMEM((tm, D_ATTN), BF16),
        pltpu.VMEM((4, WINDOW + tm, D_KV), BF16),
        pltpu.VMEM((4, WINDOW + tm, D_KV), BF16),
        pltpu.VMEM((tm, D_ATTN), BF16),
        pltpu.VMEM((8 + tm, D_CONV), F32),
        pltpu.VMEM((16 + tm, D_POOL), F32),
        pltpu.VMEM((2, N_Q_HEADS, WINDOW, 2 * WINDOW), F32),
    ]
    outs = pl.pallas_call(
        functools.partial(_prompt_mix_kernel, n_alias=n_alias),
        grid=(BATCH, nt),
        in_specs=in_specs,
        out_specs=out_specs,
        out_shape=out_shape,
        scratch_shapes=scratch,
        input_output_aliases={10 + i: 1 + i for i in range(n_alias)},
        compiler_params=pltpu.CompilerParams(
            dimension_semantics=("arbitrary", "arbitrary"), vmem_limit_bytes=VMEM_LIMIT),
        name=f"prompt_mix_l{l}",
    )(sinks, x, *weights, *prev_states)
    return outs[0], tuple(outs[1:])


def _sample_mix_kernel(*refs, n_alias):
    (sinks_ref, x_ref, kc_ref, vc_ref, cst_ref, pin_ref,
     nw_ref, win_ref, wao_ref, cw_ref, wco_ref, wp_ref, ps_ref, wo_ref) = refs[:14]
    (h_ref, ko_ref, vo_ref, co_ref, po_ref,
     qbuf, knb, vnb, att_ref, cbuf, ebuf) = refs[14 + n_alias:]
    ns, nr, T = SAMPLE_SEQS, SAMPLE_ROWS, DEC_SEQ
    x = x_ref[...]
    nb = _rms(x, nw_ref[...]).astype(BF16)

    qkv = _dot(nb, win_ref[:, OFF_Q:OFF_CX])
    qbuf[...] = (qkv[:, OFF_Q:OFF_K] * ATTN_SCALE).astype(BF16)
    k = qkv[:, OFF_K:OFF_V]
    v = qkv[:, OFF_V:OFF_CX]
    knb[...] = k.astype(BF16)
    vnb[...] = v.astype(BF16)
    ko_ref[:, 0:WINDOW - T, :] = kc_ref[:, T:WINDOW, :]
    vo_ref[:, 0:WINDOW - T, :] = vc_ref[:, T:WINDOW, :]
    ko_ref[:, WINDOW - T:WINDOW, :] = k.reshape(ns, T, D_KV)
    vo_ref[:, WINDOW - T:WINDOW, :] = v.reshape(ns, T, D_KV)

    gr = ATT_GROUP * T
    gc = ATT_GROUP * WINDOW
    rq = lax.broadcasted_iota(jnp.int32, (gr, gc), 0)
    cq = lax.broadcasted_iota(jnp.int32, (gr, gc), 1)
    tq = rq & (T - 1)
    sc_pos = cq & (WINDOW - 1)
    valid_c = jnp.logical_and((rq >> 3) == (cq >> 7), sc_pos >= tq)
    dist_c = (WINDOW + tq - sc_pos).astype(F32)
    rn = lax.broadcasted_iota(jnp.int32, (gr, gr), 0)
    cn = lax.broadcasted_iota(jnp.int32, (gr, gr), 1)
    tn = rn & (T - 1)
    jn = cn & (T - 1)
    valid_n = jnp.logical_and((rn >> 3) == (cn >> 3), jn <= tn)
    dist_n = (tn - jn).astype(F32)

    def att_group(gi, carry):
        r0 = pl.multiple_of(gi * gr, gr)
        s0 = pl.multiple_of(gi * ATT_GROUP, ATT_GROUP)
        qg = qbuf[pl.ds(r0, gr), :]
        kcg = kc_ref[pl.ds(s0, ATT_GROUP), :, :].reshape(gc, D_KV).astype(BF16)
        vcg = vc_ref[pl.ds(s0, ATT_GROUP), :, :].reshape(gc, D_KV).astype(BF16)
        kng = knb[pl.ds(r0, gr), :]
        vng = vnb[pl.ds(r0, gr), :]
        outs = []
        for h in range(N_Q_HEADS):
            kv = h // GQA_GROUP
            lo, hi = kv * HEAD_DIM, (kv + 1) * HEAD_DIM
            qh = qg[:, h * HEAD_DIM:(h + 1) * HEAD_DIM]
            slope = _head_slope(h)
            s_c = jnp.where(valid_c, _dot_nt(qh, kcg[:, lo:hi]) - slope * dist_c, -jnp.inf)
            s_n = jnp.where(valid_n, _dot_nt(qh, kng[:, lo:hi]) - slope * dist_n, -jnp.inf)
            outs.append(_softmax_pv([(s_c, vcg[:, lo:hi]), (s_n, vng[:, lo:hi])], sinks_ref[h]))
        att_ref[pl.ds(r0, gr), :] = jnp.concatenate(outs, axis=-1).astype(BF16)
        return carry

    lax.fori_loop(0, ns // ATT_GROUP, att_group, 0)

    conv = _dot(nb, win_ref[:, OFF_CX:OFF_P])
    cx = conv[:, 0:D_CONV]
    cb = conv[:, D_CONV:2 * D_CONV]
    u = conv[:, 2 * D_CONV:3 * D_CONV] * cx
    cbuf[:, 6:8, :] = cst_ref[...]
    cbuf[:, 8:16, :] = u.reshape(ns, T, D_CONV)
    w0 = cw_ref[0:1, :].reshape(1, 1, D_CONV)
    w1 = cw_ref[1:2, :].reshape(1, 1, D_CONV)
    w2 = cw_ref[2:3, :].reshape(1, 1, D_CONV)
    yc = (w0 * cbuf[:, 6:14, :] + w1 * cbuf[:, 7:15, :] + w2 * cbuf[:, 8:16, :]).reshape(nr, D_CONV)
    co_ref[...] = cbuf[:, 14:16, :]

    ebuf[:, 1:16, :] = pin_ref[...]
    ebuf[:, 16:24, :] = _dot(nb, win_ref[:, OFF_P:OFF_G]).reshape(ns, T, D_POOL)
    d_groups = []
    for g, w in enumerate(POOL_WINDOWS):
        c0 = g * POOL_GROUP
        cur = ebuf[:, 16:24, c0:c0 + POOL_GROUP]
        tot = cur
        for j in range(1, w):
            tot = tot + ebuf[:, 16 - j:24 - j, c0:c0 + POOL_GROUP]
        d_groups.append((tot / float(w) - cur).reshape(nr, POOL_GROUP))
    po_ref[...] = ebuf[:, 9:24, :]

    h_ref[...] = _mix_tail(nb, x, att_ref[...], cx, cb, yc, d_groups,
                           win_ref, wao_ref, wco_ref, wp_ref, ps_ref, wo_ref)


def _sample_mix(l, x, x_row0, sinks, kc, vc, cst, pst, weights, h_buf, prev_states):
    ns, nr = SAMPLE_SEQS, SAMPLE_ROWS
    n_alias = 1 + len(prev_states)
    xb0 = x_row0 // nr
    hb0 = N_PROMPT // nr
    in_specs = [
        pl.BlockSpec(memory_space=pltpu.SMEM),
        pl.BlockSpec((nr, D_MODEL), lambda i: (xb0 + i, 0)),
        pl.BlockSpec((None, ns, WINDOW, D_KV), lambda i: (l, i, 0, 0)),
        pl.BlockSpec((None, ns, WINDOW, D_KV), lambda i: (l, i, 0, 0)),
        pl.BlockSpec((None, ns, CONV_WIDTH - 1, D_CONV), lambda i: (l, i, 0, 0)),
        pl.BlockSpec((None, ns, POOL_BUF, D_POOL), lambda i: (l, i, 0, 0)),
    ] + _weight_specs(l, 1) + [pl.BlockSpec(memory_space=pl.ANY)] * n_alias
    out_specs = [
        pl.BlockSpec((nr, D_MODEL), lambda i: (hb0 + i, 0)),
        pl.BlockSpec((None, ns, WINDOW, D_KV), lambda i: (l, i, 0, 0)),
        pl.BlockSpec((None, ns, WINDOW, D_KV), lambda i: (l, i, 0, 0)),
        pl.BlockSpec((None, ns, CONV_WIDTH - 1, D_CONV), lambda i: (l, i, 0, 0)),
        pl.BlockSpec((None, ns, POOL_BUF, D_POOL), lambda i: (l, i, 0, 0)),
    ]
    out_shape = [
        jax.ShapeDtypeStruct((N_PAD, D_MODEL), F32),
        jax.ShapeDtypeStruct((DEPTH, DEC_BATCH, WINDOW, D_KV), F32),
        jax.ShapeDtypeStruct((DEPTH, DEC_BATCH, WINDOW, D_KV), F32),
        jax.ShapeDtypeStruct((DEPTH, DEC_BATCH, CONV_WIDTH - 1, D_CONV), F32),
        jax.ShapeDtypeStruct((DEPTH, DEC_BATCH, POOL_BUF, D_POOL), F32),
    ]
    scratch = [
        pltpu.VMEM((nr, D_ATTN), BF16),
        pltpu.VMEM((nr, D_KV), BF16),
        pltpu.VMEM((nr, D_KV), BF16),
        pltpu.VMEM((nr, D_ATTN), BF16),
        pltpu.VMEM((ns, 16, D_CONV), F32),
        pltpu.VMEM((ns, 24, D_POOL), F32),
    ]
    outs = pl.pallas_call(
        functools.partial(_sample_mix_kernel, n_alias=n_alias),
        grid=(DEC_BATCH // ns,),
        in_specs=in_specs,
        out_specs=out_specs,
        out_shape=out_shape,
        scratch_shapes=scratch,
        input_output_aliases={14 + i: i for i in range(n_alias)},
        compiler_params=pltpu.CompilerParams(
            dimension_semantics=("arbitrary",), vmem_limit_bytes=VMEM_LIMIT),
        name=f"sample_mix_l{l}",
    )(sinks, x, kc, vc, cst, pst, *weights, h_buf, *prev_states)
    return outs[0], tuple(outs[1:])


def _ffn_kernel(h_ref, nw_ref, wg_ref, wu_ref, wd_ref, o_ref, hn_ref, acc_ref):
    c = pl.program_id(1)

    @pl.when(c == 0)
    def _():
        hn_ref[...] = _rms(h_ref[...], nw_ref[...]).astype(BF16)
        acc_ref[...] = jnp.zeros_like(acc_ref)

    hn = hn_ref[...]
    g = _dot(hn, wg_ref[...])
    u = _dot(hn, wu_ref[...])
    acc_ref[...] += _dot((g * _sigmoid(g) * u).astype(BF16), wd_ref[...])

    @pl.when(c == pl.num_programs(1) - 1)
    def _():
        o_ref[...] = h_ref[...] + acc_ref[...]


def _ffn_dense(h, nw, wg, wu, wd):
    tm = FFN_TILE
    nc = D_FF // FF_CHUNK
    return pl.pallas_call(
        _ffn_kernel,
        grid=(N_TOK // tm, nc),
        in_specs=[
            pl.BlockSpec((tm, D_MODEL), lambda i, c: (i, 0)),
            pl.BlockSpec((1, D_MODEL), lambda i, c: (0, 0)),
            pl.BlockSpec((None, D_MODEL, FF_CHUNK), lambda i, c: (0, 0, c)),
            pl.BlockSpec((None, D_MODEL, FF_CHUNK), lambda i, c: (0, 0, c)),
            pl.BlockSpec((None, FF_CHUNK, D_MODEL), lambda i, c: (0, c, 0)),
        ],
        out_specs=pl.BlockSpec((tm, D_MODEL), lambda i, c: (i, 0)),
        out_shape=jax.ShapeDtypeStruct((N_PAD, D_MODEL), F32),
        scratch_shapes=[pltpu.VMEM((tm, D_MODEL), BF16), pltpu.VMEM((tm, D_MODEL), F32)],
        compiler_params=pltpu.CompilerParams(
            dimension_semantics=("arbitrary", "arbitrary"), vmem_limit_bytes=VMEM_LIMIT),
        name="ffn_dense",
    )(h, nw, wg, wu, wd)


def _router_kernel(h_ref, nw_ref, rt_ref, hn_ref, slot_ref, col_ref, cnt_ref):
    tm = FFN_TILE
    real = jnp.where(pl.program_id(0) < N_TOK // tm, 1.0, 0.0)
    hn = _rms(h_ref[...], nw_ref[...])
    hn_ref[...] = (hn * real).astype(BF16)
    logits = lax.dot_general(rt_ref[...], hn, (((1,), (1,)), ((), ())),
                             precision=lax.Precision.HIGHEST, preferred_element_type=F32)
    eidx = lax.broadcasted_iota(jnp.int32, (N_EXPERTS, tm), 0).astype(F32)
    none = float(N_EXPERTS)
    m1 = jnp.max(logits, axis=0, keepdims=True)
    i1 = jnp.min(jnp.where(logits == m1, eidx, none), axis=0, keepdims=True)
    rest = jnp.where(eidx == i1, -jnp.inf, logits)
    m2 = jnp.max(rest, axis=0, keepdims=True)
    i2 = jnp.min(jnp.where(rest == m2, eidx, none), axis=0, keepdims=True)
    e2 = jnp.exp(m2 - m1)
    w1 = 1.0 / (1.0 + e2)
    w2 = e2 / (1.0 + e2)
    sel1 = eidx == i1
    sel2 = eidx == i2
    gate = jnp.where(sel1, w1, jnp.where(sel2, w2, 0.0)) * real
    mask = jnp.where(jnp.logical_or(sel1, sel2), real, 0.0)
    chosen = mask > 0.0
    srow = lax.broadcasted_iota(jnp.int32, (tm, tm), 0)
    scol = lax.broadcasted_iota(jnp.int32, (tm, tm), 1)
    upper = jnp.where(srow < scol, 1.0, 0.0).astype(BF16)
    mask16 = jnp.concatenate([mask, jnp.zeros_like(mask)], axis=0).astype(BF16)
    slot = jnp.where(chosen, _dot(mask16, upper)[0:N_EXPERTS, :], -1.0)
    slot_ref[...] = slot.astype(jnp.int32)
    cnt_ref[...] = jnp.broadcast_to(jnp.sum(mask, axis=1, keepdims=True), (N_EXPERTS, 128))
    both = jnp.concatenate([slot, gate, jnp.zeros((128 - 2 * N_EXPERTS, tm), F32)], axis=0)
    col_ref[...] = both.T[:, 0:2 * N_EXPERTS]


def _router(h, nw, router_t):
    tm = FFN_TILE
    nt = N_PAD // tm
    last = N_TOK // tm - 1
    return pl.pallas_call(
        _router_kernel,
        grid=(nt,),
        in_specs=[
            pl.BlockSpec((tm, D_MODEL), lambda i: (jnp.minimum(i, last), 0)),
            pl.BlockSpec((1, D_MODEL), lambda i: (0, 0)),
            pl.BlockSpec((N_EXPERTS, D_MODEL), lambda i: (0, 0)),
        ],
        out_specs=[
            pl.BlockSpec((tm, D_MODEL), lambda i: (i, 0)),
            pl.BlockSpec((N_EXPERTS, tm), lambda i: (0, i)),
            pl.BlockSpec((tm, 2 * N_EXPERTS), lambda i: (i, 0)),
            pl.BlockSpec((None, N_EXPERTS, 128), lambda i: (i, 0, 0)),
        ],
        out_shape=[
            jax.ShapeDtypeStruct((N_PAD, D_MODEL), BF16),
            jax.ShapeDtypeStruct((N_EXPERTS, N_PAD), jnp.int32),
            jax.ShapeDtypeStruct((N_PAD, 2 * N_EXPERTS), F32),
            jax.ShapeDtypeStruct((nt, N_EXPERTS, 128), F32),
        ],
        compiler_params=pltpu.CompilerParams(
            dimension_semantics=("arbitrary",), vmem_limit_bytes=VMEM_LIMIT),
        name="moe_router",
    )(h, nw, router_t)


def _moe_meta(cnt):
    assert MOE_SUBS == 2
    c = cnt.reshape(N_PAD // MOE_TILE, MOE_SUBS, N_EXPERTS).transpose(0, 2, 1)
    c0, c1 = c[..., 0], c[..., 1]
    off1 = -(-c0 // ROW_ALIGN) * ROW_ALIGN
    nbt = -(-(off1 + c1) // ROW_BLOCK)
    nb0 = -(-c0 // ROW_BLOCK)
    nb1 = nbt - off1 // ROW_BLOCK
    ns0 = -(-c0 // (2 * ROW_BLOCK))
    ns1 = -(-c1 // (2 * ROW_BLOCK))
    zero = jnp.zeros_like(c0)
    return jnp.stack([zero, off1, nb0, nb1, nbt, ns0, ns1, zero], axis=-1).reshape(-1)


def _moe_kernel(meta_ref, hn_ref, srow_ref, col_ref, wg_ref, wu_ref, wd_ref, o_ref, xs_ref, ys_ref):
    tu, rb = FFN_TILE, ROW_BLOCK
    t, e, c = pl.program_id(0), pl.program_id(1), pl.program_id(2)
    nc = pl.num_programs(2)
    base = (t * N_EXPERTS + e) * N_META
    nbt = meta_ref[base + 4]

    @pl.when(jnp.logical_and(e == 0, c == 0))
    def _():
        o_ref[...] = jnp.zeros_like(o_ref)

    @pl.when(c == 0)
    def _():
        for u in range(MOE_SUBS):
            off = meta_ref[base + u]
            slot_row = srow_ref[:, u * tu:(u + 1) * tu]

            def gather(b, carry, u=u, off=off, slot_row=slot_row):
                r0 = pl.multiple_of(off + b * rb, ROW_ALIGN)
                rid = lax.broadcasted_iota(jnp.int32, (rb, tu), 0) + b * rb
                onehot = jnp.where(slot_row == rid, 1.0, 0.0).astype(BF16)
                xs_ref[pl.ds(r0, rb), :] = _dot(onehot, hn_ref[u * tu:(u + 1) * tu, :]).astype(BF16)
                return carry

            lax.fori_loop(0, meta_ref[base + 2 + u], gather, 0)

        def clear(b, carry):
            ys_ref[pl.ds(pl.multiple_of(b * rb, rb), rb), :] = jnp.zeros((rb, D_MODEL), F32)
            return carry

        lax.fori_loop(0, nbt + 2, clear, 0)

    def expert_rows(r0, rows):
        xb = xs_ref[pl.ds(r0, rows), :]
        g = _dot(xb, wg_ref[...])
        u = _dot(xb, wu_ref[...])
        ys_ref[pl.ds(r0, rows), :] += _dot((g * _sigmoid(g) * u).astype(BF16), wd_ref[...])

    def expert_pair(b, carry):
        expert_rows(pl.multiple_of(b * 2 * rb, 2 * rb), 2 * rb)
        return carry

    lax.fori_loop(0, nbt >> 1, expert_pair, 0)

    @pl.when((nbt & 1) == 1)
    def _():
        expert_rows(pl.multiple_of((nbt - 1) * rb, rb), rb)

    @pl.when(c == nc - 1)
    def _():
        lane = lax.broadcasted_iota(jnp.int32, (tu, 2 * N_EXPERTS), 1)
        for u in range(MOE_SUBS):
            off = meta_ref[base + u]
            cols = col_ref[u * tu:(u + 1) * tu, :]
            slot_col = jnp.sum(jnp.where(lane == e, cols, 0.0), axis=1, keepdims=True)
            gate_col = jnp.sum(jnp.where(lane == e + N_EXPERTS, cols, 0.0), axis=1, keepdims=True)

            def scatter(b, carry, u=u, off=off, slot_col=slot_col, gate_col=gate_col):
                r0 = pl.multiple_of(off + b * 2 * rb, ROW_ALIGN)
                cid = (lax.broadcasted_iota(jnp.int32, (tu, 2 * rb), 1) + b * 2 * rb).astype(F32)
                weighted = jnp.where(slot_col == cid, gate_col, 0.0).astype(BF16)
                o_ref[u * tu:(u + 1) * tu, :] += _dot(weighted, ys_ref[pl.ds(r0, 2 * rb), :].astype(BF16))
                return carry

            lax.fori_loop(0, meta_ref[base + 5 + u], scatter, 0)


def _moe(meta, hn, slot_row, cols, wg, wu, wd):
    tm = MOE_TILE
    nt = N_PAD // tm
    nc = D_EXPERT // EXP_CHUNK
    grid_spec = pltpu.PrefetchScalarGridSpec(
        num_scalar_prefetch=1,
        grid=(nt, N_EXPERTS, nc),
        in_specs=[
            pl.BlockSpec((tm, D_MODEL), lambda t, e, c, m: (t, 0)),
            pl.BlockSpec((None, 1, tm), lambda t, e, c, m: (e, 0, t)),
            pl.BlockSpec((tm, 2 * N_EXPERTS), lambda t, e, c, m: (t, 0)),
            pl.BlockSpec((None, None, D_MODEL, EXP_CHUNK), lambda t, e, c, m: (0, e, 0, c)),
            pl.BlockSpec((None, None, D_MODEL, EXP_CHUNK), lambda t, e, c, m: (0, e, 0, c)),
            pl.BlockSpec((None, None, EXP_CHUNK, D_MODEL), lambda t, e, c, m: (0, e, c, 0)),
        ],
        out_specs=pl.BlockSpec((tm, D_MODEL), lambda t, e, c, m: (t, 0)),
        scratch_shapes=[
            pltpu.VMEM((tm + 2 * ROW_BLOCK, D_MODEL), BF16),
            pltpu.VMEM((tm + 2 * ROW_BLOCK, D_MODEL), F32),
        ],
    )
    return pl.pallas_call(
        _moe_kernel,
        grid_spec=grid_spec,
        out_shape=jax.ShapeDtypeStruct((N_PAD, D_MODEL), F32),
        compiler_params=pltpu.CompilerParams(
            dimension_semantics=("arbitrary", "arbitrary", "arbitrary"), vmem_limit_bytes=VMEM_LIMIT),
        name="moe_experts",
    )(meta, hn, slot_row, cols, wg, wu, wd)


def _final_kernel(h_ref, y_ref, nf_ref, op_ref, os_ref):
    i = pl.program_id(0)
    out = _rms(h_ref[...] + y_ref[...], nf_ref[...])

    @pl.when(i < N_PROMPT // FFN_TILE)
    def _():
        op_ref[...] = out

    @pl.when(i >= N_PROMPT // FFN_TILE)
    def _():
        os_ref[...] = out


def _final(h, y, nf):
    tm = FFN_TILE
    np_tiles = N_PROMPT // tm
    assert N_SAMPLE == tm
    return pl.pallas_call(
        _final_kernel,
        grid=(N_TOK // tm,),
        in_specs=[
            pl.BlockSpec((tm, D_MODEL), lambda i: (i, 0)),
            pl.BlockSpec((tm, D_MODEL), lambda i: (i, 0)),
            pl.BlockSpec((1, D_MODEL), lambda i: (0, 0)),
        ],
        out_specs=[
            pl.BlockSpec((tm, D_MODEL), lambda i: (jnp.minimum(i, np_tiles - 1), 0)),
            pl.BlockSpec((tm, D_MODEL), lambda i: (0, 0)),
        ],
        out_shape=[
            jax.ShapeDtypeStruct((N_PROMPT, D_MODEL), F32),
            jax.ShapeDtypeStruct((N_SAMPLE, D_MODEL), F32),
        ],
        compiler_params=pltpu.CompilerParams(
            dimension_semantics=("arbitrary",), vmem_limit_bytes=VMEM_LIMIT),
        name="final_norm",
    )(h, y, nf)


def kernel(x_prompt, x_sample, cache_win_k, cache_win_v, state_conv, state_pool, norm_mix, w_in, attn_sinks,
           w_attn_out, conv_w, w_conv_out, w_pool, pool_scale, w_out, norm_ffn, ffn_w_gate, ffn_w_up, ffn_w_down,
           moe_router, moe_w_gate, moe_w_up, moe_w_down, norm_final):
    mix_weights = (norm_mix.reshape(DEPTH, 1, D_MODEL), w_in.astype(BF16), w_attn_out.astype(BF16), conv_w,
                   w_conv_out.astype(BF16), w_pool.astype(BF16), pool_scale.reshape(DEPTH, 1, D_MODEL),
                   w_out.astype(BF16))
    kc = cache_win_k.reshape(DEPTH, DEC_BATCH, WINDOW, D_KV)
    vc = cache_win_v.reshape(DEPTH, DEC_BATCH, WINDOW, D_KV)

    xp = x_prompt.reshape(N_PROMPT, D_MODEL)
    xs, xs_row0 = x_sample.reshape(N_SAMPLE, D_MODEL), 0
    p_states, s_states = (), ()
    for l in range(DEPTH):
        sinks = attn_sinks[l]
        h, p_states = _prompt_mix(l, xp, sinks, mix_weights, p_states)
        h, s_states = _sample_mix(l, xs, xs_row0, sinks, kc, vc, state_conv, state_pool, mix_weights, h, s_states)
        i = l // 2
        nw = norm_ffn[l].reshape(1, D_MODEL)
        if l % 2 == 0:
            xp = _ffn_dense(h, nw, ffn_w_gate[i:i + 1].astype(BF16), ffn_w_up[i:i + 1].astype(BF16),
                            ffn_w_down[i:i + 1].astype(BF16))
            xs, xs_row0 = xp, N_PROMPT
        else:
            hn, slot_row, cols, cnt = _router(h, nw, moe_router[i].T)
            meta = _moe_meta(cnt[:, :, 0].astype(jnp.int32))
            y = _moe(meta, hn, slot_row.reshape(N_EXPERTS, 1, N_PAD), cols, moe_w_gate[i:i + 1].astype(BF16),
                     moe_w_up[i:i + 1].astype(BF16), moe_w_down[i:i + 1].astype(BF16))
            y_prompt, y_sample = _final(h, y, norm_final.reshape(1, D_MODEL))

    pk, pv, pc, pp = p_states
    sk, sv, sc, sp = s_states
    kv_shape_p = (DEPTH, BATCH, WINDOW, N_KV_HEADS, HEAD_DIM)
    kv_shape_s = (DEPTH, DEC_BATCH, WINDOW, N_KV_HEADS, HEAD_DIM)
    return (y_prompt.reshape(BATCH, SEQ, D_MODEL), y_sample.reshape(DEC_BATCH, DEC_SEQ, D_MODEL),
            pk.reshape(kv_shape_p), pv.reshape(kv_shape_p), pc, pp,
            sk.reshape(kv_shape_s), sv.reshape(kv_shape_s), sc, sp)
```

```python
import functools
import math

import jax
import jax.numpy as jnp
from jax import lax
from jax.experimental import pallas as pl
from jax.experimental.pallas import tpu as pltpu

F32 = jnp.float32
BF16 = jnp.bfloat16

D_MODEL = 1024
BATCH = 4
SEQ = 4096
DEPTH = 2
DEC_BATCH = 128
DEC_SEQ = 8
HEAD_DIM = 64
N_Q_HEADS = 8
N_KV_HEADS = 2
GQA_GROUP = N_Q_HEADS // N_KV_HEADS
D_ATTN = N_Q_HEADS * HEAD_DIM
D_KV = N_KV_HEADS * HEAD_DIM
WINDOW = 128
ATTN_SCALE = HEAD_DIM ** -0.5
D_CONV = D_MODEL // 2
CONV_WIDTH = 3
D_POOL = D_MODEL // 2
POOL_WINDOWS = (2, 4, 8, 16)
POOL_GROUP = D_POOL // len(POOL_WINDOWS)
POOL_OUT_GROUP = D_MODEL // len(POOL_WINDOWS)
POOL_BUF = max(POOL_WINDOWS) - 1
OFF_Q = 0
OFF_K = OFF_Q + D_ATTN
OFF_V = OFF_K + D_KV
OFF_CX = OFF_V + D_KV
OFF_CB = OFF_CX + D_CONV
OFF_CC = OFF_CB + D_CONV
OFF_P = OFF_CC + D_CONV
OFF_G = OFF_P + D_POOL
D_IN = OFF_G + 3 * D_MODEL
D_FF = 2816
N_EXPERTS = 8
D_EXPERT = 3584
EPS = 1e-5
LOG2E = math.log2(math.e)

N_PROMPT = BATCH * SEQ
N_SAMPLE = DEC_BATCH * DEC_SEQ
N_TOK = N_PROMPT + N_SAMPLE

PROMPT_TILE = 512
SAMPLE_SEQS = 32
SAMPLE_ROWS = SAMPLE_SEQS * DEC_SEQ
ATT_GROUP = 8
FFN_TILE = 1024
FF_CHUNK = D_FF // 2
EXP_CHUNK = D_EXPERT // 4
ROW_BLOCK = 128
ROW_ALIGN = 16
EXP_TILE = 1024
N_UNITS = N_TOK // FFN_TILE
MAX_TILES = -(-(2 * N_TOK + N_UNITS * N_EXPERTS * (ROW_ALIGN - 1)) // EXP_TILE) + N_EXPERTS
N_PAD = N_TOK
SEG_BITS = 7
assert FFN_TILE == ROW_ALIGN << (SEG_BITS - 1)
VMEM_LIMIT = 58 * 1024 * 1024


def _dot(a, b):
    return jnp.dot(a, b, preferred_element_type=F32)


def _dot_nt(a, b):
    return lax.dot_general(a, b, (((1,), (1,)), ((), ())), preferred_element_type=F32)


def _rms(x, g):
    return x * lax.rsqrt(jnp.mean(x * x, axis=-1, keepdims=True) + EPS) * g


def _sigmoid(x):
    return 1.0 / (1.0 + jnp.exp(-x))


def _head_slope(h):
    return float(2.0 ** (-8.0 * (h + 1) / N_Q_HEADS))


def _softmax_pv(parts, sink):
    m = sink
    for s, _ in parts:
        m = jnp.maximum(m, jnp.max(s, axis=-1, keepdims=True))
    den = jnp.exp(sink - m)
    o = None
    for s, v in parts:
        p = jnp.exp(s - m)
        den = den + jnp.sum(p, axis=-1, keepdims=True)
        pv = _dot(p.astype(BF16), v)
        o = pv if o is None else o + pv
    return o / den


def _mix_tail(nb, x, att_bf, cx, cb, yc, d_groups, win_ref, wao_ref, wco_ref, wp_ref, ps_ref, wo_ref):
    att_o = _dot(att_bf, wao_ref[...])
    merged = _sigmoid(_dot(nb, win_ref[:, OFF_G:OFF_G + D_MODEL])) * att_o
    conv_o = _dot((cb * yc).astype(BF16), wco_ref[...])
    merged = merged + _sigmoid(_dot(nb, win_ref[:, OFF_G + D_MODEL:OFF_G + 2 * D_MODEL])) * conv_o
    pool_o = jnp.concatenate([_dot(d.astype(BF16), wp_ref[g]) for g, d in enumerate(d_groups)], axis=-1)
    pool_o = pool_o * ps_ref[...]
    merged = merged + _sigmoid(_dot(nb, win_ref[:, OFF_G + 2 * D_MODEL:OFF_G + 3 * D_MODEL])) * pool_o
    return x + _dot(merged.astype(BF16), wo_ref[...])


def _prompt_mix_kernel(*refs, n_alias):
    (sinks_ref, x_ref, nw_ref, win_ref, wao_ref, cw_ref, wco_ref, wp_ref, ps_ref, wo_ref) = refs[:10]
    (h_ref, kw_ref, vw_ref, cs_ref, pst_ref,
     qbuf, kbuf, vbuf, att_ref, ubuf, pbuf, bias_ref) = refs[10 + n_alias:]
    tm = PROMPT_TILE
    b = pl.program_id(0)
    t = pl.program_id(1)
    x = x_ref[...]
    nb = _rms(x, nw_ref[...]).astype(BF16)
    lane = lax.broadcasted_iota(jnp.int32, (tm, D_KV), 1)
    low = lane < HEAD_DIM

    @pl.when(jnp.logical_and(b == 0, t == 0))
    def _():
        qi = lax.broadcasted_iota(jnp.int32, (WINDOW, 2 * WINDOW), 0)
        si = lax.broadcasted_iota(jnp.int32, (WINDOW, 2 * WINDOW), 1)
        dist_i = qi + WINDOW - si
        dist = dist_i.astype(F32)
        band = jnp.logical_and(dist_i >= 0, dist_i <= WINDOW)
        for h in range(N_Q_HEADS):
            bias = jnp.where(band, (-_head_slope(h) * LOG2E) * dist, -jnp.inf)
            bias_ref[0, h] = bias
            bias_ref[1, h] = jnp.where(si >= WINDOW, bias, -jnp.inf)

    @pl.when(t == 0)
    def _():
        for i in range(4):
            kbuf[i, 0:WINDOW, :] = jnp.zeros((WINDOW, D_KV), BF16)
            vbuf[i, 0:WINDOW, :] = jnp.zeros((WINDOW, D_KV), BF16)
        ubuf[0:8, :] = jnp.zeros((8, D_CONV), F32)
        pbuf[0:16, :] = jnp.zeros((16, D_POOL), F32)

    @pl.when(t > 0)
    def _():
        for i in range(4):
            kbuf[i, 0:WINDOW, :] = kbuf[i, tm:tm + WINDOW, :]
            vbuf[i, 0:WINDOW, :] = vbuf[i, tm:tm + WINDOW, :]
        ubuf[0:8, :] = ubuf[tm:tm + 8, :]
        pbuf[0:16, :] = pbuf[tm:tm + 16, :]

    qkv = _dot(nb, win_ref[:, OFF_Q:OFF_CX])
    qbuf[...] = (qkv[:, OFF_Q:OFF_K] * (ATTN_SCALE * LOG2E)).astype(BF16)
    k = qkv[:, OFF_K:OFF_V]
    v = qkv[:, OFF_V:OFF_CX]
    kw_ref[...] = k[tm - WINDOW:, :]
    vw_ref[...] = v[tm - WINDOW:, :]
    k_sw = pltpu.roll(k, HEAD_DIM, 1)
    v_sw = pltpu.roll(v, HEAD_DIM, 1)
    k_var = (jnp.where(low, k, 0.0), jnp.where(low, 0.0, k_sw), jnp.where(low, k_sw, 0.0), jnp.where(low, 0.0, k))
    v_var = (jnp.where(low, v, 1.0), jnp.where(low, 1.0, v_sw), jnp.where(low, v_sw, 1.0), jnp.where(low, 1.0, v))
    for i in range(4):
        kbuf[i, WINDOW:WINDOW + tm, :] = k_var[i].astype(BF16)
        vbuf[i, WINDOW:WINDOW + tm, :] = v_var[i].astype(BF16)

    lane_q = lax.broadcasted_iota(jnp.int32, (WINDOW, 2 * HEAD_DIM), 1)
    low_q = lane_q < HEAD_DIM

    def att_block(j, carry):
        r0 = pl.multiple_of(j * WINDOW, WINDOW)
        first = jnp.where(jnp.logical_and(t == 0, j == 0), 1, 0)
        for pair in range(N_Q_HEADS // 2):
            qp = qbuf[pl.ds(r0, WINDOW), pair * 2 * HEAD_DIM:(pair + 1) * 2 * HEAD_DIM]
            halves = []
            for half in range(2):
                h = 2 * pair + half
                var = 2 * (h // GQA_GROUP) + half
                s = _dot_nt(qp, kbuf[var, pl.ds(r0, 2 * WINDOW), :]) + bias_ref[first, h]
                sink = sinks_ref[h] * LOG2E
                m = jnp.maximum(jnp.max(s, axis=-1, keepdims=True), sink)
                p = jnp.exp2(s - m).astype(BF16)
                o = _dot(p, vbuf[var, pl.ds(r0, 2 * WINDOW), :])
                den = (o[:, HEAD_DIM:HEAD_DIM + 1] if half == 0 else o[:, 0:1]) + jnp.exp2(sink - m)
                halves.append(o / den)
            att_ref[pl.ds(r0, WINDOW), pair * 2 * HEAD_DIM:(pair + 1) * 2 * HEAD_DIM] = (
                jnp.where(low_q, halves[0], halves[1]).astype(BF16))
        return carry

    lax.fori_loop(0, tm // WINDOW, att_block, 0)

    conv = _dot(nb, win_ref[:, OFF_CX:OFF_P])
    cx = conv[:, 0:D_CONV]
    cb = conv[:, D_CONV:2 * D_CONV]
    ubuf[8:8 + tm, :] = conv[:, 2 * D_CONV:3 * D_CONV] * cx
    yc = (cw_ref[0:1, :] * ubuf[6:6 + tm, :] + cw_ref[1:2, :] * ubuf[7:7 + tm, :]
          + cw_ref[2:3, :] * ubuf[8:8 + tm, :])
    cs_ref[...] = ubuf[tm + 6:tm + 8, :]

    pbuf[16:16 + tm, :] = _dot(nb, win_ref[:, OFF_P:OFF_G])
    pos = lax.broadcasted_iota(jnp.int32, (tm, 1), 0) + t * tm + 1
    d_groups = []
    for g, w in enumerate(POOL_WINDOWS):
        c0 = g * POOL_GROUP
        cur = pbuf[16:16 + tm, c0:c0 + POOL_GROUP]
        tot = cur
        for j in range(1, w):
            tot = tot + pbuf[16 - j:16 - j + tm, c0:c0 + POOL_GROUP]
        cnt = jnp.minimum(pos, w).astype(F32)
        d_groups.append(tot / cnt - cur)
    pst_ref[...] = pbuf[tm + 1:tm + 16, :]

    h_ref[...] = _mix_tail(nb, x, att_ref[...], cx, cb, yc, d_groups,
                           win_ref, wao_ref, wco_ref, wp_ref, ps_ref, wo_ref)


def _weight_specs(l, grid_rank):
    def const(*idx):
        if grid_rank == 1:
            return lambda i: idx
        return lambda b, t: idx

    return [
        pl.BlockSpec((None, 1, D_MODEL), const(l, 0, 0)),
        pl.BlockSpec((None, D_MODEL, D_IN), const(l, 0, 0)),
        pl.BlockSpec((None, D_ATTN, D_MODEL), const(l, 0, 0)),
        pl.BlockSpec((None, CONV_WIDTH, D_CONV), const(l, 0, 0)),
        pl.BlockSpec((None, D_CONV, D_MODEL), const(l, 0, 0)),
        pl.BlockSpec((None, len(POOL_WINDOWS), POOL_GROUP, POOL_OUT_GROUP), const(l, 0, 0, 0)),
        pl.BlockSpec((None, 1, D_MODEL), const(l, 0, 0)),
        pl.BlockSpec((None, D_MODEL, D_MODEL), const(l, 0, 0)),
    ]


def _prompt_mix(l, x, sinks, weights, prev_states):
    tm = PROMPT_TILE
    nt = SEQ // tm
    n_alias = len(prev_states)
    in_specs = ([pl.BlockSpec(memory_space=pltpu.SMEM),
                 pl.BlockSpec((tm, D_MODEL), lambda b, t: (b * nt + t, 0))] + _weight_specs(l, 2)
                + [pl.BlockSpec(memory_space=pl.ANY)] * n_alias)
    out_specs = [
        pl.BlockSpec((tm, D_MODEL), lambda b, t: (b * nt + t, 0)),
        pl.BlockSpec((None, None, WINDOW, D_KV), lambda b, t: (l, b, 0, 0)),
        pl.BlockSpec((None, None, WINDOW, D_KV), lambda b, t: (l, b, 0, 0)),
        pl.BlockSpec((None, None, CONV_WIDTH - 1, D_CONV), lambda b, t: (l, b, 0, 0)),
        pl.BlockSpec((None, None, POOL_BUF, D_POOL), lambda b, t: (l, b, 0, 0)),
    ]
    out_shape = [
        jax.ShapeDtypeStruct((N_PAD, D_MODEL), F32),
        jax.ShapeDtypeStruct((DEPTH, BATCH, WINDOW, D_KV), F32),
        jax.ShapeDtypeStruct((DEPTH, BATCH, WINDOW, D_KV), F32),
        jax.ShapeDtypeStruct((DEPTH, BATCH, CONV_WIDTH - 1, D_CONV), F32),
        jax.ShapeDtypeStruct((DEPTH, BATCH, POOL_BUF, D_POOL), F32),
    ]
    scratch = [
        pltpu.VMEM((tm, D_ATTN), BF16),
        pltpu.VMEM((4, WINDOW + tm, D_KV), BF16),
        pltpu.VMEM((4, WINDOW + tm, D_KV), BF16),
        pltpu.VMEM((tm, D_ATTN), BF16),
        pltpu.VMEM((8 + tm, D_CONV), F32),
        pltpu.VMEM((16 + tm, D_POOL), F32),
        pltpu.VMEM((2, N_Q_HEADS, WINDOW, 2 * WINDOW), F32),
    ]
    outs = pl.pallas_call(
        functools.partial(_prompt_mix_kernel, n_alias=n_alias),
        grid=(BATCH, nt),
        in_specs=in_specs,
        out_specs=out_specs,
        out_shape=out_shape,
        scratch_shapes=scratch,
        input_output_aliases={10 + i: 1 + i for i in range(n_alias)},
        compiler_params=pltpu.CompilerParams(
            dimension_semantics=("arbitrary", "arbitrary"), vmem_limit_bytes=VMEM_LIMIT),
        name=f"prompt_mix_l{l}",
    )(sinks, x, *weights, *prev_states)
    return outs[0], tuple(outs[1:])


def _sample_mix_kernel(*refs, n_alias):
    (sinks_ref, x_ref, kc_ref, vc_ref, cst_ref, pin_ref,
     nw_ref, win_ref, wao_ref, cw_ref, wco_ref, wp_ref, ps_ref, wo_ref) = refs[:14]
    (h_ref, ko_ref, vo_ref, co_ref, po_ref,
     qbuf, knb, vnb, att_ref, cbuf, ebuf) = refs[14 + n_alias:]
    ns, nr, T = SAMPLE_SEQS, SAMPLE_ROWS, DEC_SEQ
    x = x_ref[...]
    nb = _rms(x, nw_ref[...]).astype(BF16)

    qkv = _dot(nb, win_ref[:, OFF_Q:OFF_CX])
    qbuf[...] = (qkv[:, OFF_Q:OFF_K] * ATTN_SCALE).astype(BF16)
    k = qkv[:, OFF_K:OFF_V]
    v = qkv[:, OFF_V:OFF_CX]
    knb[...] = k.astype(BF16)
    vnb[...] = v.astype(BF16)
    ko_ref[:, 0:WINDOW - T, :] = kc_ref[:, T:WINDOW, :]
    vo_ref[:, 0:WINDOW - T, :] = vc_ref[:, T:WINDOW, :]
    ko_ref[:, WINDOW - T:WINDOW, :] = k.reshape(ns, T, D_KV)
    vo_ref[:, WINDOW - T:WINDOW, :] = v.reshape(ns, T, D_KV)

    gr = ATT_GROUP * T
    gc = ATT_GROUP * WINDOW
    rq = lax.broadcasted_iota(jnp.int32, (gr, gc), 0)
    cq = lax.broadcasted_iota(jnp.int32, (gr, gc), 1)
    tq = rq & (T - 1)
    sc_pos = cq & (WINDOW - 1)
    valid_c = jnp.logical_and((rq >> 3) == (cq >> 7), sc_pos >= tq)
    dist_c = (WINDOW + tq - sc_pos).astype(F32)
    rn = lax.broadcasted_iota(jnp.int32, (gr, gr), 0)
    cn = lax.broadcasted_iota(jnp.int32, (gr, gr), 1)
    tn = rn & (T - 1)
    jn = cn & (T - 1)
    valid_n = jnp.logical_and((rn >> 3) == (cn >> 3), jn <= tn)
    dist_n = (tn - jn).astype(F32)

    def att_group(gi, carry):
        r0 = pl.multiple_of(gi * gr, gr)
        s0 = pl.multiple_of(gi * ATT_GROUP, ATT_GROUP)
        qg = qbuf[pl.ds(r0, gr), :]
        kcg = kc_ref[pl.ds(s0, ATT_GROUP), :, :].reshape(gc, D_KV).astype(BF16)
        vcg = vc_ref[pl.ds(s0, ATT_GROUP), :, :].reshape(gc, D_KV).astype(BF16)
        kng = knb[pl.ds(r0, gr), :]
        vng = vnb[pl.ds(r0, gr), :]
        outs = []
        for h in range(N_Q_HEADS):
            kv = h // GQA_GROUP
            lo, hi = kv * HEAD_DIM, (kv + 1) * HEAD_DIM
            qh = qg[:, h * HEAD_DIM:(h + 1) * HEAD_DIM]
            slope = _head_slope(h)
            s_c = jnp.where(valid_c, _dot_nt(qh, kcg[:, lo:hi]) - slope * dist_c, -jnp.inf)
            s_n = jnp.where(valid_n, _dot_nt(qh, kng[:, lo:hi]) - slope * dist_n, -jnp.inf)
            outs.append(_softmax_pv([(s_c, vcg[:, lo:hi]), (s_n, vng[:, lo:hi])], sinks_ref[h]))
        att_ref[pl.ds(r0, gr), :] = jnp.concatenate(outs, axis=-1).astype(BF16)
        return carry

    lax.fori_loop(0, ns // ATT_GROUP, att_group, 0)

    conv = _dot(nb, win_ref[:, OFF_CX:OFF_P])
    cx = conv[:, 0:D_CONV]
    cb = conv[:, D_CONV:2 * D_CONV]
    u = conv[:, 2 * D_CONV:3 * D_CONV] * cx
    cbuf[:, 6:8, :] = cst_ref[...]
    cbuf[:, 8:16, :] = u.reshape(ns, T, D_CONV)
    w0 = cw_ref[0:1, :].reshape(1, 1, D_CONV)
    w1 = cw_ref[1:2, :].reshape(1, 1, D_CONV)
    w2 = cw_ref[2:3, :].reshape(1, 1, D_CONV)
    yc = (w0 * cbuf[:, 6:14, :] + w1 * cbuf[:, 7:15, :] + w2 * cbuf[:, 8:16, :]).reshape(nr, D_CONV)
    co_ref[...] = cbuf[:, 14:16, :]

    ebuf[:, 1:16, :] = pin_ref[...]
    ebuf[:, 16:24, :] = _dot(nb, win_ref[:, OFF_P:OFF_G]).reshape(ns, T, D_POOL)
    d_groups = []
    for g, w in enumerate(POOL_WINDOWS):
        c0 = g * POOL_GROUP
        cur = ebuf[:, 16:24, c0:c0 + POOL_GROUP]
        tot = cur
        for j in range(1, w):
            tot = tot + ebuf[:, 16 - j:24 - j, c0:c0 + POOL_GROUP]
        d_groups.append((tot / float(w) - cur).reshape(nr, POOL_GROUP))
    po_ref[...] = ebuf[:, 9:24, :]

    h_ref[...] = _mix_tail(nb, x, att_ref[...], cx, cb, yc, d_groups,
                           win_ref, wao_ref, wco_ref, wp_ref, ps_ref, wo_ref)


def _sample_mix(l, x, x_row0, sinks, kc, vc, cst, pst, weights, h_buf, prev_states):
    ns, nr = SAMPLE_SEQS, SAMPLE_ROWS
    n_alias = 1 + len(prev_states)
    xb0 = x_row0 // nr
    hb0 = N_PROMPT // nr
    in_specs = [
        pl.BlockSpec(memory_space=pltpu.SMEM),
        pl.BlockSpec((nr, D_MODEL), lambda i: (xb0 + i, 0)),
        pl.BlockSpec((None, ns, WINDOW, D_KV), lambda i: (l, i, 0, 0)),
        pl.BlockSpec((None, ns, WINDOW, D_KV), lambda i: (l, i, 0, 0)),
        pl.BlockSpec((None, ns, CONV_WIDTH - 1, D_CONV), lambda i: (l, i, 0, 0)),
        pl.BlockSpec((None, ns, POOL_BUF, D_POOL), lambda i: (l, i, 0, 0)),
    ] + _weight_specs(l, 1) + [pl.BlockSpec(memory_space=pl.ANY)] * n_alias
    out_specs = [
        pl.BlockSpec((nr, D_MODEL), lambda i: (hb0 + i, 0)),
        pl.BlockSpec((None, ns, WINDOW, D_KV), lambda i: (l, i, 0, 0)),
        pl.BlockSpec((None, ns, WINDOW, D_KV), lambda i: (l, i, 0, 0)),
        pl.BlockSpec((None, ns, CONV_WIDTH - 1, D_CONV), lambda i: (l, i, 0, 0)),
        pl.BlockSpec((None, ns, POOL_BUF, D_POOL), lambda i: (l, i, 0, 0)),
    ]
    out_shape = [
        jax.ShapeDtypeStruct((N_PAD, D_MODEL), F32),
        jax.ShapeDtypeStruct((DEPTH, DEC_BATCH, WINDOW, D_KV), F32),
        jax.ShapeDtypeStruct((DEPTH, DEC_BATCH, WINDOW, D_KV), F32),
        jax.ShapeDtypeStruct((DEPTH, DEC_BATCH, CONV_WIDTH - 1, D_CONV), F32),
        jax.ShapeDtypeStruct((DEPTH, DEC_BATCH, POOL_BUF, D_POOL), F32),
    ]
    scratch = [
        pltpu.VMEM((nr, D_ATTN), BF16),
        pltpu.VMEM((nr, D_KV), BF16),
        pltpu.VMEM((nr, D_KV), BF16),
        pltpu.VMEM((nr, D_ATTN), BF16),
        pltpu.VMEM((ns, 16, D_CONV), F32),
        pltpu.VMEM((ns, 24, D_POOL), F32),
    ]
    outs = pl.pallas_call(
        functools.partial(_sample_mix_kernel, n_alias=n_alias),
        grid=(DEC_BATCH // ns,),
        in_specs=in_specs,
        out_specs=out_specs,
        out_shape=out_shape,
        scratch_shapes=scratch,
        input_output_aliases={14 + i: i for i in range(n_alias)},
        compiler_params=pltpu.CompilerParams(
            dimension_semantics=("arbitrary",), vmem_limit_bytes=VMEM_LIMIT),
        name=f"sample_mix_l{l}",
    )(sinks, x, kc, vc, cst, pst, *weights, h_buf, *prev_states)
    return outs[0], tuple(outs[1:])


def _ffn_kernel(h_ref, nw_ref, wg_ref, wu_ref, wd_ref, o_ref, hn_ref, acc_ref):
    c = pl.program_id(1)

    @pl.when(c == 0)
    def _():
        hn_ref[...] = _rms(h_ref[...], nw_ref[...]).astype(BF16)
        acc_ref[...] = jnp.zeros_like(acc_ref)

    hn = hn_ref[...]
    g = _dot(hn, wg_ref[...])
    u = _dot(hn, wu_ref[...])
    acc_ref[...] += _dot((g * _sigmoid(g) * u).astype(BF16), wd_ref[...])

    @pl.when(c == pl.num_programs(1) - 1)
    def _():
        o_ref[...] = h_ref[...] + acc_ref[...]


def _ffn_dense(h, nw, wg, wu, wd):
    tm = FFN_TILE
    nc = D_FF // FF_CHUNK
    return pl.pallas_call(
        _ffn_kernel,
        grid=(N_TOK // tm, nc),
        in_specs=[
            pl.BlockSpec((tm, D_MODEL), lambda i, c: (i, 0)),
            pl.BlockSpec((1, D_MODEL), lambda i, c: (0, 0)),
            pl.BlockSpec((None, D_MODEL, FF_CHUNK), lambda i, c: (0, 0, c)),
            pl.BlockSpec((None, D_MODEL, FF_CHUNK), lambda i, c: (0, 0, c)),
            pl.BlockSpec((None, FF_CHUNK, D_MODEL), lambda i, c: (0, c, 0)),
        ],
        out_specs=pl.BlockSpec((tm, D_MODEL), lambda i, c: (i, 0)),
        out_shape=jax.ShapeDtypeStruct((N_PAD, D_MODEL), F32),
        scratch_shapes=[pltpu.VMEM((tm, D_MODEL), BF16), pltpu.VMEM((tm, D_MODEL), F32)],
        compiler_params=pltpu.CompilerParams(
            dimension_semantics=("arbitrary", "arbitrary"), vmem_limit_bytes=VMEM_LIMIT),
        name="ffn_dense",
    )(h, nw, wg, wu, wd)


def _router_kernel(h_ref, nw_ref, rt_ref, hn_ref, slot_ref, col_ref, cnt_ref):
    tm = FFN_TILE
    hn = _rms(h_ref[...], nw_ref[...])
    hn_ref[...] = hn.astype(BF16)
    logits = lax.dot_general(rt_ref[...], hn, (((1,), (1,)), ((), ())),
                             precision=lax.Precision.HIGHEST, preferred_element_type=F32)
    eidx = lax.broadcasted_iota(jnp.int32, (N_EXPERTS, tm), 0).astype(F32)
    none = float(N_EXPERTS)
    m1 = jnp.max(logits, axis=0, keepdims=True)
    i1 = jnp.min(jnp.where(logits == m1, eidx, none), axis=0, keepdims=True)
    rest = jnp.where(eidx == i1, -jnp.inf, logits)
    m2 = jnp.max(rest, axis=0, keepdims=True)
    i2 = jnp.min(jnp.where(rest == m2, eidx, none), axis=0, keepdims=True)
    e2 = jnp.exp(m2 - m1)
    w1 = 1.0 / (1.0 + e2)
    w2 = e2 / (1.0 + e2)
    sel1 = eidx == i1
    sel2 = eidx == i2
    gate = jnp.where(sel1, w1, jnp.where(sel2, w2, 0.0))
    chosen = jnp.logical_or(sel1, sel2)
    mask = jnp.where(chosen, 1.0, 0.0)
    srow = lax.broadcasted_iota(jnp.int32, (tm, tm), 0)
    scol = lax.broadcasted_iota(jnp.int32, (tm, tm), 1)
    upper = jnp.where(srow < scol, 1.0, 0.0).astype(BF16)
    mask16 = jnp.concatenate([mask, jnp.zeros_like(mask)], axis=0).astype(BF16)
    slot = jnp.where(chosen, _dot(mask16, upper)[0:N_EXPERTS, :], -1.0)
    slot_ref[...] = slot.astype(jnp.int32)
    cnt_ref[...] = jnp.broadcast_to(jnp.sum(mask, axis=1, keepdims=True), (N_EXPERTS, 128))
    both = jnp.concatenate([slot, gate, jnp.zeros((128 - 2 * N_EXPERTS, tm), F32)], axis=0)
    col_ref[...] = both.T[:, 0:2 * N_EXPERTS]


def _router(h, nw, router_t):
    tm = FFN_TILE
    nt = N_UNITS
    return pl.pallas_call(
        _router_kernel,
        grid=(nt,),
        in_specs=[
            pl.BlockSpec((tm, D_MODEL), lambda i: (i, 0)),
            pl.BlockSpec((1, D_MODEL), lambda i: (0, 0)),
            pl.BlockSpec((N_EXPERTS, D_MODEL), lambda i: (0, 0)),
        ],
        out_specs=[
            pl.BlockSpec((tm, D_MODEL), lambda i: (i, 0)),
            pl.BlockSpec((N_EXPERTS, tm), lambda i: (0, i)),
            pl.BlockSpec((tm, 2 * N_EXPERTS), lambda i: (i, 0)),
            pl.BlockSpec((None, N_EXPERTS, 128), lambda i: (i, 0, 0)),
        ],
        out_shape=[
            jax.ShapeDtypeStruct((N_PAD, D_MODEL), BF16),
            jax.ShapeDtypeStruct((N_EXPERTS, N_PAD), jnp.int32),
            jax.ShapeDtypeStruct((N_PAD, 2 * N_EXPERTS), F32),
            jax.ShapeDtypeStruct((nt, N_EXPERTS, 128), F32),
        ],
        compiler_params=pltpu.CompilerParams(
            dimension_semantics=("arbitrary",), vmem_limit_bytes=VMEM_LIMIT),
        name="moe_router",
    )(h, nw, router_t)


def _plan(cnt):
    seg = -(-cnt // ROW_ALIGN) * ROW_ALIGN
    tiles_e = -(-jnp.sum(seg, axis=0) // EXP_TILE)
    cum_tiles = jnp.cumsum(tiles_e)
    base_e = (cum_tiles - tiles_e) * EXP_TILE
    dst = base_e[None, :] + jnp.cumsum(seg, axis=0) - seg
    n_tiles = cum_tiles[-1]
    i = jnp.arange(MAX_TILES, dtype=jnp.int32)
    tile_expert = jnp.minimum(jnp.searchsorted(cum_tiles, i, side="right"), N_EXPERTS - 1).astype(jnp.int32)
    tile_expert = jnp.where(i < n_tiles, tile_expert, tile_expert[jnp.maximum(n_tiles - 1, 0)])
    return dst.reshape(-1).astype(jnp.int32), tile_expert, n_tiles.reshape(1).astype(jnp.int32)


def _ceil_blocks(n, block):
    return (n + block - 1) >> (block.bit_length() - 1)


def _segment_copies(n_rows, src_at, dst_at, sem):
    n = n_rows >> (ROW_ALIGN.bit_length() - 1)
    out = []
    for bit in range(SEG_BITS - 1, -1, -1):
        size = ROW_ALIGN << bit
        off = pl.multiple_of(((n >> (bit + 1)) << (bit + 1)) * ROW_ALIGN, ROW_ALIGN)
        out.append((((n >> bit) & 1) == 1, pltpu.make_async_copy(src_at(off, size), dst_at(off, size), sem)))
    return out


def _start_all(copies):
    for pred, cp in copies:
        pl.when(pred)(cp.start)


def _wait_all(copies):
    for pred, cp in copies:
        pl.when(pred)(cp.wait)


def _dispatch_kernel(cnt_ref, dst_ref, hn_ref, srow_ref, init_ref, xs_ref, stage, sems):
    del init_ref
    tu, rb = FFN_TILE, ROW_BLOCK
    t = pl.program_id(0)

    def copies(e):
        slot = e % 2
        n_rows = _ceil_blocks(cnt_ref[t * N_EXPERTS + e], ROW_ALIGN) * ROW_ALIGN
        d0 = dst_ref[t * N_EXPERTS + e]
        return _segment_copies(
            n_rows,
            lambda off, size: stage.at[slot, pl.ds(off, size)],
            lambda off, size: xs_ref.at[pl.ds(pl.multiple_of(d0 + off, ROW_ALIGN), size)],
            sems.at[slot])

    for e in range(N_EXPERTS):
        slot = e % 2
        if e >= 2:
            _wait_all(copies(e - 2))
        slot_row = srow_ref[e:e + 1, :]

        def gather(b, carry, slot=slot, slot_row=slot_row):
            r0 = pl.multiple_of(b * rb, rb)
            rid = lax.broadcasted_iota(jnp.int32, (rb, tu), 0) + b * rb
            onehot = jnp.where(slot_row == rid, 1.0, 0.0).astype(BF16)
            stage[slot, pl.ds(r0, rb), :] = _dot(onehot, hn_ref[...]).astype(BF16)
            return carry

        lax.fori_loop(0, _ceil_blocks(cnt_ref[t * N_EXPERTS + e], rb), gather, 0)
        _start_all(copies(e))

    _wait_all(copies(N_EXPERTS - 2))
    _wait_all(copies(N_EXPERTS - 1))


def _dispatch(cnt, dst, hn, slot_row):
    tm = FFN_TILE
    n_rows = MAX_TILES * EXP_TILE
    grid_spec = pltpu.PrefetchScalarGridSpec(
        num_scalar_prefetch=2,
        grid=(N_UNITS,),
        in_specs=[
            pl.BlockSpec((tm, D_MODEL), lambda t, c, d: (t, 0)),
            pl.BlockSpec((N_EXPERTS, tm), lambda t, c, d: (0, t)),
            pl.BlockSpec(memory_space=pl.ANY),
        ],
        out_specs=pl.BlockSpec(memory_space=pl.ANY),
        scratch_shapes=[pltpu.VMEM((2, tm, D_MODEL), BF16), pltpu.SemaphoreType.DMA((2,))],
    )
    return pl.pallas_call(
        _dispatch_kernel,
        grid_spec=grid_spec,
        out_shape=jax.ShapeDtypeStruct((n_rows, D_MODEL), BF16),
        input_output_aliases={4: 0},
        compiler_params=pltpu.CompilerParams(
            dimension_semantics=("arbitrary",), vmem_limit_bytes=VMEM_LIMIT),
        name="moe_dispatch",
    )(cnt, dst, hn, slot_row, jnp.zeros((n_rows, D_MODEL), BF16))


def _experts_kernel(texp_ref, ntile_ref, xs_ref, wg_ref, wu_ref, wd_ref, ys_ref, acc_ref):
    del texp_ref
    i, c = pl.program_id(0), pl.program_id(1)
    nc = pl.num_programs(1)

    @pl.when(i < ntile_ref[0])
    def _():
        xb = xs_ref[...]
        g = _dot(xb, wg_ref[...].astype(BF16))
        u = _dot(xb, wu_ref[...].astype(BF16))
        part = _dot((g * _sigmoid(g) * u).astype(BF16), wd_ref[...].astype(BF16))

        @pl.when(c == 0)
        def _():
            acc_ref[...] = part

        @pl.when(c > 0)
        def _():
            acc_ref[...] += part

        @pl.when(c == nc - 1)
        def _():
            ys_ref[...] = acc_ref[...].astype(BF16)


def _experts(tile_expert, n_tiles, xs, wg, wu, wd):
    tm = EXP_TILE
    nc = D_EXPERT // EXP_CHUNK

    def row_map(i, c, te, nt):
        return (jnp.minimum(i, nt[0] - 1), 0)

    def chunk(i, c, nt):
        return jnp.where(i < nt[0], c, nc - 1)

    grid_spec = pltpu.PrefetchScalarGridSpec(
        num_scalar_prefetch=2,
        grid=(MAX_TILES, nc),
        in_specs=[
            pl.BlockSpec((tm, D_MODEL), row_map),
            pl.BlockSpec((None, None, D_MODEL, EXP_CHUNK), lambda i, c, te, nt: (0, te[i], 0, chunk(i, c, nt))),
            pl.BlockSpec((None, None, D_MODEL, EXP_CHUNK), lambda i, c, te, nt: (0, te[i], 0, chunk(i, c, nt))),
            pl.BlockSpec((None, None, EXP_CHUNK, D_MODEL), lambda i, c, te, nt: (0, te[i], chunk(i, c, nt), 0)),
        ],
        out_specs=pl.BlockSpec((tm, D_MODEL), row_map),
        scratch_shapes=[pltpu.VMEM((tm, D_MODEL), F32)],
    )
    return pl.pallas_call(
        _experts_kernel,
        grid_spec=grid_spec,
        out_shape=jax.ShapeDtypeStruct((MAX_TILES * tm, D_MODEL), BF16),
        compiler_params=pltpu.CompilerParams(
            dimension_semantics=("arbitrary", "arbitrary"), vmem_limit_bytes=VMEM_LIMIT),
        name="moe_experts",
    )(tile_expert, n_tiles, xs, wg, wu, wd)


def _combine_kernel(cnt_ref, dst_ref, ys_ref, col_ref, h_ref, nf_ref, op_ref, os_ref, stage, sems, acc_ref):
    tu, rb = FFN_TILE, 2 * ROW_BLOCK
    t = pl.program_id(0)

    @pl.when(t == 0)
    def _():
        stage[...] = jnp.zeros_like(stage)

    def copies(e):
        slot = e % 2
        n_rows = _ceil_blocks(cnt_ref[t * N_EXPERTS + e], ROW_ALIGN) * ROW_ALIGN
        d0 = dst_ref[t * N_EXPERTS + e]
        return _segment_copies(
            n_rows,
            lambda off, size: ys_ref.at[pl.ds(pl.multiple_of(d0 + off, ROW_ALIGN), size)],
            lambda off, size: stage.at[slot, pl.ds(off, size)],
            sems.at[slot])

    _start_all(copies(0))
    acc_ref[...] = h_ref[...]
    lane = lax.broadcasted_iota(jnp.int32, (tu, 2 * N_EXPERTS), 1)
    for e in range(N_EXPERTS):
        slot = e % 2
        if e + 1 < N_EXPERTS:
            _start_all(copies(e + 1))
        _wait_all(copies(e))
        cols = col_ref[...]
        slot_col = jnp.sum(jnp.where(lane == e, cols, 0.0), axis=1, keepdims=True)
        gate_col = jnp.sum(jnp.where(lane == e + N_EXPERTS, cols, 0.0), axis=1, keepdims=True)

        def scatter(b, carry, slot=slot, slot_col=slot_col, gate_col=gate_col):
            r0 = pl.multiple_of(b * rb, rb)
            cid = (lax.broadcasted_iota(jnp.int32, (tu, rb), 1) + b * rb).astype(F32)
            weighted = jnp.where(slot_col == cid, gate_col, 0.0).astype(BF16)
            acc_ref[...] += _dot(weighted, stage[slot, pl.ds(r0, rb), :])
            return carry

        lax.fori_loop(0, _ceil_blocks(cnt_ref[t * N_EXPERTS + e], rb), scatter, 0)

    out = _rms(acc_ref[...], nf_ref[...])

    @pl.when(t < N_PROMPT // FFN_TILE)
    def _():
        op_ref[...] = out

    @pl.when(t >= N_PROMPT // FFN_TILE)
    def _():
        os_ref[...] = out


def _combine(cnt, dst, ys, cols, h, nf):
    tm = FFN_TILE
    np_tiles = N_PROMPT // tm
    assert N_SAMPLE == tm
    grid_spec = pltpu.PrefetchScalarGridSpec(
        num_scalar_prefetch=2,
        grid=(N_UNITS,),
        in_specs=[
            pl.BlockSpec(memory_space=pl.ANY),
            pl.BlockSpec((tm, 2 * N_EXPERTS), lambda t, c, d: (t, 0)),
            pl.BlockSpec((tm, D_MODEL), lambda t, c, d: (t, 0)),
            pl.BlockSpec((1, D_MODEL), lambda t, c, d: (0, 0)),
        ],
        out_specs=[
            pl.BlockSpec((tm, D_MODEL), lambda t, c, d: (jnp.minimum(t, np_tiles - 1), 0)),
            pl.BlockSpec((tm, D_MODEL), lambda t, c, d: (0, 0)),
        ],
        scratch_shapes=[pltpu.VMEM((2, tm, D_MODEL), BF16), pltpu.SemaphoreType.DMA((2,)),
                        pltpu.VMEM((tm, D_MODEL), F32)],
    )
    return pl.pallas_call(
        _combine_kernel,
        grid_spec=grid_spec,
        out_shape=[
            jax.ShapeDtypeStruct((N_PROMPT, D_MODEL), F32),
            jax.ShapeDtypeStruct((N_SAMPLE, D_MODEL), F32),
        ],
        compiler_params=pltpu.CompilerParams(
            dimension_semantics=("arbitrary",), vmem_limit_bytes=VMEM_LIMIT),
        name="moe_combine",
    )(cnt, dst, ys, cols, h, nf)


def kernel(x_prompt, x_sample, cache_win_k, cache_win_v, state_conv, state_pool, norm_mix, w_in, attn_sinks,
           w_attn_out, conv_w, w_conv_out, w_pool, pool_scale, w_out, norm_ffn, ffn_w_gate, ffn_w_up, ffn_w_down,
           moe_router, moe_w_gate, moe_w_up, moe_w_down, norm_final):
    mix_weights = (norm_mix.reshape(DEPTH, 1, D_MODEL), w_in.astype(BF16), w_attn_out.astype(BF16), conv_w,
                   w_conv_out.astype(BF16), w_pool.astype(BF16), pool_scale.reshape(DEPTH, 1, D_MODEL),
                   w_out.astype(BF16))
    kc = cache_win_k.reshape(DEPTH, DEC_BATCH, WINDOW, D_KV)
    vc = cache_win_v.reshape(DEPTH, DEC_BATCH, WINDOW, D_KV)

    xp = x_prompt.reshape(N_PROMPT, D_MODEL)
    xs, xs_row0 = x_sample.reshape(N_SAMPLE, D_MODEL), 0
    p_states, s_states = (), ()
    for l in range(DEPTH):
        sinks = attn_sinks[l]
        h, p_states = _prompt_mix(l, xp, sinks, mix_weights, p_states)
        h, s_states = _sample_mix(l, xs, xs_row0, sinks, kc, vc, state_conv, state_pool, mix_weights, h, s_states)
        i = l // 2
        nw = norm_ffn[l].reshape(1, D_MODEL)
        if l % 2 == 0:
            xp = _ffn_dense(h, nw, ffn_w_gate[i:i + 1].astype(BF16), ffn_w_up[i:i + 1].astype(BF16),
                            ffn_w_down[i:i + 1].astype(BF16))
            xs, xs_row0 = xp, N_PROMPT
        else:
            hn, slot_row, cols, cnt = _router(h, nw, moe_router[i].T)
            cnt = cnt[:, :, 0].astype(jnp.int32)
            dst, tile_expert, n_tiles = _plan(cnt)
            cnt = cnt.reshape(-1)
            xs_sorted = _dispatch(cnt, dst, hn, slot_row)
            ys_sorted = _experts(tile_expert, n_tiles, xs_sorted, moe_w_gate[i:i + 1], moe_w_up[i:i + 1],
                                 moe_w_down[i:i + 1])
            y_prompt, y_sample = _combine(cnt, dst, ys_sorted, cols, h, norm_final.reshape(1, D_MODEL))

    pk, pv, pc, pp = p_states
    sk, sv, sc, sp = s_states
    kv_shape_p = (DEPTH, BATCH, WINDOW, N_KV_HEADS, HEAD_DIM)
    kv_shape_s = (DEPTH, DEC_BATCH, WINDOW, N_KV_HEADS, HEAD_DIM)
    return (y_prompt.reshape(BATCH, SEQ, D_MODEL), y_sample.reshape(DEC_BATCH, DEC_SEQ, D_MODEL),
            pk.reshape(kv_shape_p), pv.reshape(kv_shape_p), pc, pp,
            sk.reshape(kv_shape_s), sv.reshape(kv_shape_s), sc, sp)
```

```python
import functools
import math

import jax
import jax.numpy as jnp
from jax import lax
from jax.experimental import pallas as pl
from jax.experimental.pallas import tpu as pltpu

F32 = jnp.float32
BF16 = jnp.bfloat16

D_MODEL = 1024
BATCH = 4
SEQ = 4096
DEPTH = 2
DEC_BATCH = 128
DEC_SEQ = 8
HEAD_DIM = 64
N_Q_HEADS = 8
N_KV_HEADS = 2
GQA_GROUP = N_Q_HEADS // N_KV_HEADS
D_ATTN = N_Q_HEADS * HEAD_DIM
D_KV = N_KV_HEADS * HEAD_DIM
WINDOW = 128
ATTN_SCALE = HEAD_DIM ** -0.5
D_CONV = D_MODEL // 2
CONV_WIDTH = 3
D_POOL = D_MODEL // 2
POOL_WINDOWS = (2, 4, 8, 16)
POOL_GROUP = D_POOL // len(POOL_WINDOWS)
POOL_OUT_GROUP = D_MODEL // len(POOL_WINDOWS)
POOL_BUF = max(POOL_WINDOWS) - 1
OFF_Q = 0
OFF_K = OFF_Q + D_ATTN
OFF_V = OFF_K + D_KV
OFF_CX = OFF_V + D_KV
OFF_CB = OFF_CX + D_CONV
OFF_CC = OFF_CB + D_CONV
OFF_P = OFF_CC + D_CONV
OFF_G = OFF_P + D_POOL
D_IN = OFF_G + 3 * D_MODEL
D_FF = 2816
N_EXPERTS = 8
D_EXPERT = 3584
EPS = 1e-5
LOG2E = math.log2(math.e)

N_PROMPT = BATCH * SEQ
N_SAMPLE = DEC_BATCH * DEC_SEQ
N_TOK = N_PROMPT + N_SAMPLE

PROMPT_TILE = 512
SAMPLE_SEQS = 32
SAMPLE_ROWS = SAMPLE_SEQS * DEC_SEQ
ATT_GROUP = 8
FFN_TILE = 1024
EXP_CHUNK = D_EXPERT // 4
SUB_ROWS = 256
ROW_BLOCK = 128
ROW_ALIGN = 16
EXP_TILE = 1024
N_UNITS = N_TOK // FFN_TILE
MAX_TILES = -(-(2 * N_TOK + N_UNITS * N_EXPERTS * (ROW_ALIGN - 1)) // EXP_TILE) + N_EXPERTS
N_PAD = N_TOK
SEG_BITS = 7
assert FFN_TILE == ROW_ALIGN << (SEG_BITS - 1)
VMEM_LIMIT = 58 * 1024 * 1024


def _dot(a, b):
    return jnp.dot(a, b, preferred_element_type=F32)


def _dot_nt(a, b):
    return lax.dot_general(a, b, (((1,), (1,)), ((), ())), preferred_element_type=F32)


def _rms(x, g):
    return x * lax.rsqrt(jnp.mean(x * x, axis=-1, keepdims=True) + EPS) * g


def _sigmoid(x):
    return 1.0 / (1.0 + jnp.exp(-x))


def _head_slope(h):
    return float(2.0 ** (-8.0 * (h + 1) / N_Q_HEADS))


def _softmax_pv(parts, sink):
    m = sink
    for s, _ in parts:
        m = jnp.maximum(m, jnp.max(s, axis=-1, keepdims=True))
    den = jnp.exp(sink - m)
    o = None
    for s, v in parts:
        p = jnp.exp(s - m)
        den = den + jnp.sum(p, axis=-1, keepdims=True)
        pv = _dot(p.astype(BF16), v)
        o = pv if o is None else o + pv
    return o / den


def _mix_tail(nb, x, att_bf, cx, cb, yc, d_groups, win_ref, wao_ref, wco_ref, wp_ref, ps_ref, wo_ref):
    att_o = _dot(att_bf, wao_ref[...])
    merged = _sigmoid(_dot(nb, win_ref[:, OFF_G:OFF_G + D_MODEL])) * att_o
    conv_o = _dot((cb * yc).astype(BF16), wco_ref[...])
    merged = merged + _sigmoid(_dot(nb, win_ref[:, OFF_G + D_MODEL:OFF_G + 2 * D_MODEL])) * conv_o
    pool_o = jnp.concatenate([_dot(d.astype(BF16), wp_ref[g]) for g, d in enumerate(d_groups)], axis=-1)
    pool_o = pool_o * ps_ref[...]
    merged = merged + _sigmoid(_dot(nb, win_ref[:, OFF_G + 2 * D_MODEL:OFF_G + 3 * D_MODEL])) * pool_o
    return x + _dot(merged.astype(BF16), wo_ref[...])


def _prompt_mix_kernel(*refs, n_alias):
    (sinks_ref, x_ref, nw_ref, win_ref, wao_ref, cw_ref, wco_ref, wp_ref, ps_ref, wo_ref) = refs[:10]
    (h_ref, kw_ref, vw_ref, cs_ref, pst_ref,
     qbuf, kbuf, vbuf, att_ref, ubuf, pbuf, bias_ref) = refs[10 + n_alias:]
    tm = PROMPT_TILE
    b = pl.program_id(0)
    t = pl.program_id(1)
    x = x_ref[...]
    nb = _rms(x, nw_ref[...]).astype(BF16)
    lane = lax.broadcasted_iota(jnp.int32, (tm, D_KV), 1)
    low = lane < HEAD_DIM

    @pl.when(jnp.logical_and(b == 0, t == 0))
    def _():
        qi = lax.broadcasted_iota(jnp.int32, (WINDOW, 2 * WINDOW), 0)
        si = lax.broadcasted_iota(jnp.int32, (WINDOW, 2 * WINDOW), 1)
        dist_i = qi + WINDOW - si
        dist = dist_i.astype(F32)
        band = jnp.logical_and(dist_i >= 0, dist_i <= WINDOW)
        for h in range(N_Q_HEADS):
            bias = jnp.where(band, (-_head_slope(h) * LOG2E) * dist, -jnp.inf)
            bias_ref[0, h] = bias
            bias_ref[1, h] = jnp.where(si >= WINDOW, bias, -jnp.inf)

    @pl.when(t == 0)
    def _():
        for i in range(4):
            kbuf[i, 0:WINDOW, :] = jnp.zeros((WINDOW, D_KV), BF16)
            vbuf[i, 0:WINDOW, :] = jnp.zeros((WINDOW, D_KV), BF16)
        ubuf[0:8, :] = jnp.zeros((8, D_CONV), F32)
        pbuf[0:16, :] = jnp.zeros((16, D_POOL), F32)

    @pl.when(t > 0)
    def _():
        for i in range(4):
            kbuf[i, 0:WINDOW, :] = kbuf[i, tm:tm + WINDOW, :]
            vbuf[i, 0:WINDOW, :] = vbuf[i, tm:tm + WINDOW, :]
        ubuf[0:8, :] = ubuf[tm:tm + 8, :]
        pbuf[0:16, :] = pbuf[tm:tm + 16, :]

    qkv = _dot(nb, win_ref[:, OFF_Q:OFF_CX])
    qbuf[...] = (qkv[:, OFF_Q:OFF_K] * (ATTN_SCALE * LOG2E)).astype(BF16)
    k = qkv[:, OFF_K:OFF_V]
    v = qkv[:, OFF_V:OFF_CX]
    kw_ref[...] = k[tm - WINDOW:, :]
    vw_ref[...] = v[tm - WINDOW:, :]
    k_sw = pltpu.roll(k, HEAD_DIM, 1)
    v_sw = pltpu.roll(v, HEAD_DIM, 1)
    k_var = (jnp.where(low, k, 0.0), jnp.where(low, 0.0, k_sw), jnp.where(low, k_sw, 0.0), jnp.where(low, 0.0, k))
    v_var = (jnp.where(low, v, 1.0), jnp.where(low, 1.0, v_sw), jnp.where(low, v_sw, 1.0), jnp.where(low, 1.0, v))
    for i in range(4):
        kbuf[i, WINDOW:WINDOW + tm, :] = k_var[i].astype(BF16)
        vbuf[i, WINDOW:WINDOW + tm, :] = v_var[i].astype(BF16)

    lane_q = lax.broadcasted_iota(jnp.int32, (WINDOW, 2 * HEAD_DIM), 1)
    low_q = lane_q < HEAD_DIM

    def att_block(j, carry):
        r0 = pl.multiple_of(j * WINDOW, WINDOW)
        first = jnp.where(jnp.logical_and(t == 0, j == 0), 1, 0)
        for pair in range(N_Q_HEADS // 2):
            qp = qbuf[pl.ds(r0, WINDOW), pair * 2 * HEAD_DIM:(pair + 1) * 2 * HEAD_DIM]
            halves = []
            for half in range(2):
                h = 2 * pair + half
                var = 2 * (h // GQA_GROUP) + half
                s = _dot_nt(qp, kbuf[var, pl.ds(r0, 2 * WINDOW), :]) + bias_ref[first, h]
                sink = sinks_ref[h] * LOG2E
                m = jnp.maximum(jnp.max(s, axis=-1, keepdims=True), sink)
                p = jnp.exp2(s - m).astype(BF16)
                o = _dot(p, vbuf[var, pl.ds(r0, 2 * WINDOW), :])
                den = (o[:, HEAD_DIM:HEAD_DIM + 1] if half == 0 else o[:, 0:1]) + jnp.exp2(sink - m)
                halves.append(o / den)
            att_ref[pl.ds(r0, WINDOW), pair * 2 * HEAD_DIM:(pair + 1) * 2 * HEAD_DIM] = (
                jnp.where(low_q, halves[0], halves[1]).astype(BF16))
        return carry

    lax.fori_loop(0, tm // WINDOW, att_block, 0)

    conv = _dot(nb, win_ref[:, OFF_CX:OFF_P])
    cx = conv[:, 0:D_CONV]
    cb = conv[:, D_CONV:2 * D_CONV]
    ubuf[8:8 + tm, :] = conv[:, 2 * D_CONV:3 * D_CONV] * cx
    yc = (cw_ref[0:1, :] * ubuf[6:6 + tm, :] + cw_ref[1:2, :] * ubuf[7:7 + tm, :]
          + cw_ref[2:3, :] * ubuf[8:8 + tm, :])
    cs_ref[...] = ubuf[tm + 6:tm + 8, :]

    pbuf[16:16 + tm, :] = _dot(nb, win_ref[:, OFF_P:OFF_G])
    pos = lax.broadcasted_iota(jnp.int32, (tm, 1), 0) + t * tm + 1
    d_groups = []
    for g, w in enumerate(POOL_WINDOWS):
        c0 = g * POOL_GROUP
        cur = pbuf[16:16 + tm, c0:c0 + POOL_GROUP]
        tot = cur
        for j in range(1, w):
            tot = tot + pbuf[16 - j:16 - j + tm, c0:c0 + POOL_GROUP]
        cnt = jnp.minimum(pos, w).astype(F32)
        d_groups.append(tot / cnt - cur)
    pst_ref[...] = pbuf[tm + 1:tm + 16, :]

    h_ref[...] = _mix_tail(nb, x, att_ref[...], cx, cb, yc, d_groups,
                           win_ref, wao_ref, wco_ref, wp_ref, ps_ref, wo_ref)


def _weight_specs(l, grid_rank):
    def const(*idx):
        if grid_rank == 1:
            return lambda i: idx
        return lambda b, t: idx

    return [
        pl.BlockSpec((None, 1, D_MODEL), const(l, 0, 0)),
        pl.BlockSpec((None, D_MODEL, D_IN), const(l, 0, 0)),
        pl.BlockSpec((None, D_ATTN, D_MODEL), const(l, 0, 0)),
        pl.BlockSpec((None, CONV_WIDTH, D_CONV), const(l, 0, 0)),
        pl.BlockSpec((None, D_CONV, D_MODEL), const(l, 0, 0)),
        pl.BlockSpec((None, len(POOL_WINDOWS), POOL_GROUP, POOL_OUT_GROUP), const(l, 0, 0, 0)),
        pl.BlockSpec((None, 1, D_MODEL), const(l, 0, 0)),
        pl.BlockSpec((None, D_MODEL, D_MODEL), const(l, 0, 0)),
    ]


def _prompt_mix(l, x, sinks, weights, prev_states):
    tm = PROMPT_TILE
    nt = SEQ // tm
    n_alias = len(prev_states)
    in_specs = ([pl.BlockSpec(memory_space=pltpu.SMEM),
                 pl.BlockSpec((tm, D_MODEL), lambda b, t: (b * nt + t, 0))] + _weight_specs(l, 2)
                + [pl.BlockSpec(memory_space=pl.ANY)] * n_alias)
    out_specs = [
        pl.BlockSpec((tm, D_MODEL), lambda b, t: (b * nt + t, 0)),
        pl.BlockSpec((None, None, WINDOW, D_KV), lambda b, t: (l, b, 0, 0)),
        pl.BlockSpec((None, None, WINDOW, D_KV), lambda b, t: (l, b, 0, 0)),
        pl.BlockSpec((None, None, CONV_WIDTH - 1, D_CONV), lambda b, t: (l, b, 0, 0)),
        pl.BlockSpec((None, None, POOL_BUF, D_POOL), lambda b, t: (l, b, 0, 0)),
    ]
    out_shape = [
        jax.ShapeDtypeStruct((N_PAD, D_MODEL), F32),
        jax.ShapeDtypeStruct((DEPTH, BATCH, WINDOW, D_KV), F32),
        jax.ShapeDtypeStruct((DEPTH, BATCH, WINDOW, D_KV), F32),
        jax.ShapeDtypeStruct((DEPTH, BATCH, CONV_WIDTH - 1, D_CONV), F32),
        jax.ShapeDtypeStruct((DEPTH, BATCH, POOL_BUF, D_POOL), F32),
    ]
    scratch = [
        pltpu.VMEM((tm, D_ATTN), BF16),
        pltpu.VMEM((4, WINDOW + tm, D_KV), BF16),
        pltpu.VMEM((4, WINDOW + tm, D_KV), BF16),
        pltpu.VMEM((tm, D_ATTN), BF16),
        pltpu.VMEM((8 + tm, D_CONV), F32),
        pltpu.VMEM((16 + tm, D_POOL), F32),
        pltpu.VMEM((2, N_Q_HEADS, WINDOW, 2 * WINDOW), F32),
    ]
    outs = pl.pallas_call(
        functools.partial(_prompt_mix_kernel, n_alias=n_alias),
        grid=(BATCH, nt),
        in_specs=in_specs,
        out_specs=out_specs,
        out_shape=out_shape,
        scratch_shapes=scratch,
        input_output_aliases={10 + i: 1 + i for i in range(n_alias)},
        compiler_params=pltpu.CompilerParams(
            dimension_semantics=("arbitrary", "arbitrary"), vmem_limit_bytes=VMEM_LIMIT),
        name=f"prompt_mix_l{l}",
    )(sinks, x, *weights, *prev_states)
    return outs[0], tuple(outs[1:])


def _sample_mix_kernel(*refs, n_alias):
    (sinks_ref, x_ref, kc_ref, vc_ref, cst_ref, pin_ref,
     nw_ref, win_ref, wao_ref, cw_ref, wco_ref, wp_ref, ps_ref, wo_ref) = refs[:14]
    (h_ref, ko_ref, vo_ref, co_ref, po_ref,
     qbuf, knb, vnb, att_ref, cbuf, ebuf) = refs[14 + n_alias:]
    ns, nr, T = SAMPLE_SEQS, SAMPLE_ROWS, DEC_SEQ
    x = x_ref[...]
    nb = _rms(x, nw_ref[...]).astype(BF16)

    qkv = _dot(nb, win_ref[:, OFF_Q:OFF_CX])
    qbuf[...] = (qkv[:, OFF_Q:OFF_K] * ATTN_SCALE).astype(BF16)
    k = qkv[:, OFF_K:OFF_V]
    v = qkv[:, OFF_V:OFF_CX]
    knb[...] = k.astype(BF16)
    vnb[...] = v.astype(BF16)
    ko_ref[:, 0:WINDOW - T, :] = kc_ref[:, T:WINDOW, :]
    vo_ref[:, 0:WINDOW - T, :] = vc_ref[:, T:WINDOW, :]
    ko_ref[:, WINDOW - T:WINDOW, :] = k.reshape(ns, T, D_KV)
    vo_ref[:, WINDOW - T:WINDOW, :] = v.reshape(ns, T, D_KV)

    gr = ATT_GROUP * T
    gc = ATT_GROUP * WINDOW
    rq = lax.broadcasted_iota(jnp.int32, (gr, gc), 0)
    cq = lax.broadcasted_iota(jnp.int32, (gr, gc), 1)
    tq = rq & (T - 1)
    sc_pos = cq & (WINDOW - 1)
    valid_c = jnp.logical_and((rq >> 3) == (cq >> 7), sc_pos >= tq)
    dist_c = (WINDOW + tq - sc_pos).astype(F32)
    rn = lax.broadcasted_iota(jnp.int32, (gr, gr), 0)
    cn = lax.broadcasted_iota(jnp.int32, (gr, gr), 1)
    tn = rn & (T - 1)
    jn = cn & (T - 1)
    valid_n = jnp.logical_and((rn >> 3) == (cn >> 3), jn <= tn)
    dist_n = (tn - jn).astype(F32)

    def att_group(gi, carry):
        r0 = pl.multiple_of(gi * gr, gr)
        s0 = pl.multiple_of(gi * ATT_GROUP, ATT_GROUP)
        qg = qbuf[pl.ds(r0, gr), :]
        kcg = kc_ref[pl.ds(s0, ATT_GROUP), :, :].reshape(gc, D_KV).astype(BF16)
        vcg = vc_ref[pl.ds(s0, ATT_GROUP), :, :].reshape(gc, D_KV).astype(BF16)
        kng = knb[pl.ds(r0, gr), :]
        vng = vnb[pl.ds(r0, gr), :]
        outs = []
        for h in range(N_Q_HEADS):
            kv = h // GQA_GROUP
            lo, hi = kv * HEAD_DIM, (kv + 1) * HEAD_DIM
            qh = qg[:, h * HEAD_DIM:(h + 1) * HEAD_DIM]
            slope = _head_slope(h)
            s_c = jnp.where(valid_c, _dot_nt(qh, kcg[:, lo:hi]) - slope * dist_c, -jnp.inf)
            s_n = jnp.where(valid_n, _dot_nt(qh, kng[:, lo:hi]) - slope * dist_n, -jnp.inf)
            outs.append(_softmax_pv([(s_c, vcg[:, lo:hi]), (s_n, vng[:, lo:hi])], sinks_ref[h]))
        att_ref[pl.ds(r0, gr), :] = jnp.concatenate(outs, axis=-1).astype(BF16)
        return carry

    lax.fori_loop(0, ns // ATT_GROUP, att_group, 0)

    conv = _dot(nb, win_ref[:, OFF_CX:OFF_P])
    cx = conv[:, 0:D_CONV]
    cb = conv[:, D_CONV:2 * D_CONV]
    u = conv[:, 2 * D_CONV:3 * D_CONV] * cx
    cbuf[:, 6:8, :] = cst_ref[...]
    cbuf[:, 8:16, :] = u.reshape(ns, T, D_CONV)
    w0 = cw_ref[0:1, :].reshape(1, 1, D_CONV)
    w1 = cw_ref[1:2, :].reshape(1, 1, D_CONV)
    w2 = cw_ref[2:3, :].reshape(1, 1, D_CONV)
    yc = (w0 * cbuf[:, 6:14, :] + w1 * cbuf[:, 7:15, :] + w2 * cbuf[:, 8:16, :]).reshape(nr, D_CONV)
    co_ref[...] = cbuf[:, 14:16, :]

    ebuf[:, 1:16, :] = pin_ref[...]
    ebuf[:, 16:24, :] = _dot(nb, win_ref[:, OFF_P:OFF_G]).reshape(ns, T, D_POOL)
    d_groups = []
    for g, w in enumerate(POOL_WINDOWS):
        c0 = g * POOL_GROUP
        cur = ebuf[:, 16:24, c0:c0 + POOL_GROUP]
        tot = cur
        for j in range(1, w):
            tot = tot + ebuf[:, 16 - j:24 - j, c0:c0 + POOL_GROUP]
        d_groups.append((tot / float(w) - cur).reshape(nr, POOL_GROUP))
    po_ref[...] = ebuf[:, 9:24, :]

    h_ref[...] = _mix_tail(nb, x, att_ref[...], cx, cb, yc, d_groups,
                           win_ref, wao_ref, wco_ref, wp_ref, ps_ref, wo_ref)


def _sample_mix(l, x, x_row0, sinks, kc, vc, cst, pst, weights, h_buf, prev_states):
    ns, nr = SAMPLE_SEQS, SAMPLE_ROWS
    n_alias = 1 + len(prev_states)
    xb0 = x_row0 // nr
    hb0 = N_PROMPT // nr
    in_specs = [
        pl.BlockSpec(memory_space=pltpu.SMEM),
        pl.BlockSpec((nr, D_MODEL), lambda i: (xb0 + i, 0)),
        pl.BlockSpec((None, ns, WINDOW, D_KV), lambda i: (l, i, 0, 0)),
        pl.BlockSpec((None, ns, WINDOW, D_KV), lambda i: (l, i, 0, 0)),
        pl.BlockSpec((None, ns, CONV_WIDTH - 1, D_CONV), lambda i: (l, i, 0, 0)),
        pl.BlockSpec((None, ns, POOL_BUF, D_POOL), lambda i: (l, i, 0, 0)),
    ] + _weight_specs(l, 1) + [pl.BlockSpec(memory_space=pl.ANY)] * n_alias
    out_specs = [
        pl.BlockSpec((nr, D_MODEL), lambda i: (hb0 + i, 0)),
        pl.BlockSpec((None, ns, WINDOW, D_KV), lambda i: (l, i, 0, 0)),
        pl.BlockSpec((None, ns, WINDOW, D_KV), lambda i: (l, i, 0, 0)),
        pl.BlockSpec((None, ns, CONV_WIDTH - 1, D_CONV), lambda i: (l, i, 0, 0)),
        pl.BlockSpec((None, ns, POOL_BUF, D_POOL), lambda i: (l, i, 0, 0)),
    ]
    out_shape = [
        jax.ShapeDtypeStruct((N_PAD, D_MODEL), F32),
        jax.ShapeDtypeStruct((DEPTH, DEC_BATCH, WINDOW, D_KV), F32),
        jax.ShapeDtypeStruct((DEPTH, DEC_BATCH, WINDOW, D_KV), F32),
        jax.ShapeDtypeStruct((DEPTH, DEC_BATCH, CONV_WIDTH - 1, D_CONV), F32),
        jax.ShapeDtypeStruct((DEPTH, DEC_BATCH, POOL_BUF, D_POOL), F32),
    ]
    scratch = [
        pltpu.VMEM((nr, D_ATTN), BF16),
        pltpu.VMEM((nr, D_KV), BF16),
        pltpu.VMEM((nr, D_KV), BF16),
        pltpu.VMEM((nr, D_ATTN), BF16),
        pltpu.VMEM((ns, 16, D_CONV), F32),
        pltpu.VMEM((ns, 24, D_POOL), F32),
    ]
    outs = pl.pallas_call(
        functools.partial(_sample_mix_kernel, n_alias=n_alias),
        grid=(DEC_BATCH // ns,),
        in_specs=in_specs,
        out_specs=out_specs,
        out_shape=out_shape,
        scratch_shapes=scratch,
        input_output_aliases={14 + i: i for i in range(n_alias)},
        compiler_params=pltpu.CompilerParams(
            dimension_semantics=("arbitrary",), vmem_limit_bytes=VMEM_LIMIT),
        name=f"sample_mix_l{l}",
    )(sinks, x, kc, vc, cst, pst, *weights, h_buf, *prev_states)
    return outs[0], tuple(outs[1:])


def _ffn_kernel(h_ref, nw_ref, wg_ref, wu_ref, wd_ref, o_ref):
    for r in range(0, FFN_TILE, SUB_ROWS):
        h = h_ref[r:r + SUB_ROWS, :]
        hn = _rms(h, nw_ref[...]).astype(BF16)
        g = _dot(hn, wg_ref[...])
        u = _dot(hn, wu_ref[...])
        o_ref[r:r + SUB_ROWS, :] = h + _dot((g * _sigmoid(g) * u).astype(BF16), wd_ref[...])


def _ffn_dense(h, nw, wg, wu, wd):
    tm = FFN_TILE
    resident = dict(pipeline_mode=pl.Buffered(1))
    return pl.pallas_call(
        _ffn_kernel,
        grid=(N_TOK // tm,),
        in_specs=[
            pl.BlockSpec((tm, D_MODEL), lambda i: (i, 0)),
            pl.BlockSpec((1, D_MODEL), lambda i: (0, 0)),
            pl.BlockSpec((None, D_MODEL, D_FF), lambda i: (0, 0, 0), **resident),
            pl.BlockSpec((None, D_MODEL, D_FF), lambda i: (0, 0, 0), **resident),
            pl.BlockSpec((None, D_FF, D_MODEL), lambda i: (0, 0, 0), **resident),
        ],
        out_specs=pl.BlockSpec((tm, D_MODEL), lambda i: (i, 0)),
        out_shape=jax.ShapeDtypeStruct((N_PAD, D_MODEL), F32),
        compiler_params=pltpu.CompilerParams(
            dimension_semantics=("arbitrary",), vmem_limit_bytes=VMEM_LIMIT),
        name="ffn_dense",
    )(h, nw, wg, wu, wd)


def _router_kernel(h_ref, nw_ref, rt_ref, hn_ref, slot_ref, col_ref, cnt_ref):
    tm = FFN_TILE
    hn = _rms(h_ref[...], nw_ref[...])
    hn_ref[...] = hn.astype(BF16)
    logits = lax.dot_general(rt_ref[...], hn, (((1,), (1,)), ((), ())),
                             precision=lax.Precision.HIGHEST, preferred_element_type=F32)
    eidx = lax.broadcasted_iota(jnp.int32, (N_EXPERTS, tm), 0).astype(F32)
    none = float(N_EXPERTS)
    m1 = jnp.max(logits, axis=0, keepdims=True)
    i1 = jnp.min(jnp.where(logits == m1, eidx, none), axis=0, keepdims=True)
    rest = jnp.where(eidx == i1, -jnp.inf, logits)
    m2 = jnp.max(rest, axis=0, keepdims=True)
    i2 = jnp.min(jnp.where(rest == m2, eidx, none), axis=0, keepdims=True)
    e2 = jnp.exp(m2 - m1)
    w1 = 1.0 / (1.0 + e2)
    w2 = e2 / (1.0 + e2)
    sel1 = eidx == i1
    sel2 = eidx == i2
    gate = jnp.where(sel1, w1, jnp.where(sel2, w2, 0.0))
    chosen = jnp.logical_or(sel1, sel2)
    mask = jnp.where(chosen, 1.0, 0.0)
    srow = lax.broadcasted_iota(jnp.int32, (tm, tm), 0)
    scol = lax.broadcasted_iota(jnp.int32, (tm, tm), 1)
    upper = jnp.where(srow < scol, 1.0, 0.0).astype(BF16)
    mask16 = jnp.concatenate([mask, jnp.zeros_like(mask)], axis=0).astype(BF16)
    slot = jnp.where(chosen, _dot(mask16, upper)[0:N_EXPERTS, :], -1.0)
    slot_ref[...] = slot.astype(jnp.int32)
    cnt_ref[...] = jnp.broadcast_to(jnp.sum(mask, axis=1, keepdims=True), (N_EXPERTS, 128))
    both = jnp.concatenate([slot, gate, jnp.zeros((128 - 2 * N_EXPERTS, tm), F32)], axis=0)
    col_ref[...] = both.T[:, 0:2 * N_EXPERTS]


def _router(h, nw, router_t):
    tm = FFN_TILE
    nt = N_UNITS
    return pl.pallas_call(
        _router_kernel,
        grid=(nt,),
        in_specs=[
            pl.BlockSpec((tm, D_MODEL), lambda i: (i, 0)),
            pl.BlockSpec((1, D_MODEL), lambda i: (0, 0)),
            pl.BlockSpec((N_EXPERTS, D_MODEL), lambda i: (0, 0)),
        ],
        out_specs=[
            pl.BlockSpec((tm, D_MODEL), lambda i: (i, 0)),
            pl.BlockSpec((N_EXPERTS, tm), lambda i: (0, i)),
            pl.BlockSpec((tm, 2 * N_EXPERTS), lambda i: (i, 0)),
            pl.BlockSpec((None, N_EXPERTS, 128), lambda i: (i, 0, 0)),
        ],
        out_shape=[
            jax.ShapeDtypeStruct((N_PAD, D_MODEL), BF16),
            jax.ShapeDtypeStruct((N_EXPERTS, N_PAD), jnp.int32),
            jax.ShapeDtypeStruct((N_PAD, 2 * N_EXPERTS), F32),
            jax.ShapeDtypeStruct((nt, N_EXPERTS, 128), F32),
        ],
        compiler_params=pltpu.CompilerParams(
            dimension_semantics=("arbitrary",), vmem_limit_bytes=VMEM_LIMIT),
        name="moe_router",
    )(h, nw, router_t)


def _plan(cnt):
    seg = -(-cnt // ROW_ALIGN) * ROW_ALIGN
    tiles_e = -(-jnp.sum(seg, axis=0) // EXP_TILE)
    cum_tiles = jnp.cumsum(tiles_e)
    base_e = (cum_tiles - tiles_e) * EXP_TILE
    dst = base_e[None, :] + jnp.cumsum(seg, axis=0) - seg
    n_tiles = cum_tiles[-1]
    i = jnp.arange(MAX_TILES, dtype=jnp.int32)
    tile_expert = jnp.sum((cum_tiles[None, :] <= i[:, None]).astype(jnp.int32), axis=1)
    tile_expert = jnp.minimum(tile_expert, N_EXPERTS - 1)
    tile_expert = jnp.where(i < n_tiles, tile_expert, tile_expert[jnp.maximum(n_tiles - 1, 0)])
    return dst.reshape(-1).astype(jnp.int32), tile_expert, n_tiles.reshape(1).astype(jnp.int32)


def _ceil_blocks(n, block):
    return (n + block - 1) >> (block.bit_length() - 1)


def _segment_copies(n_rows, src_at, dst_at, sem):
    n = n_rows >> (ROW_ALIGN.bit_length() - 1)
    out = []
    for bit in range(SEG_BITS - 1, -1, -1):
        size = ROW_ALIGN << bit
        off = pl.multiple_of(((n >> (bit + 1)) << (bit + 1)) * ROW_ALIGN, ROW_ALIGN)
        out.append((((n >> bit) & 1) == 1, pltpu.make_async_copy(src_at(off, size), dst_at(off, size), sem)))
    return out


def _start_all(copies):
    for pred, cp in copies:
        pl.when(pred)(cp.start)


def _wait_all(copies):
    for pred, cp in copies:
        pl.when(pred)(cp.wait)


def _dispatch_kernel(cnt_ref, dst_ref, hn_ref, srow_ref, init_ref, xs_ref, stage, sems):
    del init_ref
    tu, rb = FFN_TILE, ROW_BLOCK
    t = pl.program_id(0)

    def copies(e):
        slot = e % 2
        n_rows = _ceil_blocks(cnt_ref[t * N_EXPERTS + e], ROW_ALIGN) * ROW_ALIGN
        d0 = dst_ref[t * N_EXPERTS + e]
        return _segment_copies(
            n_rows,
            lambda off, size: stage.at[slot, pl.ds(off, size)],
            lambda off, size: xs_ref.at[pl.ds(pl.multiple_of(d0 + off, ROW_ALIGN), size)],
            sems.at[slot])

    for e in range(N_EXPERTS):
        slot = e % 2
        if e >= 2:
            _wait_all(copies(e - 2))
        slot_row = srow_ref[e:e + 1, :]

        def gather(b, carry, slot=slot, slot_row=slot_row):
            r0 = pl.multiple_of(b * rb, rb)
            rid = lax.broadcasted_iota(jnp.int32, (rb, tu), 0) + b * rb
            onehot = jnp.where(slot_row == rid, 1.0, 0.0).astype(BF16)
            stage[slot, pl.ds(r0, rb), :] = _dot(onehot, hn_ref[...]).astype(BF16)
            return carry

        lax.fori_loop(0, _ceil_blocks(cnt_ref[t * N_EXPERTS + e], rb), gather, 0)
        _start_all(copies(e))

    _wait_all(copies(N_EXPERTS - 2))
    _wait_all(copies(N_EXPERTS - 1))


def _dispatch(cnt, dst, hn, slot_row):
    tm = FFN_TILE
    n_rows = MAX_TILES * EXP_TILE
    grid_spec = pltpu.PrefetchScalarGridSpec(
        num_scalar_prefetch=2,
        grid=(N_UNITS,),
        in_specs=[
            pl.BlockSpec((tm, D_MODEL), lambda t, c, d: (t, 0)),
            pl.BlockSpec((N_EXPERTS, tm), lambda t, c, d: (0, t)),
            pl.BlockSpec(memory_space=pl.ANY),
        ],
        out_specs=pl.BlockSpec(memory_space=pl.ANY),
        scratch_shapes=[pltpu.VMEM((2, tm, D_MODEL), BF16), pltpu.SemaphoreType.DMA((2,))],
    )
    return pl.pallas_call(
        _dispatch_kernel,
        grid_spec=grid_spec,
        out_shape=jax.ShapeDtypeStruct((n_rows, D_MODEL), BF16),
        input_output_aliases={4: 0},
        compiler_params=pltpu.CompilerParams(
            dimension_semantics=("arbitrary",), vmem_limit_bytes=VMEM_LIMIT),
        name="moe_dispatch",
    )(cnt, dst, hn, slot_row, jnp.zeros((n_rows, D_MODEL), BF16))


def _experts_kernel(texp_ref, ntile_ref, xs_ref, wg_ref, wu_ref, wd_ref, ys_ref, acc_ref):
    del texp_ref
    i, c = pl.program_id(0), pl.program_id(1)
    nc = pl.num_programs(1)

    @pl.when(i < ntile_ref[0])
    def _():
        @pl.when(c == 0)
        def _():
            acc_ref[...] = jnp.zeros_like(acc_ref)

        wg = wg_ref[...].astype(BF16)
        wu = wu_ref[...].astype(BF16)
        wd = wd_ref[...].astype(BF16)
        for r in range(0, EXP_TILE, SUB_ROWS):
            xb = xs_ref[r:r + SUB_ROWS, :]
            g = _dot(xb, wg)
            u = _dot(xb, wu)
            acc_ref[r:r + SUB_ROWS, :] += _dot((g * _sigmoid(g) * u).astype(BF16), wd)

        @pl.when(c == nc - 1)
        def _():
            ys_ref[...] = acc_ref[...].astype(BF16)


def _experts(tile_expert, n_tiles, xs, wg, wu, wd):
    tm = EXP_TILE
    nc = D_EXPERT // EXP_CHUNK

    def row_map(i, c, te, nt):
        return (jnp.minimum(i, nt[0] - 1), 0)

    def chunk(i, c, nt):
        return jnp.where(i < nt[0], c, nc - 1)

    grid_spec = pltpu.PrefetchScalarGridSpec(
        num_scalar_prefetch=2,
        grid=(MAX_TILES, nc),
        in_specs=[
            pl.BlockSpec((tm, D_MODEL), row_map),
            pl.BlockSpec((None, None, D_MODEL, EXP_CHUNK), lambda i, c, te, nt: (0, te[i], 0, chunk(i, c, nt))),
            pl.BlockSpec((None, None, D_MODEL, EXP_CHUNK), lambda i, c, te, nt: (0, te[i], 0, chunk(i, c, nt))),
            pl.BlockSpec((None, None, EXP_CHUNK, D_MODEL), lambda i, c, te, nt: (0, te[i], chunk(i, c, nt), 0)),
        ],
        out_specs=pl.BlockSpec((tm, D_MODEL), row_map),
        scratch_shapes=[pltpu.VMEM((tm, D_MODEL), F32)],
    )
    return pl.pallas_call(
        _experts_kernel,
        grid_spec=grid_spec,
        out_shape=jax.ShapeDtypeStruct((MAX_TILES * tm, D_MODEL), BF16),
        compiler_params=pltpu.CompilerParams(
            dimension_semantics=("arbitrary", "arbitrary"), vmem_limit_bytes=VMEM_LIMIT),
        name="moe_experts",
    )(tile_expert, n_tiles, xs, wg, wu, wd)


def _combine_kernel(cnt_ref, dst_ref, ys_ref, col_ref, h_ref, nf_ref, op_ref, os_ref, stage, sems, acc_ref):
    tu, rb = FFN_TILE, 2 * ROW_BLOCK
    t = pl.program_id(0)

    @pl.when(t == 0)
    def _():
        stage[...] = jnp.zeros_like(stage)

    def copies(e):
        slot = e % 2
        n_rows = _ceil_blocks(cnt_ref[t * N_EXPERTS + e], ROW_ALIGN) * ROW_ALIGN
        d0 = dst_ref[t * N_EXPERTS + e]
        return _segment_copies(
            n_rows,
            lambda off, size: ys_ref.at[pl.ds(pl.multiple_of(d0 + off, ROW_ALIGN), size)],
            lambda off, size: stage.at[slot, pl.ds(off, size)],
            sems.at[slot])

    _start_all(copies(0))
    acc_ref[...] = h_ref[...]
    for e in range(N_EXPERTS):
        slot = e % 2
        if e + 1 < N_EXPERTS:
            _start_all(copies(e + 1))
        _wait_all(copies(e))
        slot_col = col_ref[:, e:e + 1]
        gate_col = col_ref[:, e + N_EXPERTS:e + N_EXPERTS + 1]

        def scatter(b, carry, slot=slot, slot_col=slot_col, gate_col=gate_col):
            r0 = pl.multiple_of(b * rb, rb)
            cid = (lax.broadcasted_iota(jnp.int32, (tu, rb), 1) + b * rb).astype(F32)
            weighted = jnp.where(slot_col == cid, gate_col, 0.0).astype(BF16)
            acc_ref[...] += _dot(weighted, stage[slot, pl.ds(r0, rb), :])
            return carry

        lax.fori_loop(0, _ceil_blocks(cnt_ref[t * N_EXPERTS + e], rb), scatter, 0)

    out = _rms(acc_ref[...], nf_ref[...])

    @pl.when(t < N_PROMPT // FFN_TILE)
    def _():
        op_ref[...] = out

    @pl.when(t >= N_PROMPT // FFN_TILE)
    def _():
        os_ref[...] = out


def _combine(cnt, dst, ys, cols, h, nf):
    tm = FFN_TILE
    np_tiles = N_PROMPT // tm
    assert N_SAMPLE == tm
    grid_spec = pltpu.PrefetchScalarGridSpec(
        num_scalar_prefetch=2,
        grid=(N_UNITS,),
        in_specs=[
            pl.BlockSpec(memory_space=pl.ANY),
            pl.BlockSpec((tm, 2 * N_EXPERTS), lambda t, c, d: (t, 0)),
            pl.BlockSpec((tm, D_MODEL), lambda t, c, d: (t, 0)),
            pl.BlockSpec((1, D_MODEL), lambda t, c, d: (0, 0)),
        ],
        out_specs=[
            pl.BlockSpec((tm, D_MODEL), lambda t, c, d: (jnp.minimum(t, np_tiles - 1), 0)),
            pl.BlockSpec((tm, D_MODEL), lambda t, c, d: (0, 0)),
        ],
        scratch_shapes=[pltpu.VMEM((2, tm, D_MODEL), BF16), pltpu.SemaphoreType.DMA((2,)),
                        pltpu.VMEM((tm, D_MODEL), F32)],
    )
    return pl.pallas_call(
        _combine_kernel,
        grid_spec=grid_spec,
        out_shape=[
            jax.ShapeDtypeStruct((N_PROMPT, D_MODEL), F32),
            jax.ShapeDtypeStruct((N_SAMPLE, D_MODEL), F32),
        ],
        compiler_params=pltpu.CompilerParams(
            dimension_semantics=("arbitrary",), vmem_limit_bytes=VMEM_LIMIT),
        name="moe_combine",
    )(cnt, dst, ys, cols, h, nf)


def kernel(x_prompt, x_sample, cache_win_k, cache_win_v, state_conv, state_pool, norm_mix, w_in, attn_sinks,
           w_attn_out, conv_w, w_conv_out, w_pool, pool_scale, w_out, norm_ffn, ffn_w_gate, ffn_w_up, ffn_w_down,
           moe_router, moe_w_gate, moe_w_up, moe_w_down, norm_final):
    mix_weights = (norm_mix.reshape(DEPTH, 1, D_MODEL), w_in.astype(BF16), w_attn_out.astype(BF16), conv_w,
                   w_conv_out.astype(BF16), w_pool.astype(BF16), pool_scale.reshape(DEPTH, 1, D_MODEL),
                   w_out.astype(BF16))
    kc = cache_win_k.reshape(DEPTH, DEC_BATCH, WINDOW, D_KV)
    vc = cache_win_v.reshape(DEPTH, DEC_BATCH, WINDOW, D_KV)

    xp = x_prompt.reshape(N_PROMPT, D_MODEL)
    xs, xs_row0 = x_sample.reshape(N_SAMPLE, D_MODEL), 0
    p_states, s_states = (), ()
    for l in range(DEPTH):
        sinks = attn_sinks[l]
        h, p_states = _prompt_mix(l, xp, sinks, mix_weights, p_states)
        h, s_states = _sample_mix(l, xs, xs_row0, sinks, kc, vc, state_conv, state_pool, mix_weights, h, s_states)
        i = l // 2
        nw = norm_ffn[l].reshape(1, D_MODEL)
        if l % 2 == 0:
            xp = _ffn_dense(h, nw, ffn_w_gate[i:i + 1].astype(BF16), ffn_w_up[i:i + 1].astype(BF16),
                            ffn_w_down[i:i + 1].astype(BF16))
            xs, xs_row0 = xp, N_PROMPT
        else:
            hn, slot_row, cols, cnt = _router(h, nw, moe_router[i].T)
            cnt = cnt[:, :, 0].astype(jnp.int32)
            dst, tile_expert, n_tiles = _plan(cnt)
            cnt = cnt.reshape(-1)
            xs_sorted = _dispatch(cnt, dst, hn, slot_row)
            ys_sorted = _experts(tile_expert, n_tiles, xs_sorted, moe_w_gate[i:i + 1], moe_w_up[i:i + 1],
                                 moe_w_down[i:i + 1])
            y_prompt, y_sample = _combine(cnt, dst, ys_sorted, cols, h, norm_final.reshape(1, D_MODEL))

    pk, pv, pc, pp = p_states
    sk, sv, sc, sp = s_states
    kv_shape_p = (DEPTH, BATCH, WINDOW, N_KV_HEADS, HEAD_DIM)
    kv_shape_s = (DEPTH, DEC_BATCH, WINDOW, N_KV_HEADS, HEAD_DIM)
    return (y_prompt.reshape(BATCH, SEQ, D_MODEL), y_sample.reshape(DEC_BATCH, DEC_SEQ, D_MODEL),
            pk.reshape(kv_shape_p), pv.reshape(kv_shape_p), pc, pp,
            sk.reshape(kv_shape_s), sv.reshape(kv_shape_s), sc, sp)
```

```python
import functools
import math

import jax
import jax.numpy as jnp
from jax import lax
from jax.experimental import pallas as pl
from jax.experimental.pallas import tpu as pltpu

F32 = jnp.float32
BF16 = jnp.bfloat16

D_MODEL = 1024
BATCH = 4
SEQ = 4096
DEPTH = 2
DEC_BATCH = 128
DEC_SEQ = 8
HEAD_DIM = 64
N_Q_HEADS = 8
N_KV_HEADS = 2
GQA_GROUP = N_Q_HEADS // N_KV_HEADS
D_ATTN = N_Q_HEADS * HEAD_DIM
D_KV = N_KV_HEADS * HEAD_DIM
WINDOW = 128
ATTN_SCALE = HEAD_DIM ** -0.5
D_CONV = D_MODEL // 2
CONV_WIDTH = 3
D_POOL = D_MODEL // 2
POOL_WINDOWS = (2, 4, 8, 16)
POOL_GROUP = D_POOL // len(POOL_WINDOWS)
POOL_OUT_GROUP = D_MODEL // len(POOL_WINDOWS)
POOL_BUF = max(POOL_WINDOWS) - 1
OFF_Q = 0
OFF_K = OFF_Q + D_ATTN
OFF_V = OFF_K + D_KV
OFF_CX = OFF_V + D_KV
OFF_CB = OFF_CX + D_CONV
OFF_CC = OFF_CB + D_CONV
OFF_P = OFF_CC + D_CONV
OFF_G = OFF_P + D_POOL
D_IN = OFF_G + 3 * D_MODEL
D_FF = 2816
N_EXPERTS = 8
D_EXPERT = 3584
EPS = 1e-5
LOG2E = math.log2(math.e)

N_PROMPT = BATCH * SEQ
N_SAMPLE = DEC_BATCH * DEC_SEQ
N_TOK = N_PROMPT + N_SAMPLE

PROMPT_TILE = 512
SAMPLE_SEQS = 32
SAMPLE_ROWS = SAMPLE_SEQS * DEC_SEQ
ATT_GROUP = 8
FFN_TILE = 1024
EXP_CHUNK = D_EXPERT // 4
SUB_ROWS = 256
ROW_BLOCK = 128
ROW_ALIGN = 16
EXP_TILE = 1024
N_UNITS = N_TOK // FFN_TILE
MAX_TILES = -(-(2 * N_TOK + N_UNITS * N_EXPERTS * (ROW_ALIGN - 1)) // EXP_TILE) + N_EXPERTS
N_PAD = N_TOK
SEG_BITS = 7
assert FFN_TILE == ROW_ALIGN << (SEG_BITS - 1)
VMEM_LIMIT = 58 * 1024 * 1024


def _dot(a, b):
    return jnp.dot(a, b, preferred_element_type=F32)


def _dot_nt(a, b):
    return lax.dot_general(a, b, (((1,), (1,)), ((), ())), preferred_element_type=F32)


def _rms(x, g):
    return x * lax.rsqrt(jnp.mean(x * x, axis=-1, keepdims=True) + EPS) * g


def _sigmoid(x):
    return 1.0 / (1.0 + jnp.exp(-x))


def _head_slope(h):
    return float(2.0 ** (-8.0 * (h + 1) / N_Q_HEADS))


def _softmax_pv(parts, sink):
    m = sink
    for s, _ in parts:
        m = jnp.maximum(m, jnp.max(s, axis=-1, keepdims=True))
    den = jnp.exp(sink - m)
    o = None
    for s, v in parts:
        p = jnp.exp(s - m)
        den = den + jnp.sum(p, axis=-1, keepdims=True)
        pv = _dot(p.astype(BF16), v)
        o = pv if o is None else o + pv
    return o / den


def _mix_tail(nb, x, att_bf, cx, cb, yc, d_groups, win_ref, wao_ref, wco_ref, wp_ref, ps_ref, wo_ref):
    att_o = _dot(att_bf, wao_ref[...])
    merged = _sigmoid(_dot(nb, win_ref[:, OFF_G:OFF_G + D_MODEL])) * att_o
    conv_o = _dot((cb * yc).astype(BF16), wco_ref[...])
    merged = merged + _sigmoid(_dot(nb, win_ref[:, OFF_G + D_MODEL:OFF_G + 2 * D_MODEL])) * conv_o
    pool_o = jnp.concatenate([_dot(d.astype(BF16), wp_ref[g]) for g, d in enumerate(d_groups)], axis=-1)
    pool_o = pool_o * ps_ref[...]
    merged = merged + _sigmoid(_dot(nb, win_ref[:, OFF_G + 2 * D_MODEL:OFF_G + 3 * D_MODEL])) * pool_o
    return x + _dot(merged.astype(BF16), wo_ref[...])


def _prompt_mix_kernel(*refs, n_alias):
    (sinks_ref, x_ref, nw_ref, win_ref, wao_ref, cw_ref, wco_ref, wp_ref, ps_ref, wo_ref) = refs[:10]
    (h_ref, kw_ref, vw_ref, cs_ref, pst_ref,
     qbuf, kbuf, vbuf, att_ref, ubuf, pbuf, bias_ref) = refs[10 + n_alias:]
    tm = PROMPT_TILE
    b = pl.program_id(0)
    t = pl.program_id(1)
    x = x_ref[...]
    nb = _rms(x, nw_ref[...]).astype(BF16)
    lane = lax.broadcasted_iota(jnp.int32, (tm, D_KV), 1)
    low = lane < HEAD_DIM

    @pl.when(jnp.logical_and(b == 0, t == 0))
    def _():
        qi = lax.broadcasted_iota(jnp.int32, (WINDOW, 2 * WINDOW), 0)
        si = lax.broadcasted_iota(jnp.int32, (WINDOW, 2 * WINDOW), 1)
        dist_i = qi + WINDOW - si
        dist = dist_i.astype(F32)
        band = jnp.logical_and(dist_i >= 0, dist_i <= WINDOW)
        for h in range(N_Q_HEADS):
            bias = jnp.where(band, (-_head_slope(h) * LOG2E) * dist, -jnp.inf)
            bias_ref[0, h] = bias
            bias_ref[1, h] = jnp.where(si >= WINDOW, bias, -jnp.inf)

    @pl.when(t == 0)
    def _():
        for i in range(4):
            kbuf[i, 0:WINDOW, :] = jnp.zeros((WINDOW, D_KV), BF16)
            vbuf[i, 0:WINDOW, :] = jnp.zeros((WINDOW, D_KV), BF16)
        ubuf[0:8, :] = jnp.zeros((8, D_CONV), F32)
        pbuf[0:16, :] = jnp.zeros((16, D_POOL), F32)

    @pl.when(t > 0)
    def _():
        for i in range(4):
            kbuf[i, 0:WINDOW, :] = kbuf[i, tm:tm + WINDOW, :]
            vbuf[i, 0:WINDOW, :] = vbuf[i, tm:tm + WINDOW, :]
        ubuf[0:8, :] = ubuf[tm:tm + 8, :]
        pbuf[0:16, :] = pbuf[tm:tm + 16, :]

    qkv = _dot(nb, win_ref[:, OFF_Q:OFF_CX])
    qbuf[...] = (qkv[:, OFF_Q:OFF_K] * (ATTN_SCALE * LOG2E)).astype(BF16)
    k = qkv[:, OFF_K:OFF_V]
    v = qkv[:, OFF_V:OFF_CX]
    kw_ref[...] = k[tm - WINDOW:, :]
    vw_ref[...] = v[tm - WINDOW:, :]
    k_sw = pltpu.roll(k, HEAD_DIM, 1)
    v_sw = pltpu.roll(v, HEAD_DIM, 1)
    k_var = (jnp.where(low, k, 0.0), jnp.where(low, 0.0, k_sw), jnp.where(low, k_sw, 0.0), jnp.where(low, 0.0, k))
    v_var = (jnp.where(low, v, 1.0), jnp.where(low, 1.0, v_sw), jnp.where(low, v_sw, 1.0), jnp.where(low, 1.0, v))
    for i in range(4):
        kbuf[i, WINDOW:WINDOW + tm, :] = k_var[i].astype(BF16)
        vbuf[i, WINDOW:WINDOW + tm, :] = v_var[i].astype(BF16)

    lane_q = lax.broadcasted_iota(jnp.int32, (WINDOW, 2 * HEAD_DIM), 1)
    low_q = lane_q < HEAD_DIM

    seq_start = jnp.where(t == 0, 1, 0)
    for j in range(tm // WINDOW):
        r0 = j * WINDOW
        first = seq_start if j == 0 else 0
        for pair in range(N_Q_HEADS // 2):
            qp = qbuf[r0:r0 + WINDOW, pair * 2 * HEAD_DIM:(pair + 1) * 2 * HEAD_DIM]
            halves = []
            for half in range(2):
                h = 2 * pair + half
                var = 2 * (h // GQA_GROUP) + half
                s = _dot_nt(qp, kbuf[var, r0:r0 + 2 * WINDOW, :]) + bias_ref[first, h]
                sink = sinks_ref[h] * LOG2E
                m = jnp.maximum(jnp.max(s, axis=-1, keepdims=True), sink)
                p = jnp.exp2(s - m).astype(BF16)
                o = _dot(p, vbuf[var, r0:r0 + 2 * WINDOW, :])
                den = (o[:, HEAD_DIM:HEAD_DIM + 1] if half == 0 else o[:, 0:1]) + jnp.exp2(sink - m)
                halves.append(o / den)
            att_ref[r0:r0 + WINDOW, pair * 2 * HEAD_DIM:(pair + 1) * 2 * HEAD_DIM] = (
                jnp.where(low_q, halves[0], halves[1]).astype(BF16))

    conv = _dot(nb, win_ref[:, OFF_CX:OFF_P])
    cx = conv[:, 0:D_CONV]
    cb = conv[:, D_CONV:2 * D_CONV]
    ubuf[8:8 + tm, :] = conv[:, 2 * D_CONV:3 * D_CONV] * cx
    yc = (cw_ref[0:1, :] * ubuf[6:6 + tm, :] + cw_ref[1:2, :] * ubuf[7:7 + tm, :]
          + cw_ref[2:3, :] * ubuf[8:8 + tm, :])
    cs_ref[...] = ubuf[tm + 6:tm + 8, :]

    pbuf[16:16 + tm, :] = _dot(nb, win_ref[:, OFF_P:OFF_G])
    pos = lax.broadcasted_iota(jnp.int32, (tm, 1), 0) + t * tm + 1
    d_groups = []
    for g, w in enumerate(POOL_WINDOWS):
        c0 = g * POOL_GROUP
        cur = pbuf[16:16 + tm, c0:c0 + POOL_GROUP]
        tot = cur
        for j in range(1, w):
            tot = tot + pbuf[16 - j:16 - j + tm, c0:c0 + POOL_GROUP]
        cnt = jnp.minimum(pos, w).astype(F32)
        d_groups.append(tot / cnt - cur)
    pst_ref[...] = pbuf[tm + 1:tm + 16, :]

    h_ref[...] = _mix_tail(nb, x, att_ref[...], cx, cb, yc, d_groups,
                           win_ref, wao_ref, wco_ref, wp_ref, ps_ref, wo_ref)


def _weight_specs(l, grid_rank):
    def const(*idx):
        if grid_rank == 1:
            return lambda i: idx
        return lambda b, t: idx

    return [
        pl.BlockSpec((None, 1, D_MODEL), const(l, 0, 0)),
        pl.BlockSpec((None, D_MODEL, D_IN), const(l, 0, 0)),
        pl.BlockSpec((None, D_ATTN, D_MODEL), const(l, 0, 0)),
        pl.BlockSpec((None, CONV_WIDTH, D_CONV), const(l, 0, 0)),
        pl.BlockSpec((None, D_CONV, D_MODEL), const(l, 0, 0)),
        pl.BlockSpec((None, len(POOL_WINDOWS), POOL_GROUP, POOL_OUT_GROUP), const(l, 0, 0, 0)),
        pl.BlockSpec((None, 1, D_MODEL), const(l, 0, 0)),
        pl.BlockSpec((None, D_MODEL, D_MODEL), const(l, 0, 0)),
    ]


def _prompt_mix(l, x, sinks, weights, prev_states):
    tm = PROMPT_TILE
    nt = SEQ // tm
    n_alias = len(prev_states)
    in_specs = ([pl.BlockSpec(memory_space=pltpu.SMEM),
                 pl.BlockSpec((tm, D_MODEL), lambda b, t: (b * nt + t, 0))] + _weight_specs(l, 2)
                + [pl.BlockSpec(memory_space=pl.ANY)] * n_alias)
    out_specs = [
        pl.BlockSpec((tm, D_MODEL), lambda b, t: (b * nt + t, 0)),
        pl.BlockSpec((None, None, WINDOW, D_KV), lambda b, t: (l, b, 0, 0)),
        pl.BlockSpec((None, None, WINDOW, D_KV), lambda b, t: (l, b, 0, 0)),
        pl.BlockSpec((None, None, CONV_WIDTH - 1, D_CONV), lambda b, t: (l, b, 0, 0)),
        pl.BlockSpec((None, None, POOL_BUF, D_POOL), lambda b, t: (l, b, 0, 0)),
    ]
    out_shape = [
        jax.ShapeDtypeStruct((N_PAD, D_MODEL), F32),
        jax.ShapeDtypeStruct((DEPTH, BATCH, WINDOW, D_KV), F32),
        jax.ShapeDtypeStruct((DEPTH, BATCH, WINDOW, D_KV), F32),
        jax.ShapeDtypeStruct((DEPTH, BATCH, CONV_WIDTH - 1, D_CONV), F32),
        jax.ShapeDtypeStruct((DEPTH, BATCH, POOL_BUF, D_POOL), F32),
    ]
    scratch = [
        pltpu.VMEM((tm, D_ATTN), BF16),
        pltpu.VMEM((4, WINDOW + tm, D_KV), BF16),
        pltpu.VMEM((4, WINDOW + tm, D_KV), BF16),
        pltpu.VMEM((tm, D_ATTN), BF16),
        pltpu.VMEM((8 + tm, D_CONV), F32),
        pltpu.VMEM((16 + tm, D_POOL), F32),
        pltpu.VMEM((2, N_Q_HEADS, WINDOW, 2 * WINDOW), F32),
    ]
    outs = pl.pallas_call(
        functools.partial(_prompt_mix_kernel, n_alias=n_alias),
        grid=(BATCH, nt),
        in_specs=in_specs,
        out_specs=out_specs,
        out_shape=out_shape,
        scratch_shapes=scratch,
        input_output_aliases={10 + i: 1 + i for i in range(n_alias)},
        compiler_params=pltpu.CompilerParams(
            dimension_semantics=("arbitrary", "arbitrary"), vmem_limit_bytes=VMEM_LIMIT),
        name=f"prompt_mix_l{l}",
    )(sinks, x, *weights, *prev_states)
    return outs[0], tuple(outs[1:])


def _sample_mix_kernel(*refs, n_alias):
    (sinks_ref, x_ref, kc_ref, vc_ref, cst_ref, pin_ref,
     nw_ref, win_ref, wao_ref, cw_ref, wco_ref, wp_ref, ps_ref, wo_ref) = refs[:14]
    (h_ref, ko_ref, vo_ref, co_ref, po_ref,
     qbuf, knb, vnb, att_ref, cbuf, ebuf, bias_c, bias_n) = refs[14 + n_alias:]
    ns, nr, T = SAMPLE_SEQS, SAMPLE_ROWS, DEC_SEQ
    x = x_ref[...]
    nb = _rms(x, nw_ref[...]).astype(BF16)

    qkv = _dot(nb, win_ref[:, OFF_Q:OFF_CX])
    qbuf[...] = (qkv[:, OFF_Q:OFF_K] * ATTN_SCALE).astype(BF16)
    k = qkv[:, OFF_K:OFF_V]
    v = qkv[:, OFF_V:OFF_CX]
    knb[...] = k.astype(BF16)
    vnb[...] = v.astype(BF16)
    ko_ref[:, 0:WINDOW - T, :] = kc_ref[:, T:WINDOW, :]
    vo_ref[:, 0:WINDOW - T, :] = vc_ref[:, T:WINDOW, :]
    ko_ref[:, WINDOW - T:WINDOW, :] = k.reshape(ns, T, D_KV)
    vo_ref[:, WINDOW - T:WINDOW, :] = v.reshape(ns, T, D_KV)

    gr = ATT_GROUP * T
    gc = ATT_GROUP * WINDOW

    @pl.when(pl.program_id(0) == 0)
    def _():
        rq = lax.broadcasted_iota(jnp.int32, (gr, gc), 0)
        cq = lax.broadcasted_iota(jnp.int32, (gr, gc), 1)
        tq = rq & (T - 1)
        sc_pos = cq & (WINDOW - 1)
        valid_c = jnp.logical_and((rq >> 3) == (cq >> 7), sc_pos >= tq)
        dist_c = (WINDOW + tq - sc_pos).astype(F32)
        rn = lax.broadcasted_iota(jnp.int32, (gr, gr), 0)
        cn = lax.broadcasted_iota(jnp.int32, (gr, gr), 1)
        tn = rn & (T - 1)
        jn = cn & (T - 1)
        valid_n = jnp.logical_and((rn >> 3) == (cn >> 3), jn <= tn)
        dist_n = (tn - jn).astype(F32)
        for h in range(N_Q_HEADS):
            bias_c[h] = jnp.where(valid_c, -_head_slope(h) * dist_c, -jnp.inf)
            bias_n[h] = jnp.where(valid_n, -_head_slope(h) * dist_n, -jnp.inf)

    for gi in range(ns // ATT_GROUP):
        r0 = gi * gr
        s0 = gi * ATT_GROUP
        qg = qbuf[r0:r0 + gr, :]
        kcg = kc_ref[s0:s0 + ATT_GROUP, :, :].reshape(gc, D_KV).astype(BF16)
        vcg = vc_ref[s0:s0 + ATT_GROUP, :, :].reshape(gc, D_KV).astype(BF16)
        kng = knb[r0:r0 + gr, :]
        vng = vnb[r0:r0 + gr, :]
        outs = []
        for h in range(N_Q_HEADS):
            kv = h // GQA_GROUP
            lo, hi = kv * HEAD_DIM, (kv + 1) * HEAD_DIM
            qh = qg[:, h * HEAD_DIM:(h + 1) * HEAD_DIM]
            s_c = _dot_nt(qh, kcg[:, lo:hi]) + bias_c[h]
            s_n = _dot_nt(qh, kng[:, lo:hi]) + bias_n[h]
            outs.append(_softmax_pv([(s_c, vcg[:, lo:hi]), (s_n, vng[:, lo:hi])], sinks_ref[h]))
        att_ref[r0:r0 + gr, :] = jnp.concatenate(outs, axis=-1).astype(BF16)

    conv = _dot(nb, win_ref[:, OFF_CX:OFF_P])
    cx = conv[:, 0:D_CONV]
    cb = conv[:, D_CONV:2 * D_CONV]
    u = conv[:, 2 * D_CONV:3 * D_CONV] * cx
    cbuf[:, 6:8, :] = cst_ref[...]
    cbuf[:, 8:16, :] = u.reshape(ns, T, D_CONV)
    w0 = cw_ref[0:1, :].reshape(1, 1, D_CONV)
    w1 = cw_ref[1:2, :].reshape(1, 1, D_CONV)
    w2 = cw_ref[2:3, :].reshape(1, 1, D_CONV)
    yc = (w0 * cbuf[:, 6:14, :] + w1 * cbuf[:, 7:15, :] + w2 * cbuf[:, 8:16, :]).reshape(nr, D_CONV)
    co_ref[...] = cbuf[:, 14:16, :]

    ebuf[:, 1:16, :] = pin_ref[...]
    ebuf[:, 16:24, :] = _dot(nb, win_ref[:, OFF_P:OFF_G]).reshape(ns, T, D_POOL)
    d_groups = []
    for g, w in enumerate(POOL_WINDOWS):
        c0 = g * POOL_GROUP
        cur = ebuf[:, 16:24, c0:c0 + POOL_GROUP]
        tot = cur
        for j in range(1, w):
            tot = tot + ebuf[:, 16 - j:24 - j, c0:c0 + POOL_GROUP]
        d_groups.append((tot / float(w) - cur).reshape(nr, POOL_GROUP))
    po_ref[...] = ebuf[:, 9:24, :]

    h_ref[...] = _mix_tail(nb, x, att_ref[...], cx, cb, yc, d_groups,
                           win_ref, wao_ref, wco_ref, wp_ref, ps_ref, wo_ref)


def _sample_mix(l, x, x_row0, sinks, kc, vc, cst, pst, weights, h_buf, prev_states):
    ns, nr = SAMPLE_SEQS, SAMPLE_ROWS
    n_alias = 1 + len(prev_states)
    xb0 = x_row0 // nr
    hb0 = N_PROMPT // nr
    in_specs = [
        pl.BlockSpec(memory_space=pltpu.SMEM),
        pl.BlockSpec((nr, D_MODEL), lambda i: (xb0 + i, 0)),
        pl.BlockSpec((None, ns, WINDOW, D_KV), lambda i: (l, i, 0, 0)),
        pl.BlockSpec((None, ns, WINDOW, D_KV), lambda i: (l, i, 0, 0)),
        pl.BlockSpec((None, ns, CONV_WIDTH - 1, D_CONV), lambda i: (l, i, 0, 0)),
        pl.BlockSpec((None, ns, POOL_BUF, D_POOL), lambda i: (l, i, 0, 0)),
    ] + _weight_specs(l, 1) + [pl.BlockSpec(memory_space=pl.ANY)] * n_alias
    out_specs = [
        pl.BlockSpec((nr, D_MODEL), lambda i: (hb0 + i, 0)),
        pl.BlockSpec((None, ns, WINDOW, D_KV), lambda i: (l, i, 0, 0)),
        pl.BlockSpec((None, ns, WINDOW, D_KV), lambda i: (l, i, 0, 0)),
        pl.BlockSpec((None, ns, CONV_WIDTH - 1, D_CONV), lambda i: (l, i, 0, 0)),
        pl.BlockSpec((None, ns, POOL_BUF, D_POOL), lambda i: (l, i, 0, 0)),
    ]
    out_shape = [
        jax.ShapeDtypeStruct((N_PAD, D_MODEL), F32),
        jax.ShapeDtypeStruct((DEPTH, DEC_BATCH, WINDOW, D_KV), F32),
        jax.ShapeDtypeStruct((DEPTH, DEC_BATCH, WINDOW, D_KV), F32),
        jax.ShapeDtypeStruct((DEPTH, DEC_BATCH, CONV_WIDTH - 1, D_CONV), F32),
        jax.ShapeDtypeStruct((DEPTH, DEC_BATCH, POOL_BUF, D_POOL), F32),
    ]
    scratch = [
        pltpu.VMEM((nr, D_ATTN), BF16),
        pltpu.VMEM((nr, D_KV), BF16),
        pltpu.VMEM((nr, D_KV), BF16),
        pltpu.VMEM((nr, D_ATTN), BF16),
        pltpu.VMEM((ns, 16, D_CONV), F32),
        pltpu.VMEM((ns, 24, D_POOL), F32),
        pltpu.VMEM((N_Q_HEADS, ATT_GROUP * DEC_SEQ, ATT_GROUP * WINDOW), F32),
        pltpu.VMEM((N_Q_HEADS, ATT_GROUP * DEC_SEQ, ATT_GROUP * DEC_SEQ), F32),
    ]
    outs = pl.pallas_call(
        functools.partial(_sample_mix_kernel, n_alias=n_alias),
        grid=(DEC_BATCH // ns,),
        in_specs=in_specs,
        out_specs=out_specs,
        out_shape=out_shape,
        scratch_shapes=scratch,
        input_output_aliases={14 + i: i for i in range(n_alias)},
        compiler_params=pltpu.CompilerParams(
            dimension_semantics=("arbitrary",), vmem_limit_bytes=VMEM_LIMIT),
        name=f"sample_mix_l{l}",
    )(sinks, x, kc, vc, cst, pst, *weights, h_buf, *prev_states)
    return outs[0], tuple(outs[1:])


def _ffn_kernel(h_ref, nw_ref, wg_ref, wu_ref, wd_ref, o_ref):
    for r in range(0, FFN_TILE, SUB_ROWS):
        h = h_ref[r:r + SUB_ROWS, :]
        hn = _rms(h, nw_ref[...]).astype(BF16)
        g = _dot(hn, wg_ref[...])
        u = _dot(hn, wu_ref[...])
        o_ref[r:r + SUB_ROWS, :] = h + _dot((g * _sigmoid(g) * u).astype(BF16), wd_ref[...])


def _ffn_dense(h, nw, wg, wu, wd):
    tm = FFN_TILE
    resident = dict(pipeline_mode=pl.Buffered(1))
    return pl.pallas_call(
        _ffn_kernel,
        grid=(N_TOK // tm,),
        in_specs=[
            pl.BlockSpec((tm, D_MODEL), lambda i: (i, 0)),
            pl.BlockSpec((1, D_MODEL), lambda i: (0, 0)),
            pl.BlockSpec((None, D_MODEL, D_FF), lambda i: (0, 0, 0), **resident),
            pl.BlockSpec((None, D_MODEL, D_FF), lambda i: (0, 0, 0), **resident),
            pl.BlockSpec((None, D_FF, D_MODEL), lambda i: (0, 0, 0), **resident),
        ],
        out_specs=pl.BlockSpec((tm, D_MODEL), lambda i: (i, 0)),
        out_shape=jax.ShapeDtypeStruct((N_PAD, D_MODEL), F32),
        compiler_params=pltpu.CompilerParams(
            dimension_semantics=("arbitrary",), vmem_limit_bytes=VMEM_LIMIT),
        name="ffn_dense",
    )(h, nw, wg, wu, wd)


def _router_kernel(h_ref, nw_ref, rt_ref, hn_ref, slot_ref, col_ref, cnt_ref):
    tm = FFN_TILE
    hn = _rms(h_ref[...], nw_ref[...])
    hn_ref[...] = hn.astype(BF16)
    logits = lax.dot_general(rt_ref[...], hn, (((1,), (1,)), ((), ())),
                             precision=lax.Precision.HIGHEST, preferred_element_type=F32)
    eidx = lax.broadcasted_iota(jnp.int32, (N_EXPERTS, tm), 0).astype(F32)
    none = float(N_EXPERTS)
    m1 = jnp.max(logits, axis=0, keepdims=True)
    i1 = jnp.min(jnp.where(logits == m1, eidx, none), axis=0, keepdims=True)
    rest = jnp.where(eidx == i1, -jnp.inf, logits)
    m2 = jnp.max(rest, axis=0, keepdims=True)
    i2 = jnp.min(jnp.where(rest == m2, eidx, none), axis=0, keepdims=True)
    e2 = jnp.exp(m2 - m1)
    w1 = 1.0 / (1.0 + e2)
    w2 = e2 / (1.0 + e2)
    sel1 = eidx == i1
    sel2 = eidx == i2
    gate = jnp.where(sel1, w1, jnp.where(sel2, w2, 0.0))
    chosen = jnp.logical_or(sel1, sel2)
    mask = jnp.where(chosen, 1.0, 0.0)
    srow = lax.broadcasted_iota(jnp.int32, (tm, tm), 0)
    scol = lax.broadcasted_iota(jnp.int32, (tm, tm), 1)
    upper = jnp.where(srow < scol, 1.0, 0.0).astype(BF16)
    mask16 = jnp.concatenate([mask, jnp.zeros_like(mask)], axis=0).astype(BF16)
    slot = jnp.where(chosen, _dot(mask16, upper)[0:N_EXPERTS, :], -1.0)
    slot_ref[...] = slot.astype(jnp.int32)
    cnt_ref[...] = jnp.broadcast_to(jnp.sum(mask, axis=1, keepdims=True), (N_EXPERTS, 128))
    both = jnp.concatenate([slot, gate, jnp.zeros((128 - 2 * N_EXPERTS, tm), F32)], axis=0)
    col_ref[...] = both.T[:, 0:2 * N_EXPERTS]


def _router(h, nw, router_t):
    tm = FFN_TILE
    nt = N_UNITS
    return pl.pallas_call(
        _router_kernel,
        grid=(nt,),
        in_specs=[
            pl.BlockSpec((tm, D_MODEL), lambda i: (i, 0)),
            pl.BlockSpec((1, D_MODEL), lambda i: (0, 0)),
            pl.BlockSpec((N_EXPERTS, D_MODEL), lambda i: (0, 0)),
        ],
        out_specs=[
            pl.BlockSpec((tm, D_MODEL), lambda i: (i, 0)),
            pl.BlockSpec((N_EXPERTS, tm), lambda i: (0, i)),
            pl.BlockSpec((tm, 2 * N_EXPERTS), lambda i: (i, 0)),
            pl.BlockSpec((None, N_EXPERTS, 128), lambda i: (i, 0, 0)),
        ],
        out_shape=[
            jax.ShapeDtypeStruct((N_PAD, D_MODEL), BF16),
            jax.ShapeDtypeStruct((N_EXPERTS, N_PAD), jnp.int32),
            jax.ShapeDtypeStruct((N_PAD, 2 * N_EXPERTS), F32),
            jax.ShapeDtypeStruct((nt, N_EXPERTS, 128), F32),
        ],
        compiler_params=pltpu.CompilerParams(
            dimension_semantics=("arbitrary",), vmem_limit_bytes=VMEM_LIMIT),
        name="moe_router",
    )(h, nw, router_t)


def _plan(cnt):
    seg = -(-cnt // ROW_ALIGN) * ROW_ALIGN
    tiles_e = -(-jnp.sum(seg, axis=0) // EXP_TILE)
    cum_tiles = jnp.cumsum(tiles_e)
    base_e = (cum_tiles - tiles_e) * EXP_TILE
    dst = base_e[None, :] + jnp.cumsum(seg, axis=0) - seg
    n_tiles = cum_tiles[-1]
    i = jnp.arange(MAX_TILES, dtype=jnp.int32)
    tile_expert = jnp.sum((cum_tiles[None, :] <= i[:, None]).astype(jnp.int32), axis=1)
    tile_expert = jnp.minimum(tile_expert, N_EXPERTS - 1)
    tile_expert = jnp.where(i < n_tiles, tile_expert, tile_expert[jnp.maximum(n_tiles - 1, 0)])
    return dst.reshape(-1).astype(jnp.int32), tile_expert, n_tiles.reshape(1).astype(jnp.int32)


def _ceil_blocks(n, block):
    return (n + block - 1) >> (block.bit_length() - 1)


def _segment_copies(n_rows, src_at, dst_at, sem):
    n = n_rows >> (ROW_ALIGN.bit_length() - 1)
    out = []
    for bit in range(SEG_BITS - 1, -1, -1):
        size = ROW_ALIGN << bit
        off = pl.multiple_of(((n >> (bit + 1)) << (bit + 1)) * ROW_ALIGN, ROW_ALIGN)
        out.append((((n >> bit) & 1) == 1, pltpu.make_async_copy(src_at(off, size), dst_at(off, size), sem)))
    return out


def _start_all(copies):
    for pred, cp in copies:
        pl.when(pred)(cp.start)


def _wait_all(copies):
    for pred, cp in copies:
        pl.when(pred)(cp.wait)


def _dispatch_kernel(cnt_ref, dst_ref, hn_ref, srow_ref, init_ref, xs_ref, stage, sems):
    del init_ref
    tu, rb = FFN_TILE, ROW_BLOCK
    t = pl.program_id(0)

    def copies(e):
        slot = e % 2
        n_rows = _ceil_blocks(cnt_ref[t * N_EXPERTS + e], ROW_ALIGN) * ROW_ALIGN
        d0 = dst_ref[t * N_EXPERTS + e]
        return _segment_copies(
            n_rows,
            lambda off, size: stage.at[slot, pl.ds(off, size)],
            lambda off, size: xs_ref.at[pl.ds(pl.multiple_of(d0 + off, ROW_ALIGN), size)],
            sems.at[slot])

    for e in range(N_EXPERTS):
        slot = e % 2
        if e >= 2:
            _wait_all(copies(e - 2))
        slot_row = srow_ref[e:e + 1, :]

        def gather(b, carry, slot=slot, slot_row=slot_row):
            r0 = pl.multiple_of(b * rb, rb)
            rid = lax.broadcasted_iota(jnp.int32, (rb, tu), 0) + b * rb
            onehot = jnp.where(slot_row == rid, 1.0, 0.0).astype(BF16)
            stage[slot, pl.ds(r0, rb), :] = _dot(onehot, hn_ref[...]).astype(BF16)
            return carry

        lax.fori_loop(0, _ceil_blocks(cnt_ref[t * N_EXPERTS + e], rb), gather, 0)
        _start_all(copies(e))

    _wait_all(copies(N_EXPERTS - 2))
    _wait_all(copies(N_EXPERTS - 1))


def _dispatch(cnt, dst, hn, slot_row):
    tm = FFN_TILE
    n_rows = MAX_TILES * EXP_TILE
    grid_spec = pltpu.PrefetchScalarGridSpec(
        num_scalar_prefetch=2,
        grid=(N_UNITS,),
        in_specs=[
            pl.BlockSpec((tm, D_MODEL), lambda t, c, d: (t, 0)),
            pl.BlockSpec((N_EXPERTS, tm), lambda t, c, d: (0, t)),
            pl.BlockSpec(memory_space=pl.ANY),
        ],
        out_specs=pl.BlockSpec(memory_space=pl.ANY),
        scratch_shapes=[pltpu.VMEM((2, tm, D_MODEL), BF16), pltpu.SemaphoreType.DMA((2,))],
    )
    return pl.pallas_call(
        _dispatch_kernel,
        grid_spec=grid_spec,
        out_shape=jax.ShapeDtypeStruct((n_rows, D_MODEL), BF16),
        input_output_aliases={4: 0},
        compiler_params=pltpu.CompilerParams(
            dimension_semantics=("arbitrary",), vmem_limit_bytes=VMEM_LIMIT),
        name="moe_dispatch",
    )(cnt, dst, hn, slot_row, jnp.zeros((n_rows, D_MODEL), BF16))


def _experts_kernel(texp_ref, ntile_ref, xs_ref, wg_ref, wu_ref, wd_ref, ys_ref, acc_ref):
    del texp_ref
    i, c = pl.program_id(0), pl.program_id(1)
    nc = pl.num_programs(1)

    @pl.when(i < ntile_ref[0])
    def _():
        @pl.when(c == 0)
        def _():
            acc_ref[...] = jnp.zeros_like(acc_ref)

        wg = wg_ref[...].astype(BF16)
        wu = wu_ref[...].astype(BF16)
        wd = wd_ref[...].astype(BF16)
        for r in range(0, EXP_TILE, SUB_ROWS):
            xb = xs_ref[r:r + SUB_ROWS, :]
            g = _dot(xb, wg)
            u = _dot(xb, wu)
            acc_ref[r:r + SUB_ROWS, :] += _dot((g * _sigmoid(g) * u).astype(BF16), wd)

        @pl.when(c == nc - 1)
        def _():
            ys_ref[...] = acc_ref[...].astype(BF16)


def _experts(tile_expert, n_tiles, xs, wg, wu, wd):
    tm = EXP_TILE
    nc = D_EXPERT // EXP_CHUNK

    def row_map(i, c, te, nt):
        return (jnp.minimum(i, jnp.maximum(nt[0] - 1, 0)), 0)

    def chunk(i, c, nt):
        return jnp.where(i < nt[0], c, nc - 1)

    grid_spec = pltpu.PrefetchScalarGridSpec(
        num_scalar_prefetch=2,
        grid=(MAX_TILES, nc),
        in_specs=[
            pl.BlockSpec((tm, D_MODEL), row_map),
            pl.BlockSpec((None, None, D_MODEL, EXP_CHUNK), lambda i, c, te, nt: (0, te[i], 0, chunk(i, c, nt))),
            pl.BlockSpec((None, None, D_MODEL, EXP_CHUNK), lambda i, c, te, nt: (0, te[i], 0, chunk(i, c, nt))),
            pl.BlockSpec((None, None, EXP_CHUNK, D_MODEL), lambda i, c, te, nt: (0, te[i], chunk(i, c, nt), 0)),
        ],
        out_specs=pl.BlockSpec((tm, D_MODEL), row_map),
        scratch_shapes=[pltpu.VMEM((tm, D_MODEL), F32)],
    )
    return pl.pallas_call(
        _experts_kernel,
        grid_spec=grid_spec,
        out_shape=jax.ShapeDtypeStruct((MAX_TILES * tm, D_MODEL), BF16),
        compiler_params=pltpu.CompilerParams(
            dimension_semantics=("arbitrary", "arbitrary"), vmem_limit_bytes=VMEM_LIMIT),
        name="moe_experts",
    )(tile_expert, n_tiles, xs, wg, wu, wd)


def _combine_kernel(cnt_ref, dst_ref, ys_ref, col_ref, h_ref, nf_ref, op_ref, os_ref, stage, sems, acc_ref):
    tu, rb = FFN_TILE, 2 * ROW_BLOCK
    t = pl.program_id(0)

    @pl.when(t == 0)
    def _():
        stage[...] = jnp.zeros_like(stage)

    def copies(e):
        slot = e % 2
        n_rows = _ceil_blocks(cnt_ref[t * N_EXPERTS + e], ROW_ALIGN) * ROW_ALIGN
        d0 = dst_ref[t * N_EXPERTS + e]
        return _segment_copies(
            n_rows,
            lambda off, size: ys_ref.at[pl.ds(pl.multiple_of(d0 + off, ROW_ALIGN), size)],
            lambda off, size: stage.at[slot, pl.ds(off, size)],
            sems.at[slot])

    _start_all(copies(0))
    acc_ref[...] = h_ref[...]
    for e in range(N_EXPERTS):
        slot = e % 2
        if e + 1 < N_EXPERTS:
            _start_all(copies(e + 1))
        _wait_all(copies(e))
        slot_col = col_ref[:, e:e + 1]
        gate_col = col_ref[:, e + N_EXPERTS:e + N_EXPERTS + 1]

        def scatter(b, carry, slot=slot, slot_col=slot_col, gate_col=gate_col):
            r0 = pl.multiple_of(b * rb, rb)
            cid = (lax.broadcasted_iota(jnp.int32, (tu, rb), 1) + b * rb).astype(F32)
            weighted = jnp.where(slot_col == cid, gate_col, 0.0).astype(BF16)
            acc_ref[...] += _dot(weighted, stage[slot, pl.ds(r0, rb), :])
            return carry

        lax.fori_loop(0, _ceil_blocks(cnt_ref[t * N_EXPERTS + e], rb), scatter, 0)

    out = _rms(acc_ref[...], nf_ref[...])

    @pl.when(t < N_PROMPT // FFN_TILE)
    def _():
        op_ref[...] = out

    @pl.when(t >= N_PROMPT // FFN_TILE)
    def _():
        os_ref[...] = out


def _combine(cnt, dst, ys, cols, h, nf):
    tm = FFN_TILE
    np_tiles = N_PROMPT // tm
    assert N_SAMPLE == tm
    grid_spec = pltpu.PrefetchScalarGridSpec(
        num_scalar_prefetch=2,
        grid=(N_UNITS,),
        in_specs=[
            pl.BlockSpec(memory_space=pl.ANY),
            pl.BlockSpec((tm, 2 * N_EXPERTS), lambda t, c, d: (t, 0)),
            pl.BlockSpec((tm, D_MODEL), lambda t, c, d: (t, 0)),
            pl.BlockSpec((1, D_MODEL), lambda t, c, d: (0, 0)),
        ],
        out_specs=[
            pl.BlockSpec((tm, D_MODEL), lambda t, c, d: (jnp.minimum(t, np_tiles - 1), 0)),
            pl.BlockSpec((tm, D_MODEL), lambda t, c, d: (0, 0)),
        ],
        scratch_shapes=[pltpu.VMEM((2, tm, D_MODEL), BF16), pltpu.SemaphoreType.DMA((2,)),
                        pltpu.VMEM((tm, D_MODEL), F32)],
    )
    return pl.pallas_call(
        _combine_kernel,
        grid_spec=grid_spec,
        out_shape=[
            jax.ShapeDtypeStruct((N_PROMPT, D_MODEL), F32),
            jax.ShapeDtypeStruct((N_SAMPLE, D_MODEL), F32),
        ],
        compiler_params=pltpu.CompilerParams(
            dimension_semantics=("arbitrary",), vmem_limit_bytes=VMEM_LIMIT),
        name="moe_combine",
    )(cnt, dst, ys, cols, h, nf)


def kernel(x_prompt, x_sample, cache_win_k, cache_win_v, state_conv, state_pool, norm_mix, w_in, attn_sinks,
           w_attn_out, conv_w, w_conv_out, w_pool, pool_scale, w_out, norm_ffn, ffn_w_gate, ffn_w_up, ffn_w_down,
           moe_router, moe_w_gate, moe_w_up, moe_w_down, norm_final):
    mix_weights = (norm_mix.reshape(DEPTH, 1, D_MODEL), w_in.astype(BF16), w_attn_out.astype(BF16), conv_w,
                   w_conv_out.astype(BF16), w_pool.astype(BF16), pool_scale.reshape(DEPTH, 1, D_MODEL),
                   w_out.astype(BF16))
    kc = cache_win_k.reshape(DEPTH, DEC_BATCH, WINDOW, D_KV)
    vc = cache_win_v.reshape(DEPTH, DEC_BATCH, WINDOW, D_KV)

    xp = x_prompt.reshape(N_PROMPT, D_MODEL)
    xs, xs_row0 = x_sample.reshape(N_SAMPLE, D_MODEL), 0
    p_states, s_states = (), ()
    for l in range(DEPTH):
        sinks = attn_sinks[l]
        h, p_states = _prompt_mix(l, xp, sinks, mix_weights, p_states)
        h, s_states = _sample_mix(l, xs, xs_row0, sinks, kc, vc, state_conv, state_pool, mix_weights, h, s_states)
        i = l // 2
        nw = norm_ffn[l].reshape(1, D_MODEL)
        if l % 2 == 0:
            xp = _ffn_dense(h, nw, ffn_w_gate[i:i + 1].astype(BF16), ffn_w_up[i:i + 1].astype(BF16),
                            ffn_w_down[i:i + 1].astype(BF16))
            xs, xs_row0 = xp, N_PROMPT
        else:
            hn, slot_row, cols, cnt = _router(h, nw, moe_router[i].T)
            cnt = cnt[:, :, 0].astype(jnp.int32)
            dst, tile_expert, n_tiles = _plan(cnt)
            cnt = cnt.reshape(-1)
            xs_sorted = _dispatch(cnt, dst, hn, slot_row)
            ys_sorted = _experts(tile_expert, n_tiles, xs_sorted, moe_w_gate[i:i + 1], moe_w_up[i:i + 1],
                                 moe_w_down[i:i + 1])
            y_prompt, y_sample = _combine(cnt, dst, ys_sorted, cols, h, norm_final.reshape(1, D_MODEL))

    pk, pv, pc, pp = p_states
    sk, sv, sc, sp = s_states
    kv_shape_p = (DEPTH, BATCH, WINDOW, N_KV_HEADS, HEAD_DIM)
    kv_shape_s = (DEPTH, DEC_BATCH, WINDOW, N_KV_HEADS, HEAD_DIM)
    return (y_prompt.reshape(BATCH, SEQ, D_MODEL), y_sample.reshape(DEC_BATCH, DEC_SEQ, D_MODEL),
            pk.reshape(kv_shape_p), pv.reshape(kv_shape_p), pc, pp,
            sk.reshape(kv_shape_s), sv.reshape(kv_shape_s), sc, sp)
```

```python
import functools
import math

import jax
import jax.numpy as jnp
from jax import lax
from jax.experimental import pallas as pl
from jax.experimental.pallas import tpu as pltpu

F32 = jnp.float32
BF16 = jnp.bfloat16

D_MODEL = 1024
BATCH = 4
SEQ = 4096
DEPTH = 2
DEC_BATCH = 128
DEC_SEQ = 8
HEAD_DIM = 64
N_Q_HEADS = 8
N_KV_HEADS = 2
GQA_GROUP = N_Q_HEADS // N_KV_HEADS
D_ATTN = N_Q_HEADS * HEAD_DIM
D_KV = N_KV_HEADS * HEAD_DIM
WINDOW = 128
ATTN_SCALE = HEAD_DIM ** -0.5
D_CONV = D_MODEL // 2
CONV_WIDTH = 3
D_POOL = D_MODEL // 2
POOL_WINDOWS = (2, 4, 8, 16)
POOL_GROUP = D_POOL // len(POOL_WINDOWS)
POOL_OUT_GROUP = D_MODEL // len(POOL_WINDOWS)
POOL_BUF = max(POOL_WINDOWS) - 1
OFF_Q = 0
OFF_K = OFF_Q + D_ATTN
OFF_V = OFF_K + D_KV
OFF_CX = OFF_V + D_KV
OFF_CB = OFF_CX + D_CONV
OFF_CC = OFF_CB + D_CONV
OFF_P = OFF_CC + D_CONV
OFF_G = OFF_P + D_POOL
D_IN = OFF_G + 3 * D_MODEL
D_FF = 2816
N_EXPERTS = 8
D_EXPERT = 3584
EPS = 1e-5
LOG2E = math.log2(math.e)

N_PROMPT = BATCH * SEQ
N_SAMPLE = DEC_BATCH * DEC_SEQ
N_TOK = N_PROMPT + N_SAMPLE

PROMPT_TILE = 512
SAMPLE_SEQS = 32
SAMPLE_ROWS = SAMPLE_SEQS * DEC_SEQ
ATT_GROUP = 8
FFN_TILE = 1024
EXP_CHUNK = D_EXPERT // 2
SUB_ROWS = 256
CAST_STEPS = 64
ROW_BLOCK = 128
ROW_ALIGN = 16
EXP_TILE = 512
N_UNITS = N_TOK // FFN_TILE
MAX_TILES = -(-(2 * N_TOK + N_UNITS * N_EXPERTS * (ROW_ALIGN - 1)) // EXP_TILE) + N_EXPERTS
N_PAD = N_TOK
SEG_BITS = 7
assert FFN_TILE == ROW_ALIGN << (SEG_BITS - 1)
VMEM_LIMIT = 58 * 1024 * 1024


def _dot(a, b):
    return jnp.dot(a, b, preferred_element_type=F32)


def _dot_nt(a, b):
    return lax.dot_general(a, b, (((1,), (1,)), ((), ())), preferred_element_type=F32)


def _rms(x, g):
    return x * lax.rsqrt(jnp.mean(x * x, axis=-1, keepdims=True) + EPS) * g


def _sigmoid(x):
    return 0.5 * jnp.tanh(0.5 * x) + 0.5


def _head_slope(h):
    return float(2.0 ** (-8.0 * (h + 1) / N_Q_HEADS))


def _softmax_pv(parts, sink):
    m = sink
    for s, _ in parts:
        m = jnp.maximum(m, jnp.max(s, axis=-1, keepdims=True))
    den = jnp.exp(sink - m)
    o = None
    for s, v in parts:
        p = jnp.exp(s - m)
        den = den + jnp.sum(p, axis=-1, keepdims=True)
        pv = _dot(p.astype(BF16), v)
        o = pv if o is None else o + pv
    return o / den


def _z_part(nb, win_ref, i):
    if i == 0:
        return (_dot(nb, win_ref[:, OFF_CX:OFF_P]),)
    if i == 1:
        return (_dot(nb, win_ref[:, OFF_P:OFF_G]), _dot(nb, win_ref[:, OFF_G:OFF_G + D_MODEL]))
    return (_dot(nb, win_ref[:, OFF_G + (i - 1) * D_MODEL:OFF_G + i * D_MODEL]),)


def _mix_tail(x, att_bf, cb, yc, d_groups, gate_logits, wao_ref, wco_ref, wp_ref, ps_ref, wo_ref):
    att_o = _dot(att_bf, wao_ref[...])
    merged = _sigmoid(gate_logits[0]) * att_o
    conv_o = _dot((cb * yc).astype(BF16), wco_ref[...])
    merged = merged + _sigmoid(gate_logits[1]) * conv_o
    pool_o = jnp.concatenate([_dot(d.astype(BF16), wp_ref[g]) for g, d in enumerate(d_groups)], axis=-1)
    pool_o = pool_o * ps_ref[...]
    merged = merged + _sigmoid(gate_logits[2]) * pool_o
    return x + _dot(merged.astype(BF16), wo_ref[...])


def _prompt_mix_kernel(*refs, n_alias):
    (sinks_ref, x_ref, nw_ref, win_ref, wao_ref, cw_ref, wco_ref, wp_ref, ps_ref, wo_ref) = refs[:10]
    (h_ref, kw_ref, vw_ref, cs_ref, pst_ref,
     qbuf, kbuf, vbuf, att_ref, ubuf, pbuf, bias_ref) = refs[10 + n_alias:]
    tm = PROMPT_TILE
    b = pl.program_id(0)
    t = pl.program_id(1)
    x = x_ref[...]
    nb = _rms(x, nw_ref[...]).astype(BF16)
    lane = lax.broadcasted_iota(jnp.int32, (tm, D_KV), 1)
    low = lane < HEAD_DIM

    @pl.when(jnp.logical_and(b == 0, t == 0))
    def _():
        qi = lax.broadcasted_iota(jnp.int32, (WINDOW, 2 * WINDOW), 0)
        si = lax.broadcasted_iota(jnp.int32, (WINDOW, 2 * WINDOW), 1)
        dist_i = qi + WINDOW - si
        dist = dist_i.astype(F32)
        band = jnp.logical_and(dist_i >= 0, dist_i <= WINDOW)
        for h in range(N_Q_HEADS):
            bias = jnp.where(band, (-_head_slope(h) * LOG2E) * dist, -jnp.inf)
            bias_ref[0, h] = bias
            bias_ref[1, h] = jnp.where(si >= WINDOW, bias, -jnp.inf)

    @pl.when(t == 0)
    def _():
        for i in range(4):
            kbuf[i, 0:WINDOW, :] = jnp.zeros((WINDOW, D_KV), BF16)
            vbuf[i, 0:WINDOW, :] = jnp.zeros((WINDOW, D_KV), BF16)
        ubuf[0:8, :] = jnp.zeros((8, D_CONV), F32)
        pbuf[0:16, :] = jnp.zeros((16, D_POOL), F32)

    @pl.when(t > 0)
    def _():
        for i in range(4):
            kbuf[i, 0:WINDOW, :] = kbuf[i, tm:tm + WINDOW, :]
            vbuf[i, 0:WINDOW, :] = vbuf[i, tm:tm + WINDOW, :]
        ubuf[0:8, :] = ubuf[tm:tm + 8, :]
        pbuf[0:16, :] = pbuf[tm:tm + 16, :]

    qkv = _dot(nb, win_ref[:, OFF_Q:OFF_CX])
    qbuf[...] = (qkv[:, OFF_Q:OFF_K] * (ATTN_SCALE * LOG2E)).astype(BF16)
    k = qkv[:, OFF_K:OFF_V]
    v = qkv[:, OFF_V:OFF_CX]
    kw_ref[...] = k[tm - WINDOW:, :]
    vw_ref[...] = v[tm - WINDOW:, :]
    k_sw = pltpu.roll(k, HEAD_DIM, 1)
    v_sw = pltpu.roll(v, HEAD_DIM, 1)
    k_var = (jnp.where(low, k, 0.0), jnp.where(low, 0.0, k_sw), jnp.where(low, k_sw, 0.0), jnp.where(low, 0.0, k))
    v_var = (jnp.where(low, v, 1.0), jnp.where(low, 1.0, v_sw), jnp.where(low, v_sw, 1.0), jnp.where(low, 1.0, v))
    for i in range(4):
        kbuf[i, WINDOW:WINDOW + tm, :] = k_var[i].astype(BF16)
        vbuf[i, WINDOW:WINDOW + tm, :] = v_var[i].astype(BF16)

    lane_q = lax.broadcasted_iota(jnp.int32, (WINDOW, 2 * HEAD_DIM), 1)
    low_q = lane_q < HEAD_DIM

    seq_start = jnp.where(t == 0, 1, 0)
    z_parts = []
    for j in range(tm // WINDOW):
        z_parts.append(_z_part(nb, win_ref, j))
        r0 = j * WINDOW
        first = seq_start if j == 0 else 0
        for pair in range(N_Q_HEADS // 2):
            qp = qbuf[r0:r0 + WINDOW, pair * 2 * HEAD_DIM:(pair + 1) * 2 * HEAD_DIM]
            halves = []
            for half in range(2):
                h = 2 * pair + half
                var = 2 * (h // GQA_GROUP) + half
                s = _dot_nt(qp, kbuf[var, r0:r0 + 2 * WINDOW, :]) + bias_ref[first, h]
                sink = sinks_ref[h] * LOG2E
                m = jnp.maximum(jnp.max(s, axis=-1, keepdims=True), sink)
                p = jnp.exp2(s - m).astype(BF16)
                o = _dot(p, vbuf[var, r0:r0 + 2 * WINDOW, :])
                den = (o[:, HEAD_DIM:HEAD_DIM + 1] if half == 0 else o[:, 0:1]) + jnp.exp2(sink - m)
                halves.append(o / den)
            att_ref[r0:r0 + WINDOW, pair * 2 * HEAD_DIM:(pair + 1) * 2 * HEAD_DIM] = (
                jnp.where(low_q, halves[0], halves[1]).astype(BF16))

    (conv,), (pool_in, gate_att), (gate_conv,), (gate_pool,) = z_parts
    cx = conv[:, 0:D_CONV]
    cb = conv[:, D_CONV:2 * D_CONV]
    ubuf[8:8 + tm, :] = conv[:, 2 * D_CONV:3 * D_CONV] * cx
    yc = (cw_ref[0:1, :] * ubuf[6:6 + tm, :] + cw_ref[1:2, :] * ubuf[7:7 + tm, :]
          + cw_ref[2:3, :] * ubuf[8:8 + tm, :])
    cs_ref[...] = ubuf[tm + 6:tm + 8, :]

    pbuf[16:16 + tm, :] = pool_in
    pos = lax.broadcasted_iota(jnp.int32, (tm, 1), 0) + t * tm + 1
    d_groups = []
    for g, w in enumerate(POOL_WINDOWS):
        c0 = g * POOL_GROUP
        cur = pbuf[16:16 + tm, c0:c0 + POOL_GROUP]
        tot = cur
        for j in range(1, w):
            tot = tot + pbuf[16 - j:16 - j + tm, c0:c0 + POOL_GROUP]
        cnt = jnp.minimum(pos, w).astype(F32)
        d_groups.append(tot / cnt - cur)
    pst_ref[...] = pbuf[tm + 1:tm + 16, :]

    h_ref[...] = _mix_tail(x, att_ref[...], cb, yc, d_groups, (gate_att, gate_conv, gate_pool),
                           wao_ref, wco_ref, wp_ref, ps_ref, wo_ref)


def _weight_specs(l, grid_rank):
    def const(*idx):
        if grid_rank == 1:
            return lambda i: idx
        return lambda b, t: idx

    return [
        pl.BlockSpec((None, 1, D_MODEL), const(l, 0, 0)),
        pl.BlockSpec((None, D_MODEL, D_IN), const(l, 0, 0)),
        pl.BlockSpec((None, D_ATTN, D_MODEL), const(l, 0, 0)),
        pl.BlockSpec((None, CONV_WIDTH, D_CONV), const(l, 0, 0)),
        pl.BlockSpec((None, D_CONV, D_MODEL), const(l, 0, 0)),
        pl.BlockSpec((None, len(POOL_WINDOWS), POOL_GROUP, POOL_OUT_GROUP), const(l, 0, 0, 0)),
        pl.BlockSpec((None, 1, D_MODEL), const(l, 0, 0)),
        pl.BlockSpec((None, D_MODEL, D_MODEL), const(l, 0, 0)),
    ]


def _prompt_mix(l, x, sinks, weights, prev_states):
    tm = PROMPT_TILE
    nt = SEQ // tm
    n_alias = len(prev_states)
    in_specs = ([pl.BlockSpec(memory_space=pltpu.SMEM),
                 pl.BlockSpec((tm, D_MODEL), lambda b, t: (b * nt + t, 0))] + _weight_specs(l, 2)
                + [pl.BlockSpec(memory_space=pl.ANY)] * n_alias)
    out_specs = [
        pl.BlockSpec((tm, D_MODEL), lambda b, t: (b * nt + t, 0)),
        pl.BlockSpec((None, None, WINDOW, D_KV), lambda b, t: (l, b, 0, 0)),
        pl.BlockSpec((None, None, WINDOW, D_KV), lambda b, t: (l, b, 0, 0)),
        pl.BlockSpec((None, None, CONV_WIDTH - 1, D_CONV), lambda b, t: (l, b, 0, 0)),
        pl.BlockSpec((None, None, POOL_BUF, D_POOL), lambda b, t: (l, b, 0, 0)),
    ]
    out_shape = [
        jax.ShapeDtypeStruct((N_PAD, D_MODEL), F32),
        jax.ShapeDtypeStruct((DEPTH, BATCH, WINDOW, D_KV), F32),
        jax.ShapeDtypeStruct((DEPTH, BATCH, WINDOW, D_KV), F32),
        jax.ShapeDtypeStruct((DEPTH, BATCH, CONV_WIDTH - 1, D_CONV), F32),
        jax.ShapeDtypeStruct((DEPTH, BATCH, POOL_BUF, D_POOL), F32),
    ]
    scratch = [
        pltpu.VMEM((tm, D_ATTN), BF16),
        pltpu.VMEM((4, WINDOW + tm, D_KV), BF16),
        pltpu.VMEM((4, WINDOW + tm, D_KV), BF16),
        pltpu.VMEM((tm, D_ATTN), BF16),
        pltpu.VMEM((8 + tm, D_CONV), F32),
        pltpu.VMEM((16 + tm, D_POOL), F32),
        pltpu.VMEM((2, N_Q_HEADS, WINDOW, 2 * WINDOW), F32),
    ]
    outs = pl.pallas_call(
        functools.partial(_prompt_mix_kernel, n_alias=n_alias),
        grid=(BATCH, nt),
        in_specs=in_specs,
        out_specs=out_specs,
        out_shape=out_shape,
        scratch_shapes=scratch,
        input_output_aliases={10 + i: 1 + i for i in range(n_alias)},
        compiler_params=pltpu.CompilerParams(
            dimension_semantics=("arbitrary", "arbitrary"), vmem_limit_bytes=VMEM_LIMIT),
        name=f"prompt_mix_l{l}",
    )(sinks, x, *weights, *prev_states)
    return outs[0], tuple(outs[1:])


def _sample_mix_kernel(*refs, n_alias):
    (sinks_ref, x_ref, kc_ref, vc_ref, cst_ref, pin_ref,
     nw_ref, win_ref, wao_ref, cw_ref, wco_ref, wp_ref, ps_ref, wo_ref) = refs[:14]
    (h_ref, ko_ref, vo_ref, co_ref, po_ref,
     qbuf, knb, vnb, att_ref, cbuf, ebuf, bias_c, bias_n) = refs[14 + n_alias:]
    ns, nr, T = SAMPLE_SEQS, SAMPLE_ROWS, DEC_SEQ
    x = x_ref[...]
    nb = _rms(x, nw_ref[...]).astype(BF16)

    qkv = _dot(nb, win_ref[:, OFF_Q:OFF_CX])
    qbuf[...] = (qkv[:, OFF_Q:OFF_K] * ATTN_SCALE).astype(BF16)
    k = qkv[:, OFF_K:OFF_V]
    v = qkv[:, OFF_V:OFF_CX]
    knb[...] = k.astype(BF16)
    vnb[...] = v.astype(BF16)
    ko_ref[:, 0:WINDOW - T, :] = kc_ref[:, T:WINDOW, :]
    vo_ref[:, 0:WINDOW - T, :] = vc_ref[:, T:WINDOW, :]
    ko_ref[:, WINDOW - T:WINDOW, :] = k.reshape(ns, T, D_KV)
    vo_ref[:, WINDOW - T:WINDOW, :] = v.reshape(ns, T, D_KV)

    gr = ATT_GROUP * T
    gc = ATT_GROUP * WINDOW

    @pl.when(pl.program_id(0) == 0)
    def _():
        rq = lax.broadcasted_iota(jnp.int32, (gr, gc), 0)
        cq = lax.broadcasted_iota(jnp.int32, (gr, gc), 1)
        tq = rq & (T - 1)
        sc_pos = cq & (WINDOW - 1)
        valid_c = jnp.logical_and((rq >> 3) == (cq >> 7), sc_pos >= tq)
        dist_c = (WINDOW + tq - sc_pos).astype(F32)
        rn = lax.broadcasted_iota(jnp.int32, (gr, gr), 0)
        cn = lax.broadcasted_iota(jnp.int32, (gr, gr), 1)
        tn = rn & (T - 1)
        jn = cn & (T - 1)
        valid_n = jnp.logical_and((rn >> 3) == (cn >> 3), jn <= tn)
        dist_n = (tn - jn).astype(F32)
        for h in range(N_Q_HEADS):
            bias_c[h] = jnp.where(valid_c, -_head_slope(h) * dist_c, -jnp.inf)
            bias_n[h] = jnp.where(valid_n, -_head_slope(h) * dist_n, -jnp.inf)

    z_parts = []
    for gi in range(ns // ATT_GROUP):
        z_parts.append(_z_part(nb, win_ref, gi))
        r0 = gi * gr
        s0 = gi * ATT_GROUP
        qg = qbuf[r0:r0 + gr, :]
        kcg = kc_ref[s0:s0 + ATT_GROUP, :, :].reshape(gc, D_KV).astype(BF16)
        vcg = vc_ref[s0:s0 + ATT_GROUP, :, :].reshape(gc, D_KV).astype(BF16)
        kng = knb[r0:r0 + gr, :]
        vng = vnb[r0:r0 + gr, :]
        outs = []
        for h in range(N_Q_HEADS):
            kv = h // GQA_GROUP
            lo, hi = kv * HEAD_DIM, (kv + 1) * HEAD_DIM
            qh = qg[:, h * HEAD_DIM:(h + 1) * HEAD_DIM]
            s_c = _dot_nt(qh, kcg[:, lo:hi]) + bias_c[h]
            s_n = _dot_nt(qh, kng[:, lo:hi]) + bias_n[h]
            outs.append(_softmax_pv([(s_c, vcg[:, lo:hi]), (s_n, vng[:, lo:hi])], sinks_ref[h]))
        att_ref[r0:r0 + gr, :] = jnp.concatenate(outs, axis=-1).astype(BF16)

    (conv,), (pool_in, gate_att), (gate_conv,), (gate_pool,) = z_parts
    cx = conv[:, 0:D_CONV]
    cb = conv[:, D_CONV:2 * D_CONV]
    u = conv[:, 2 * D_CONV:3 * D_CONV] * cx
    cbuf[:, 6:8, :] = cst_ref[...]
    cbuf[:, 8:16, :] = u.reshape(ns, T, D_CONV)
    w0 = cw_ref[0:1, :].reshape(1, 1, D_CONV)
    w1 = cw_ref[1:2, :].reshape(1, 1, D_CONV)
    w2 = cw_ref[2:3, :].reshape(1, 1, D_CONV)
    yc = (w0 * cbuf[:, 6:14, :] + w1 * cbuf[:, 7:15, :] + w2 * cbuf[:, 8:16, :]).reshape(nr, D_CONV)
    co_ref[...] = cbuf[:, 14:16, :]

    ebuf[:, 1:16, :] = pin_ref[...]
    ebuf[:, 16:24, :] = pool_in.reshape(ns, T, D_POOL)
    d_groups = []
    for g, w in enumerate(POOL_WINDOWS):
        c0 = g * POOL_GROUP
        cur = ebuf[:, 16:24, c0:c0 + POOL_GROUP]
        tot = cur
        for j in range(1, w):
            tot = tot + ebuf[:, 16 - j:24 - j, c0:c0 + POOL_GROUP]
        d_groups.append((tot / float(w) - cur).reshape(nr, POOL_GROUP))
    po_ref[...] = ebuf[:, 9:24, :]

    h_ref[...] = _mix_tail(x, att_ref[...], cb, yc, d_groups, (gate_att, gate_conv, gate_pool),
                           wao_ref, wco_ref, wp_ref, ps_ref, wo_ref)


def _sample_mix(l, x, x_row0, sinks, kc, vc, cst, pst, weights, h_buf, prev_states):
    ns, nr = SAMPLE_SEQS, SAMPLE_ROWS
    n_alias = 1 + len(prev_states)
    xb0 = x_row0 // nr
    hb0 = N_PROMPT // nr
    in_specs = [
        pl.BlockSpec(memory_space=pltpu.SMEM),
        pl.BlockSpec((nr, D_MODEL), lambda i: (xb0 + i, 0)),
        pl.BlockSpec((None, ns, WINDOW, D_KV), lambda i: (l, i, 0, 0)),
        pl.BlockSpec((None, ns, WINDOW, D_KV), lambda i: (l, i, 0, 0)),
        pl.BlockSpec((None, ns, CONV_WIDTH - 1, D_CONV), lambda i: (l, i, 0, 0)),
        pl.BlockSpec((None, ns, POOL_BUF, D_POOL), lambda i: (l, i, 0, 0)),
    ] + _weight_specs(l, 1) + [pl.BlockSpec(memory_space=pl.ANY)] * n_alias
    out_specs = [
        pl.BlockSpec((nr, D_MODEL), lambda i: (hb0 + i, 0)),
        pl.BlockSpec((None, ns, WINDOW, D_KV), lambda i: (l, i, 0, 0)),
        pl.BlockSpec((None, ns, WINDOW, D_KV), lambda i: (l, i, 0, 0)),
        pl.BlockSpec((None, ns, CONV_WIDTH - 1, D_CONV), lambda i: (l, i, 0, 0)),
        pl.BlockSpec((None, ns, POOL_BUF, D_POOL), lambda i: (l, i, 0, 0)),
    ]
    out_shape = [
        jax.ShapeDtypeStruct((N_PAD, D_MODEL), F32),
        jax.ShapeDtypeStruct((DEPTH, DEC_BATCH, WINDOW, D_KV), F32),
        jax.ShapeDtypeStruct((DEPTH, DEC_BATCH, WINDOW, D_KV), F32),
        jax.ShapeDtypeStruct((DEPTH, DEC_BATCH, CONV_WIDTH - 1, D_CONV), F32),
        jax.ShapeDtypeStruct((DEPTH, DEC_BATCH, POOL_BUF, D_POOL), F32),
    ]
    scratch = [
        pltpu.VMEM((nr, D_ATTN), BF16),
        pltpu.VMEM((nr, D_KV), BF16),
        pltpu.VMEM((nr, D_KV), BF16),
        pltpu.VMEM((nr, D_ATTN), BF16),
        pltpu.VMEM((ns, 16, D_CONV), F32),
        pltpu.VMEM((ns, 24, D_POOL), F32),
        pltpu.VMEM((N_Q_HEADS, ATT_GROUP * DEC_SEQ, ATT_GROUP * WINDOW), F32),
        pltpu.VMEM((N_Q_HEADS, ATT_GROUP * DEC_SEQ, ATT_GROUP * DEC_SEQ), F32),
    ]
    outs = pl.pallas_call(
        functools.partial(_sample_mix_kernel, n_alias=n_alias),
        grid=(DEC_BATCH // ns,),
        in_specs=in_specs,
        out_specs=out_specs,
        out_shape=out_shape,
        scratch_shapes=scratch,
        input_output_aliases={14 + i: i for i in range(n_alias)},
        compiler_params=pltpu.CompilerParams(
            dimension_semantics=("arbitrary",), vmem_limit_bytes=VMEM_LIMIT),
        name=f"sample_mix_l{l}",
    )(sinks, x, kc, vc, cst, pst, *weights, h_buf, *prev_states)
    return outs[0], tuple(outs[1:])


def _ffn_kernel(h_ref, nw_ref, wg_ref, wu_ref, wd_ref, eg_ref, eu_ref, ed_ref, o_ref, egb_ref, eub_ref, edb_ref):
    h = h_ref[...]
    hn = _rms(h, nw_ref[...]).astype(BF16)
    g = _dot(hn, wg_ref[...])
    u = _dot(hn, wu_ref[...])
    o_ref[...] = h + _dot((g * _sigmoid(g) * u).astype(BF16), wd_ref[...])
    egb_ref[...] = eg_ref[...].astype(BF16)
    eub_ref[...] = eu_ref[...].astype(BF16)
    edb_ref[...] = ed_ref[...].astype(BF16)


def _ffn_dense(h, nw, wg, wu, wd, eg, eu, ed):
    tm = SUB_ROWS
    steps = N_TOK // tm
    n_cast = CAST_STEPS
    assert n_cast <= steps
    gr = eg.shape[0] // n_cast
    dr = ed.shape[0] // n_cast
    assert gr * n_cast == eg.shape[0] and dr * n_cast == ed.shape[0] and gr % 16 == 0 and dr % 16 == 0
    resident = dict(pipeline_mode=pl.Buffered(1))

    def cast_map(i):
        return (jnp.minimum(i, n_cast - 1), 0)

    return pl.pallas_call(
        _ffn_kernel,
        grid=(steps,),
        in_specs=[
            pl.BlockSpec((tm, D_MODEL), lambda i: (i, 0)),
            pl.BlockSpec((1, D_MODEL), lambda i: (0, 0)),
            pl.BlockSpec((None, D_MODEL, D_FF), lambda i: (0, 0, 0), **resident),
            pl.BlockSpec((None, D_MODEL, D_FF), lambda i: (0, 0, 0), **resident),
            pl.BlockSpec((None, D_FF, D_MODEL), lambda i: (0, 0, 0), **resident),
            pl.BlockSpec((gr, D_EXPERT), cast_map),
            pl.BlockSpec((gr, D_EXPERT), cast_map),
            pl.BlockSpec((dr, D_MODEL), cast_map),
        ],
        out_specs=[
            pl.BlockSpec((tm, D_MODEL), lambda i: (i, 0)),
            pl.BlockSpec((gr, D_EXPERT), cast_map),
            pl.BlockSpec((gr, D_EXPERT), cast_map),
            pl.BlockSpec((dr, D_MODEL), cast_map),
        ],
        out_shape=[
            jax.ShapeDtypeStruct((N_PAD, D_MODEL), F32),
            jax.ShapeDtypeStruct(eg.shape, BF16),
            jax.ShapeDtypeStruct(eu.shape, BF16),
            jax.ShapeDtypeStruct(ed.shape, BF16),
        ],
        compiler_params=pltpu.CompilerParams(
            dimension_semantics=("arbitrary",), vmem_limit_bytes=VMEM_LIMIT),
        name="ffn_dense",
    )(h, nw, wg, wu, wd, eg, eu, ed)


def _router_kernel(h_ref, nw_ref, rt_ref, hn_ref, slot_ref, col_ref, cnt_ref):
    tm = FFN_TILE
    hn = _rms(h_ref[...], nw_ref[...])
    hn_ref[...] = hn.astype(BF16)
    logits = lax.dot_general(rt_ref[...], hn, (((1,), (1,)), ((), ())),
                             precision=lax.Precision.HIGHEST, preferred_element_type=F32)
    eidx = lax.broadcasted_iota(jnp.int32, (N_EXPERTS, tm), 0).astype(F32)
    none = float(N_EXPERTS)
    m1 = jnp.max(logits, axis=0, keepdims=True)
    i1 = jnp.min(jnp.where(logits == m1, eidx, none), axis=0, keepdims=True)
    rest = jnp.where(eidx == i1, -jnp.inf, logits)
    m2 = jnp.max(rest, axis=0, keepdims=True)
    i2 = jnp.min(jnp.where(rest == m2, eidx, none), axis=0, keepdims=True)
    e2 = jnp.exp(m2 - m1)
    w1 = 1.0 / (1.0 + e2)
    w2 = e2 / (1.0 + e2)
    sel1 = eidx == i1
    sel2 = eidx == i2
    gate = jnp.where(sel1, w1, jnp.where(sel2, w2, 0.0))
    chosen = jnp.logical_or(sel1, sel2)
    mask = jnp.where(chosen, 1.0, 0.0)
    srow = lax.broadcasted_iota(jnp.int32, (tm, tm), 0)
    scol = lax.broadcasted_iota(jnp.int32, (tm, tm), 1)
    upper = jnp.where(srow < scol, 1.0, 0.0).astype(BF16)
    mask16 = jnp.concatenate([mask, jnp.zeros_like(mask)], axis=0).astype(BF16)
    slot = jnp.where(chosen, _dot(mask16, upper)[0:N_EXPERTS, :], -1.0)
    slot_ref[...] = slot.astype(jnp.int32)
    cnt_ref[...] = jnp.broadcast_to(jnp.sum(mask, axis=1, keepdims=True), (N_EXPERTS, 128))
    both = jnp.concatenate([slot, gate, jnp.zeros((128 - 2 * N_EXPERTS, tm), F32)], axis=0)
    col_ref[...] = both.T[:, 0:2 * N_EXPERTS]


def _router(h, nw, router_t):
    tm = FFN_TILE
    nt = N_UNITS
    return pl.pallas_call(
        _router_kernel,
        grid=(nt,),
        in_specs=[
            pl.BlockSpec((tm, D_MODEL), lambda i: (i, 0)),
            pl.BlockSpec((1, D_MODEL), lambda i: (0, 0)),
            pl.BlockSpec((N_EXPERTS, D_MODEL), lambda i: (0, 0)),
        ],
        out_specs=[
            pl.BlockSpec((tm, D_MODEL), lambda i: (i, 0)),
            pl.BlockSpec((N_EXPERTS, tm), lambda i: (0, i)),
            pl.BlockSpec((tm, 2 * N_EXPERTS), lambda i: (i, 0)),
            pl.BlockSpec((None, N_EXPERTS, 128), lambda i: (i, 0, 0)),
        ],
        out_shape=[
            jax.ShapeDtypeStruct((N_PAD, D_MODEL), BF16),
            jax.ShapeDtypeStruct((N_EXPERTS, N_PAD), jnp.int32),
            jax.ShapeDtypeStruct((N_PAD, 2 * N_EXPERTS), F32),
            jax.ShapeDtypeStruct((nt, N_EXPERTS, 128), F32),
        ],
        compiler_params=pltpu.CompilerParams(
            dimension_semantics=("arbitrary",), vmem_limit_bytes=VMEM_LIMIT),
        name="moe_router",
    )(h, nw, router_t)


def _plan(cnt):
    seg = -(-cnt // ROW_ALIGN) * ROW_ALIGN
    tiles_e = -(-jnp.sum(seg, axis=0) // EXP_TILE)
    cum_tiles = jnp.cumsum(tiles_e)
    base_e = (cum_tiles - tiles_e) * EXP_TILE
    dst = base_e[None, :] + jnp.cumsum(seg, axis=0) - seg
    n_tiles = cum_tiles[-1]
    i = jnp.arange(MAX_TILES, dtype=jnp.int32)
    tile_expert = jnp.sum((cum_tiles[None, :] <= i[:, None]).astype(jnp.int32), axis=1)
    tile_expert = jnp.minimum(tile_expert, N_EXPERTS - 1)
    tile_expert = jnp.where(i < n_tiles, tile_expert, tile_expert[jnp.maximum(n_tiles - 1, 0)])
    return dst.reshape(-1).astype(jnp.int32), tile_expert, n_tiles.reshape(1).astype(jnp.int32)


def _ceil_blocks(n, block):
    return (n + block - 1) >> (block.bit_length() - 1)


def _segment_copies(n_rows, src_at, dst_at, sem):
    n = n_rows >> (ROW_ALIGN.bit_length() - 1)
    out = []
    for bit in range(SEG_BITS - 1, -1, -1):
        size = ROW_ALIGN << bit
        off = pl.multiple_of(((n >> (bit + 1)) << (bit + 1)) * ROW_ALIGN, ROW_ALIGN)
        out.append((((n >> bit) & 1) == 1, pltpu.make_async_copy(src_at(off, size), dst_at(off, size), sem)))
    return out


def _start_all(copies):
    for pred, cp in copies:
        pl.when(pred)(cp.start)


def _wait_all(copies):
    for pred, cp in copies:
        pl.when(pred)(cp.wait)


def _dispatch_kernel(cnt_ref, dst_ref, hn_ref, srow_ref, init_ref, xs_ref, stage, sems):
    del init_ref
    tu, rb = FFN_TILE, ROW_BLOCK
    t = pl.program_id(0)

    def copies(e):
        slot = e % 2
        n_rows = _ceil_blocks(cnt_ref[t * N_EXPERTS + e], ROW_ALIGN) * ROW_ALIGN
        d0 = dst_ref[t * N_EXPERTS + e]
        return _segment_copies(
            n_rows,
            lambda off, size: stage.at[slot, pl.ds(off, size)],
            lambda off, size: xs_ref.at[pl.ds(pl.multiple_of(d0 + off, ROW_ALIGN), size)],
            sems.at[slot])

    for e in range(N_EXPERTS):
        slot = e % 2
        if e >= 2:
            _wait_all(copies(e - 2))
        slot_row = srow_ref[e:e + 1, :]

        def gather(b, carry, slot=slot, slot_row=slot_row):
            r0 = pl.multiple_of(b * rb, rb)
            rid = lax.broadcasted_iota(jnp.int32, (rb, tu), 0) + b * rb
            onehot = jnp.where(slot_row == rid, 1.0, 0.0).astype(BF16)
            stage[slot, pl.ds(r0, rb), :] = _dot(onehot, hn_ref[...]).astype(BF16)
            return carry

        lax.fori_loop(0, _ceil_blocks(cnt_ref[t * N_EXPERTS + e], rb), gather, 0)
        _start_all(copies(e))

    _wait_all(copies(N_EXPERTS - 2))
    _wait_all(copies(N_EXPERTS - 1))


def _dispatch(cnt, dst, hn, slot_row):
    tm = FFN_TILE
    n_rows = MAX_TILES * EXP_TILE
    grid_spec = pltpu.PrefetchScalarGridSpec(
        num_scalar_prefetch=2,
        grid=(N_UNITS,),
        in_specs=[
            pl.BlockSpec((tm, D_MODEL), lambda t, c, d: (t, 0)),
            pl.BlockSpec((N_EXPERTS, tm), lambda t, c, d: (0, t)),
            pl.BlockSpec(memory_space=pl.ANY),
        ],
        out_specs=pl.BlockSpec(memory_space=pl.ANY),
        scratch_shapes=[pltpu.VMEM((2, tm, D_MODEL), BF16), pltpu.SemaphoreType.DMA((2,))],
    )
    return pl.pallas_call(
        _dispatch_kernel,
        grid_spec=grid_spec,
        out_shape=jax.ShapeDtypeStruct((n_rows, D_MODEL), BF16),
        input_output_aliases={4: 0},
        compiler_params=pltpu.CompilerParams(
            dimension_semantics=("arbitrary",), vmem_limit_bytes=VMEM_LIMIT),
        name="moe_dispatch",
    )(cnt, dst, hn, slot_row, jnp.zeros((n_rows, D_MODEL), BF16))


def _experts_kernel(texp_ref, ntile_ref, xs_ref, wg_ref, wu_ref, wd_ref, ys_ref, acc_ref):
    del texp_ref
    i, c = pl.program_id(0), pl.program_id(1)
    nc = pl.num_programs(1)

    @pl.when(i < ntile_ref[0])
    def _():
        @pl.when(c == 0)
        def _():
            acc_ref[...] = jnp.zeros_like(acc_ref)

        for r in range(0, EXP_TILE, SUB_ROWS):
            xb = xs_ref[r:r + SUB_ROWS, :]
            g = _dot(xb, wg_ref[...])
            u = _dot(xb, wu_ref[...])
            acc_ref[r:r + SUB_ROWS, :] += _dot((g * _sigmoid(g) * u).astype(BF16), wd_ref[...])

        @pl.when(c == nc - 1)
        def _():
            ys_ref[...] = acc_ref[...].astype(BF16)


def _experts(tile_expert, n_tiles, xs, wg, wu, wd):
    tm = EXP_TILE
    nc = D_EXPERT // EXP_CHUNK

    def row_map(i, c, te, nt):
        return (jnp.minimum(i, jnp.maximum(nt[0] - 1, 0)), 0)

    def chunk(i, c, nt):
        return jnp.where(i < nt[0], c, nc - 1)

    grid_spec = pltpu.PrefetchScalarGridSpec(
        num_scalar_prefetch=2,
        grid=(MAX_TILES, nc),
        in_specs=[
            pl.BlockSpec((tm, D_MODEL), row_map),
            pl.BlockSpec((None, D_MODEL, EXP_CHUNK), lambda i, c, te, nt: (te[i], 0, chunk(i, c, nt))),
            pl.BlockSpec((None, D_MODEL, EXP_CHUNK), lambda i, c, te, nt: (te[i], 0, chunk(i, c, nt))),
            pl.BlockSpec((None, EXP_CHUNK, D_MODEL), lambda i, c, te, nt: (te[i], chunk(i, c, nt), 0)),
        ],
        out_specs=pl.BlockSpec((tm, D_MODEL), row_map),
        scratch_shapes=[pltpu.VMEM((tm, D_MODEL), F32)],
    )
    return pl.pallas_call(
        _experts_kernel,
        grid_spec=grid_spec,
        out_shape=jax.ShapeDtypeStruct((MAX_TILES * tm, D_MODEL), BF16),
        compiler_params=pltpu.CompilerParams(
            dimension_semantics=("arbitrary", "arbitrary"), vmem_limit_bytes=VMEM_LIMIT),
        name="moe_experts",
    )(tile_expert, n_tiles, xs, wg, wu, wd)


def _combine_kernel(cnt_ref, dst_ref, ys_ref, col_ref, h_ref, nf_ref, op_ref, os_ref, stage, sems, acc_ref):
    tu, rb = FFN_TILE, 2 * ROW_BLOCK
    t = pl.program_id(0)

    @pl.when(t == 0)
    def _():
        stage[...] = jnp.zeros_like(stage)

    def copies(e):
        slot = e % 2
        n_rows = _ceil_blocks(cnt_ref[t * N_EXPERTS + e], ROW_ALIGN) * ROW_ALIGN
        d0 = dst_ref[t * N_EXPERTS + e]
        return _segment_copies(
            n_rows,
            lambda off, size: ys_ref.at[pl.ds(pl.multiple_of(d0 + off, ROW_ALIGN), size)],
            lambda off, size: stage.at[slot, pl.ds(off, size)],
            sems.at[slot])

    _start_all(copies(0))
    acc_ref[...] = h_ref[...]
    for e in range(N_EXPERTS):
        slot = e % 2
        if e + 1 < N_EXPERTS:
            _start_all(copies(e + 1))
        _wait_all(copies(e))
        slot_col = col_ref[:, e:e + 1]
        gate_col = col_ref[:, e + N_EXPERTS:e + N_EXPERTS + 1]

        def scatter(b, carry, slot=slot, slot_col=slot_col, gate_col=gate_col):
            r0 = pl.multiple_of(b * rb, rb)
            cid = (lax.broadcasted_iota(jnp.int32, (tu, rb), 1) + b * rb).astype(F32)
            weighted = jnp.where(slot_col == cid, gate_col, 0.0).astype(BF16)
            acc_ref[...] += _dot(weighted, stage[slot, pl.ds(r0, rb), :])
            return carry

        lax.fori_loop(0, _ceil_blocks(cnt_ref[t * N_EXPERTS + e], rb), scatter, 0)

    out = _rms(acc_ref[...], nf_ref[...])

    @pl.when(t < N_PROMPT // FFN_TILE)
    def _():
        op_ref[...] = out

    @pl.when(t >= N_PROMPT // FFN_TILE)
    def _():
        os_ref[...] = out


def _combine(cnt, dst, ys, cols, h, nf):
    tm = FFN_TILE
    np_tiles = N_PROMPT // tm
    assert N_SAMPLE == tm
    grid_spec = pltpu.PrefetchScalarGridSpec(
        num_scalar_prefetch=2,
        grid=(N_UNITS,),
        in_specs=[
            pl.BlockSpec(memory_space=pl.ANY),
            pl.BlockSpec((tm, 2 * N_EXPERTS), lambda t, c, d: (t, 0)),
            pl.BlockSpec((tm, D_MODEL), lambda t, c, d: (t, 0)),
            pl.BlockSpec((1, D_MODEL), lambda t, c, d: (0, 0)),
        ],
        out_specs=[
            pl.BlockSpec((tm, D_MODEL), lambda t, c, d: (jnp.minimum(t, np_tiles - 1), 0)),
            pl.BlockSpec((tm, D_MODEL), lambda t, c, d: (0, 0)),
        ],
        scratch_shapes=[pltpu.VMEM((2, tm, D_MODEL), BF16), pltpu.SemaphoreType.DMA((2,)),
                        pltpu.VMEM((tm, D_MODEL), F32)],
    )
    return pl.pallas_call(
        _combine_kernel,
        grid_spec=grid_spec,
        out_shape=[
            jax.ShapeDtypeStruct((N_PROMPT, D_MODEL), F32),
            jax.ShapeDtypeStruct((N_SAMPLE, D_MODEL), F32),
        ],
        compiler_params=pltpu.CompilerParams(
            dimension_semantics=("arbitrary",), vmem_limit_bytes=VMEM_LIMIT),
        name="moe_combine",
    )(cnt, dst, ys, cols, h, nf)


def kernel(x_prompt, x_sample, cache_win_k, cache_win_v, state_conv, state_pool, norm_mix, w_in, attn_sinks,
           w_attn_out, conv_w, w_conv_out, w_pool, pool_scale, w_out, norm_ffn, ffn_w_gate, ffn_w_up, ffn_w_down,
           moe_router, moe_w_gate, moe_w_up, moe_w_down, norm_final):
    mix_weights = (norm_mix.reshape(DEPTH, 1, D_MODEL), w_in.astype(BF16), w_attn_out.astype(BF16), conv_w,
                   w_conv_out.astype(BF16), w_pool.astype(BF16), pool_scale.reshape(DEPTH, 1, D_MODEL),
                   w_out.astype(BF16))
    kc = cache_win_k.reshape(DEPTH, DEC_BATCH, WINDOW, D_KV)
    vc = cache_win_v.reshape(DEPTH, DEC_BATCH, WINDOW, D_KV)

    xp = x_prompt.reshape(N_PROMPT, D_MODEL)
    xs, xs_row0 = x_sample.reshape(N_SAMPLE, D_MODEL), 0
    p_states, s_states = (), ()
    for l in range(DEPTH):
        sinks = attn_sinks[l]
        h, p_states = _prompt_mix(l, xp, sinks, mix_weights, p_states)
        h, s_states = _sample_mix(l, xs, xs_row0, sinks, kc, vc, state_conv, state_pool, mix_weights, h, s_states)
        i = l // 2
        nw = norm_ffn[l].reshape(1, D_MODEL)
        if l % 2 == 0:
            assert l + 1 < DEPTH and moe_w_gate.shape[0] == 1
            xp, eg, eu, ed = _ffn_dense(
                h, nw, ffn_w_gate[i:i + 1].astype(BF16), ffn_w_up[i:i + 1].astype(BF16),
                ffn_w_down[i:i + 1].astype(BF16),
                moe_w_gate.reshape(N_EXPERTS * D_MODEL, D_EXPERT), moe_w_up.reshape(N_EXPERTS * D_MODEL, D_EXPERT),
                moe_w_down.reshape(N_EXPERTS * D_EXPERT, D_MODEL))
            xs, xs_row0 = xp, N_PROMPT
        else:
            hn, slot_row, cols, cnt = _router(h, nw, moe_router[i].T)
            cnt = cnt[:, :, 0].astype(jnp.int32)
            dst, tile_expert, n_tiles = _plan(cnt)
            cnt = cnt.reshape(-1)
            xs_sorted = _dispatch(cnt, dst, hn, slot_row)
            ys_sorted = _experts(tile_expert, n_tiles, xs_sorted,
                                 eg.reshape(N_EXPERTS, D_MODEL, D_EXPERT), eu.reshape(N_EXPERTS, D_MODEL, D_EXPERT),
                                 ed.reshape(N_EXPERTS, D_EXPERT, D_MODEL))
            y_prompt, y_sample = _combine(cnt, dst, ys_sorted, cols, h, norm_final.reshape(1, D_MODEL))

    pk, pv, pc, pp = p_states
    sk, sv, sc, sp = s_states
    kv_shape_p = (DEPTH, BATCH, WINDOW, N_KV_HEADS, HEAD_DIM)
    kv_shape_s = (DEPTH, DEC_BATCH, WINDOW, N_KV_HEADS, HEAD_DIM)
    return (y_prompt.reshape(BATCH, SEQ, D_MODEL), y_sample.reshape(DEC_BATCH, DEC_SEQ, D_MODEL),
            pk.reshape(kv_shape_p), pv.reshape(kv_shape_p), pc, pp,
            sk.reshape(kv_shape_s), sv.reshape(kv_shape_s), sc, sp)
```

```python
import functools
import math

import jax
import jax.numpy as jnp
from jax import lax
from jax.experimental import pallas as pl
from jax.experimental.pallas import tpu as pltpu

F32 = jnp.float32
BF16 = jnp.bfloat16

D_MODEL = 1024
BATCH = 4
SEQ = 4096
DEPTH = 2
DEC_BATCH = 128
DEC_SEQ = 8
HEAD_DIM = 64
N_Q_HEADS = 8
N_KV_HEADS = 2
GQA_GROUP = N_Q_HEADS // N_KV_HEADS
D_ATTN = N_Q_HEADS * HEAD_DIM
D_KV = N_KV_HEADS * HEAD_DIM
WINDOW = 128
ATTN_SCALE = HEAD_DIM ** -0.5
D_CONV = D_MODEL // 2
CONV_WIDTH = 3
D_POOL = D_MODEL // 2
POOL_WINDOWS = (2, 4, 8, 16)
POOL_GROUP = D_POOL // len(POOL_WINDOWS)
POOL_OUT_GROUP = D_MODEL // len(POOL_WINDOWS)
POOL_BUF = max(POOL_WINDOWS) - 1
OFF_Q = 0
OFF_K = OFF_Q + D_ATTN
OFF_V = OFF_K + D_KV
OFF_CX = OFF_V + D_KV
OFF_CB = OFF_CX + D_CONV
OFF_CC = OFF_CB + D_CONV
OFF_P = OFF_CC + D_CONV
OFF_G = OFF_P + D_POOL
D_IN = OFF_G + 3 * D_MODEL
D_FF = 2816
N_EXPERTS = 8
D_EXPERT = 3584
EPS = 1e-5
LOG2E = math.log2(math.e)

N_PROMPT = BATCH * SEQ
N_SAMPLE = DEC_BATCH * DEC_SEQ
N_TOK = N_PROMPT + N_SAMPLE

PROMPT_TILE = 512
SAMPLE_SEQS = 32
SAMPLE_ROWS = SAMPLE_SEQS * DEC_SEQ
ATT_GROUP = 8
FFN_TILE = 1024
EXP_CHUNK = D_EXPERT // 2
SUB_ROWS = 256
CAST_STEPS = 64
ROW_BLOCK = 128
ROW_ALIGN = 16
EXP_TILE = 512
N_UNITS = N_TOK // FFN_TILE
MAX_TILES = -(-(2 * N_TOK + N_UNITS * N_EXPERTS * (ROW_ALIGN - 1)) // EXP_TILE) + N_EXPERTS
N_PAD = N_TOK
SEG_BITS = 7
assert FFN_TILE == ROW_ALIGN << (SEG_BITS - 1)
VMEM_LIMIT = 58 * 1024 * 1024


def _dot(a, b):
    return jnp.dot(a, b, preferred_element_type=F32)


def _dot_nt(a, b):
    return lax.dot_general(a, b, (((1,), (1,)), ((), ())), preferred_element_type=F32)


def _rms(x, g):
    return x * lax.rsqrt(jnp.mean(x * x, axis=-1, keepdims=True) + EPS) * g


def _sigmoid(x):
    return 0.5 * jnp.tanh(0.5 * x) + 0.5


def _head_slope(h):
    return float(2.0 ** (-8.0 * (h + 1) / N_Q_HEADS))


def _softmax_pv(parts, sink):
    m = sink
    for s, _ in parts:
        m = jnp.maximum(m, jnp.max(s, axis=-1, keepdims=True))
    den = jnp.exp(sink - m)
    o = None
    for s, v in parts:
        p = jnp.exp(s - m)
        den = den + jnp.sum(p, axis=-1, keepdims=True)
        pv = _dot(p.astype(BF16), v)
        o = pv if o is None else o + pv
    return o / den


def _z_part(nb, win_ref, i):
    if i == 0:
        return (_dot(nb, win_ref[:, OFF_CX:OFF_P]),)
    if i == 1:
        return (_dot(nb, win_ref[:, OFF_P:OFF_G]), _dot(nb, win_ref[:, OFF_G:OFF_G + D_MODEL]))
    return (_dot(nb, win_ref[:, OFF_G + (i - 1) * D_MODEL:OFF_G + i * D_MODEL]),)


def _mix_tail(x, att_bf, cb, yc, d_groups, gate_logits, wao_ref, wco_ref, wp_ref, ps_ref, wo_ref):
    att_o = _dot(att_bf, wao_ref[...])
    merged = _sigmoid(gate_logits[0]) * att_o
    conv_o = _dot((cb * yc).astype(BF16), wco_ref[...])
    merged = merged + _sigmoid(gate_logits[1]) * conv_o
    pool_o = jnp.concatenate([_dot(d.astype(BF16), wp_ref[g]) for g, d in enumerate(d_groups)], axis=-1)
    pool_o = pool_o * ps_ref[...]
    merged = merged + _sigmoid(gate_logits[2]) * pool_o
    return x + _dot(merged.astype(BF16), wo_ref[...])


def _prompt_mix_kernel(*refs, n_alias):
    (sinks_ref, x_ref, nw_ref, win_ref, wao_ref, cw_ref, wco_ref, wp_ref, ps_ref, wo_ref) = refs[:10]
    (h_ref, kw_ref, vw_ref, cs_ref, pst_ref,
     qbuf, kbuf, vbuf, att_ref, ubuf, pbuf, bias_ref) = refs[10 + n_alias:]
    tm = PROMPT_TILE
    b = pl.program_id(0)
    t = pl.program_id(1)
    x = x_ref[...]
    nb = _rms(x, nw_ref[...]).astype(BF16)
    lane = lax.broadcasted_iota(jnp.int32, (tm, D_KV), 1)
    low = lane < HEAD_DIM

    @pl.when(jnp.logical_and(b == 0, t == 0))
    def _():
        qi = lax.broadcasted_iota(jnp.int32, (WINDOW, 2 * WINDOW), 0)
        si = lax.broadcasted_iota(jnp.int32, (WINDOW, 2 * WINDOW), 1)
        dist_i = qi + WINDOW - si
        dist = dist_i.astype(F32)
        band = jnp.logical_and(dist_i >= 0, dist_i <= WINDOW)
        for h in range(N_Q_HEADS):
            bias = jnp.where(band, (-_head_slope(h) * LOG2E) * dist, -jnp.inf)
            bias_ref[0, h] = bias
            bias_ref[1, h] = jnp.where(si >= WINDOW, bias, -jnp.inf)

    @pl.when(t == 0)
    def _():
        for i in range(4):
            kbuf[i, 0:WINDOW, :] = jnp.zeros((WINDOW, D_KV), BF16)
            vbuf[i, 0:WINDOW, :] = jnp.zeros((WINDOW, D_KV), BF16)
        ubuf[0:8, :] = jnp.zeros((8, D_CONV), F32)
        pbuf[0:16, :] = jnp.zeros((16, D_POOL), F32)

    @pl.when(t > 0)
    def _():
        for i in range(4):
            kbuf[i, 0:WINDOW, :] = kbuf[i, tm:tm + WINDOW, :]
            vbuf[i, 0:WINDOW, :] = vbuf[i, tm:tm + WINDOW, :]
        ubuf[0:8, :] = ubuf[tm:tm + 8, :]
        pbuf[0:16, :] = pbuf[tm:tm + 16, :]

    qkv = _dot(nb, win_ref[:, OFF_Q:OFF_CX])
    qbuf[...] = (qkv[:, OFF_Q:OFF_K] * (ATTN_SCALE * LOG2E)).astype(BF16)
    k = qkv[:, OFF_K:OFF_V]
    v = qkv[:, OFF_V:OFF_CX]
    kw_ref[...] = k[tm - WINDOW:, :]
    vw_ref[...] = v[tm - WINDOW:, :]
    k_sw = pltpu.roll(k, HEAD_DIM, 1)
    v_sw = pltpu.roll(v, HEAD_DIM, 1)
    k_var = (jnp.where(low, k, 0.0), jnp.where(low, 0.0, k_sw), jnp.where(low, k_sw, 0.0), jnp.where(low, 0.0, k))
    v_var = (jnp.where(low, v, 1.0), jnp.where(low, 1.0, v_sw), jnp.where(low, v_sw, 1.0), jnp.where(low, 1.0, v))
    for i in range(4):
        kbuf[i, WINDOW:WINDOW + tm, :] = k_var[i].astype(BF16)
        vbuf[i, WINDOW:WINDOW + tm, :] = v_var[i].astype(BF16)

    lane_q = lax.broadcasted_iota(jnp.int32, (WINDOW, 2 * HEAD_DIM), 1)
    low_q = lane_q < HEAD_DIM

    seq_start = jnp.where(t == 0, 1, 0)
    z_parts = []
    for j in range(tm // WINDOW):
        z_parts.append(_z_part(nb, win_ref, j))
        r0 = j * WINDOW
        first = seq_start if j == 0 else 0
        for pair in range(N_Q_HEADS // 2):
            qp = qbuf[r0:r0 + WINDOW, pair * 2 * HEAD_DIM:(pair + 1) * 2 * HEAD_DIM]
            halves = []
            for half in range(2):
                h = 2 * pair + half
                var = 2 * (h // GQA_GROUP) + half
                s = _dot_nt(qp, kbuf[var, r0:r0 + 2 * WINDOW, :]) + bias_ref[first, h]
                sink = sinks_ref[h] * LOG2E
                m = jnp.maximum(jnp.max(s, axis=-1, keepdims=True), sink)
                p = jnp.exp2(s - m).astype(BF16)
                o = _dot(p, vbuf[var, r0:r0 + 2 * WINDOW, :])
                den = (o[:, HEAD_DIM:HEAD_DIM + 1] if half == 0 else o[:, 0:1]) + jnp.exp2(sink - m)
                halves.append(o / den)
            att_ref[r0:r0 + WINDOW, pair * 2 * HEAD_DIM:(pair + 1) * 2 * HEAD_DIM] = (
                jnp.where(low_q, halves[0], halves[1]).astype(BF16))

    (conv,), (pool_in, gate_att), (gate_conv,), (gate_pool,) = z_parts
    cx = conv[:, 0:D_CONV]
    cb = conv[:, D_CONV:2 * D_CONV]
    ubuf[8:8 + tm, :] = conv[:, 2 * D_CONV:3 * D_CONV] * cx
    yc = (cw_ref[0:1, :] * ubuf[6:6 + tm, :] + cw_ref[1:2, :] * ubuf[7:7 + tm, :]
          + cw_ref[2:3, :] * ubuf[8:8 + tm, :])
    cs_ref[...] = ubuf[tm + 6:tm + 8, :]

    pbuf[16:16 + tm, :] = pool_in
    pos = lax.broadcasted_iota(jnp.int32, (tm, 1), 0) + t * tm + 1
    d_groups = []
    for g, w in enumerate(POOL_WINDOWS):
        c0 = g * POOL_GROUP
        cur = pbuf[16:16 + tm, c0:c0 + POOL_GROUP]
        tot = cur
        for j in range(1, w):
            tot = tot + pbuf[16 - j:16 - j + tm, c0:c0 + POOL_GROUP]
        cnt = jnp.minimum(pos, w).astype(F32)
        d_groups.append(tot / cnt - cur)
    pst_ref[...] = pbuf[tm + 1:tm + 16, :]

    h_ref[...] = _mix_tail(x, att_ref[...], cb, yc, d_groups, (gate_att, gate_conv, gate_pool),
                           wao_ref, wco_ref, wp_ref, ps_ref, wo_ref)


def _weight_specs(l, grid_rank):
    def const(*idx):
        if grid_rank == 1:
            return lambda i: idx
        return lambda b, t: idx

    return [
        pl.BlockSpec((None, 1, D_MODEL), const(l, 0, 0)),
        pl.BlockSpec((None, D_MODEL, D_IN), const(l, 0, 0)),
        pl.BlockSpec((None, D_ATTN, D_MODEL), const(l, 0, 0)),
        pl.BlockSpec((None, CONV_WIDTH, D_CONV), const(l, 0, 0)),
        pl.BlockSpec((None, D_CONV, D_MODEL), const(l, 0, 0)),
        pl.BlockSpec((None, len(POOL_WINDOWS), POOL_GROUP, POOL_OUT_GROUP), const(l, 0, 0, 0)),
        pl.BlockSpec((None, 1, D_MODEL), const(l, 0, 0)),
        pl.BlockSpec((None, D_MODEL, D_MODEL), const(l, 0, 0)),
    ]


def _prompt_mix(l, x, sinks, weights, prev_states):
    tm = PROMPT_TILE
    nt = SEQ // tm
    n_alias = len(prev_states)
    in_specs = ([pl.BlockSpec(memory_space=pltpu.SMEM),
                 pl.BlockSpec((tm, D_MODEL), lambda b, t: (b * nt + t, 0))] + _weight_specs(l, 2)
                + [pl.BlockSpec(memory_space=pl.ANY)] * n_alias)
    out_specs = [
        pl.BlockSpec((tm, D_MODEL), lambda b, t: (b * nt + t, 0)),
        pl.BlockSpec((None, None, WINDOW, D_KV), lambda b, t: (l, b, 0, 0)),
        pl.BlockSpec((None, None, WINDOW, D_KV), lambda b, t: (l, b, 0, 0)),
        pl.BlockSpec((None, None, CONV_WIDTH - 1, D_CONV), lambda b, t: (l, b, 0, 0)),
        pl.BlockSpec((None, None, POOL_BUF, D_POOL), lambda b, t: (l, b, 0, 0)),
    ]
    out_shape = [
        jax.ShapeDtypeStruct((N_PAD, D_MODEL), F32),
        jax.ShapeDtypeStruct((DEPTH, BATCH, WINDOW, D_KV), F32),
        jax.ShapeDtypeStruct((DEPTH, BATCH, WINDOW, D_KV), F32),
        jax.ShapeDtypeStruct((DEPTH, BATCH, CONV_WIDTH - 1, D_CONV), F32),
        jax.ShapeDtypeStruct((DEPTH, BATCH, POOL_BUF, D_POOL), F32),
    ]
    scratch = [
        pltpu.VMEM((tm, D_ATTN), BF16),
        pltpu.VMEM((4, WINDOW + tm, D_KV), BF16),
        pltpu.VMEM((4, WINDOW + tm, D_KV), BF16),
        pltpu.VMEM((tm, D_ATTN), BF16),
        pltpu.VMEM((8 + tm, D_CONV), F32),
        pltpu.VMEM((16 + tm, D_POOL), F32),
        pltpu.VMEM((2, N_Q_HEADS, WINDOW, 2 * WINDOW), F32),
    ]
    outs = pl.pallas_call(
        functools.partial(_prompt_mix_kernel, n_alias=n_alias),
        grid=(BATCH, nt),
        in_specs=in_specs,
        out_specs=out_specs,
        out_shape=out_shape,
        scratch_shapes=scratch,
        input_output_aliases={10 + i: 1 + i for i in range(n_alias)},
        compiler_params=pltpu.CompilerParams(
            dimension_semantics=("arbitrary", "arbitrary"), vmem_limit_bytes=VMEM_LIMIT),
        name=f"prompt_mix_l{l}",
    )(sinks, x, *weights, *prev_states)
    return outs[0], tuple(outs[1:])


def _sample_mix_kernel(*refs, n_alias):
    (sinks_ref, x_ref, kc_ref, vc_ref, cst_ref, pin_ref,
     nw_ref, win_ref, wao_ref, cw_ref, wco_ref, wp_ref, ps_ref, wo_ref) = refs[:14]
    (h_ref, ko_ref, vo_ref, co_ref, po_ref,
     qbuf, knb, vnb, att_ref, cbuf, ebuf, bias_c, bias_n) = refs[14 + n_alias:]
    ns, nr, T = SAMPLE_SEQS, SAMPLE_ROWS, DEC_SEQ
    x = x_ref[...]
    nb = _rms(x, nw_ref[...]).astype(BF16)

    qkv = _dot(nb, win_ref[:, OFF_Q:OFF_CX])
    qbuf[...] = (qkv[:, OFF_Q:OFF_K] * ATTN_SCALE).astype(BF16)
    k = qkv[:, OFF_K:OFF_V]
    v = qkv[:, OFF_V:OFF_CX]
    knb[...] = k.astype(BF16)
    vnb[...] = v.astype(BF16)
    ko_ref[:, 0:WINDOW - T, :] = kc_ref[:, T:WINDOW, :]
    vo_ref[:, 0:WINDOW - T, :] = vc_ref[:, T:WINDOW, :]
    ko_ref[:, WINDOW - T:WINDOW, :] = k.reshape(ns, T, D_KV)
    vo_ref[:, WINDOW - T:WINDOW, :] = v.reshape(ns, T, D_KV)

    gr = ATT_GROUP * T
    gc = ATT_GROUP * WINDOW

    @pl.when(pl.program_id(0) == 0)
    def _():
        rq = lax.broadcasted_iota(jnp.int32, (gr, gc), 0)
        cq = lax.broadcasted_iota(jnp.int32, (gr, gc), 1)
        tq = rq & (T - 1)
        sc_pos = cq & (WINDOW - 1)
        valid_c = jnp.logical_and((rq >> 3) == (cq >> 7), sc_pos >= tq)
        dist_c = (WINDOW + tq - sc_pos).astype(F32)
        rn = lax.broadcasted_iota(jnp.int32, (gr, gr), 0)
        cn = lax.broadcasted_iota(jnp.int32, (gr, gr), 1)
        tn = rn & (T - 1)
        jn = cn & (T - 1)
        valid_n = jnp.logical_and((rn >> 3) == (cn >> 3), jn <= tn)
        dist_n = (tn - jn).astype(F32)
        for h in range(N_Q_HEADS):
            bias_c[h] = jnp.where(valid_c, -_head_slope(h) * dist_c, -jnp.inf)
            bias_n[h] = jnp.where(valid_n, -_head_slope(h) * dist_n, -jnp.inf)

    z_parts = []
    for gi in range(ns // ATT_GROUP):
        z_parts.append(_z_part(nb, win_ref, gi))
        r0 = gi * gr
        s0 = gi * ATT_GROUP
        qg = qbuf[r0:r0 + gr, :]
        kcg = kc_ref[s0:s0 + ATT_GROUP, :, :].reshape(gc, D_KV).astype(BF16)
        vcg = vc_ref[s0:s0 + ATT_GROUP, :, :].reshape(gc, D_KV).astype(BF16)
        kng = knb[r0:r0 + gr, :]
        vng = vnb[r0:r0 + gr, :]
        outs = []
        for h in range(N_Q_HEADS):
            kv = h // GQA_GROUP
            lo, hi = kv * HEAD_DIM, (kv + 1) * HEAD_DIM
            qh = qg[:, h * HEAD_DIM:(h + 1) * HEAD_DIM]
            s_c = _dot_nt(qh, kcg[:, lo:hi]) + bias_c[h]
            s_n = _dot_nt(qh, kng[:, lo:hi]) + bias_n[h]
            outs.append(_softmax_pv([(s_c, vcg[:, lo:hi]), (s_n, vng[:, lo:hi])], sinks_ref[h]))
        att_ref[r0:r0 + gr, :] = jnp.concatenate(outs, axis=-1).astype(BF16)

    (conv,), (pool_in, gate_att), (gate_conv,), (gate_pool,) = z_parts
    cx = conv[:, 0:D_CONV]
    cb = conv[:, D_CONV:2 * D_CONV]
    u = conv[:, 2 * D_CONV:3 * D_CONV] * cx
    cbuf[:, 6:8, :] = cst_ref[...]
    cbuf[:, 8:16, :] = u.reshape(ns, T, D_CONV)
    w0 = cw_ref[0:1, :].reshape(1, 1, D_CONV)
    w1 = cw_ref[1:2, :].reshape(1, 1, D_CONV)
    w2 = cw_ref[2:3, :].reshape(1, 1, D_CONV)
    yc = (w0 * cbuf[:, 6:14, :] + w1 * cbuf[:, 7:15, :] + w2 * cbuf[:, 8:16, :]).reshape(nr, D_CONV)
    co_ref[...] = cbuf[:, 14:16, :]

    ebuf[:, 1:16, :] = pin_ref[...]
    ebuf[:, 16:24, :] = pool_in.reshape(ns, T, D_POOL)
    d_groups = []
    for g, w in enumerate(POOL_WINDOWS):
        c0 = g * POOL_GROUP
        cur = ebuf[:, 16:24, c0:c0 + POOL_GROUP]
        tot = cur
        for j in range(1, w):
            tot = tot + ebuf[:, 16 - j:24 - j, c0:c0 + POOL_GROUP]
        d_groups.append((tot / float(w) - cur).reshape(nr, POOL_GROUP))
    po_ref[...] = ebuf[:, 9:24, :]

    h_ref[...] = _mix_tail(x, att_ref[...], cb, yc, d_groups, (gate_att, gate_conv, gate_pool),
                           wao_ref, wco_ref, wp_ref, ps_ref, wo_ref)


def _sample_mix(l, x, x_row0, sinks, kc, vc, cst, pst, weights, h_buf, prev_states):
    ns, nr = SAMPLE_SEQS, SAMPLE_ROWS
    n_alias = 1 + len(prev_states)
    xb0 = x_row0 // nr
    hb0 = N_PROMPT // nr
    in_specs = [
        pl.BlockSpec(memory_space=pltpu.SMEM),
        pl.BlockSpec((nr, D_MODEL), lambda i: (xb0 + i, 0)),
        pl.BlockSpec((None, ns, WINDOW, D_KV), lambda i: (l, i, 0, 0)),
        pl.BlockSpec((None, ns, WINDOW, D_KV), lambda i: (l, i, 0, 0)),
        pl.BlockSpec((None, ns, CONV_WIDTH - 1, D_CONV), lambda i: (l, i, 0, 0)),
        pl.BlockSpec((None, ns, POOL_BUF, D_POOL), lambda i: (l, i, 0, 0)),
    ] + _weight_specs(l, 1) + [pl.BlockSpec(memory_space=pl.ANY)] * n_alias
    out_specs = [
        pl.BlockSpec((nr, D_MODEL), lambda i: (hb0 + i, 0)),
        pl.BlockSpec((None, ns, WINDOW, D_KV), lambda i: (l, i, 0, 0)),
        pl.BlockSpec((None, ns, WINDOW, D_KV), lambda i: (l, i, 0, 0)),
        pl.BlockSpec((None, ns, CONV_WIDTH - 1, D_CONV), lambda i: (l, i, 0, 0)),
        pl.BlockSpec((None, ns, POOL_BUF, D_POOL), lambda i: (l, i, 0, 0)),
    ]
    out_shape = [
        jax.ShapeDtypeStruct((N_PAD, D_MODEL), F32),
        jax.ShapeDtypeStruct((DEPTH, DEC_BATCH, WINDOW, D_KV), F32),
        jax.ShapeDtypeStruct((DEPTH, DEC_BATCH, WINDOW, D_KV), F32),
        jax.ShapeDtypeStruct((DEPTH, DEC_BATCH, CONV_WIDTH - 1, D_CONV), F32),
        jax.ShapeDtypeStruct((DEPTH, DEC_BATCH, POOL_BUF, D_POOL), F32),
    ]
    scratch = [
        pltpu.VMEM((nr, D_ATTN), BF16),
        pltpu.VMEM((nr, D_KV), BF16),
        pltpu.VMEM((nr, D_KV), BF16),
        pltpu.VMEM((nr, D_ATTN), BF16),
        pltpu.VMEM((ns, 16, D_CONV), F32),
        pltpu.VMEM((ns, 24, D_POOL), F32),
        pltpu.VMEM((N_Q_HEADS, ATT_GROUP * DEC_SEQ, ATT_GROUP * WINDOW), F32),
        pltpu.VMEM((N_Q_HEADS, ATT_GROUP * DEC_SEQ, ATT_GROUP * DEC_SEQ), F32),
    ]
    outs = pl.pallas_call(
        functools.partial(_sample_mix_kernel, n_alias=n_alias),
        grid=(DEC_BATCH // ns,),
        in_specs=in_specs,
        out_specs=out_specs,
        out_shape=out_shape,
        scratch_shapes=scratch,
        input_output_aliases={14 + i: i for i in range(n_alias)},
        compiler_params=pltpu.CompilerParams(
            dimension_semantics=("arbitrary",), vmem_limit_bytes=VMEM_LIMIT),
        name=f"sample_mix_l{l}",
    )(sinks, x, kc, vc, cst, pst, *weights, h_buf, *prev_states)
    return outs[0], tuple(outs[1:])


def _ffn_kernel(h_ref, nw_ref, wg_ref, wu_ref, wd_ref, eg_ref, eu_ref, ed_ref, o_ref, egb_ref, eub_ref, edb_ref):
    h = h_ref[...]
    hn = _rms(h, nw_ref[...]).astype(BF16)
    g = _dot(hn, wg_ref[...])
    u = _dot(hn, wu_ref[...])
    o_ref[...] = h + _dot((g * _sigmoid(g) * u).astype(BF16), wd_ref[...])
    egb_ref[...] = eg_ref[...].astype(BF16)
    eub_ref[...] = eu_ref[...].astype(BF16)
    edb_ref[...] = ed_ref[...].astype(BF16)


def _ffn_dense(h, nw, wg, wu, wd, eg, eu, ed):
    tm = SUB_ROWS
    steps = N_TOK // tm
    n_cast = CAST_STEPS
    assert n_cast <= steps
    gr = eg.shape[0] // n_cast
    dr = ed.shape[0] // n_cast
    assert gr * n_cast == eg.shape[0] and dr * n_cast == ed.shape[0] and gr % 16 == 0 and dr % 16 == 0
    resident = dict(pipeline_mode=pl.Buffered(1))

    def cast_map(i):
        return (jnp.minimum(i, n_cast - 1), 0)

    return pl.pallas_call(
        _ffn_kernel,
        grid=(steps,),
        in_specs=[
            pl.BlockSpec((tm, D_MODEL), lambda i: (i, 0)),
            pl.BlockSpec((1, D_MODEL), lambda i: (0, 0)),
            pl.BlockSpec((None, D_MODEL, D_FF), lambda i: (0, 0, 0), **resident),
            pl.BlockSpec((None, D_MODEL, D_FF), lambda i: (0, 0, 0), **resident),
            pl.BlockSpec((None, D_FF, D_MODEL), lambda i: (0, 0, 0), **resident),
            pl.BlockSpec((gr, D_EXPERT), cast_map),
            pl.BlockSpec((gr, D_EXPERT), cast_map),
            pl.BlockSpec((dr, D_MODEL), cast_map),
        ],
        out_specs=[
            pl.BlockSpec((tm, D_MODEL), lambda i: (i, 0)),
            pl.BlockSpec((gr, D_EXPERT), cast_map),
            pl.BlockSpec((gr, D_EXPERT), cast_map),
            pl.BlockSpec((dr, D_MODEL), cast_map),
        ],
        out_shape=[
            jax.ShapeDtypeStruct((N_PAD, D_MODEL), F32),
            jax.ShapeDtypeStruct(eg.shape, BF16),
            jax.ShapeDtypeStruct(eu.shape, BF16),
            jax.ShapeDtypeStruct(ed.shape, BF16),
        ],
        compiler_params=pltpu.CompilerParams(
            dimension_semantics=("arbitrary",), vmem_limit_bytes=VMEM_LIMIT),
        name="ffn_dense",
    )(h, nw, wg, wu, wd, eg, eu, ed)


def _router_kernel(h_ref, nw_ref, rt_ref, hn_ref, slot_ref, col_ref, cnt_ref):
    tm = FFN_TILE
    hn = _rms(h_ref[...], nw_ref[...])
    hn_ref[...] = hn.astype(BF16)
    logits = lax.dot_general(rt_ref[...], hn, (((1,), (1,)), ((), ())),
                             precision=lax.Precision.HIGHEST, preferred_element_type=F32)
    eidx = lax.broadcasted_iota(jnp.int32, (N_EXPERTS, tm), 0).astype(F32)
    none = float(N_EXPERTS)
    m1 = jnp.max(logits, axis=0, keepdims=True)
    i1 = jnp.min(jnp.where(logits == m1, eidx, none), axis=0, keepdims=True)
    rest = jnp.where(eidx == i1, -jnp.inf, logits)
    m2 = jnp.max(rest, axis=0, keepdims=True)
    i2 = jnp.min(jnp.where(rest == m2, eidx, none), axis=0, keepdims=True)
    e2 = jnp.exp(m2 - m1)
    w1 = 1.0 / (1.0 + e2)
    w2 = e2 / (1.0 + e2)
    sel1 = eidx == i1
    sel2 = eidx == i2
    gate = jnp.where(sel1, w1, jnp.where(sel2, w2, 0.0))
    chosen = jnp.logical_or(sel1, sel2)
    mask = jnp.where(chosen, 1.0, 0.0)
    srow = lax.broadcasted_iota(jnp.int32, (tm, tm), 0)
    scol = lax.broadcasted_iota(jnp.int32, (tm, tm), 1)
    upper = jnp.where(srow < scol, 1.0, 0.0).astype(BF16)
    mask16 = jnp.concatenate([mask, jnp.zeros_like(mask)], axis=0).astype(BF16)
    slot = jnp.where(chosen, _dot(mask16, upper)[0:N_EXPERTS, :], -1.0)
    slot_ref[...] = slot.astype(jnp.int32)
    cnt_ref[...] = jnp.broadcast_to(jnp.sum(mask, axis=1, keepdims=True), (N_EXPERTS, 128))
    both = jnp.concatenate([slot, gate, jnp.zeros((128 - 2 * N_EXPERTS, tm), F32)], axis=0)
    col_ref[...] = both.T[:, 0:2 * N_EXPERTS]


def _router(h, nw, router_t):
    tm = FFN_TILE
    nt = N_UNITS
    return pl.pallas_call(
        _router_kernel,
        grid=(nt,),
        in_specs=[
            pl.BlockSpec((tm, D_MODEL), lambda i: (i, 0)),
            pl.BlockSpec((1, D_MODEL), lambda i: (0, 0)),
            pl.BlockSpec((N_EXPERTS, D_MODEL), lambda i: (0, 0)),
        ],
        out_specs=[
            pl.BlockSpec((tm, D_MODEL), lambda i: (i, 0)),
            pl.BlockSpec((N_EXPERTS, tm), lambda i: (0, i)),
            pl.BlockSpec((tm, 2 * N_EXPERTS), lambda i: (i, 0)),
            pl.BlockSpec((None, N_EXPERTS, 128), lambda i: (i, 0, 0)),
        ],
        out_shape=[
            jax.ShapeDtypeStruct((N_PAD, D_MODEL), BF16),
            jax.ShapeDtypeStruct((N_EXPERTS, N_PAD), jnp.int32),
            jax.ShapeDtypeStruct((N_PAD, 2 * N_EXPERTS), F32),
            jax.ShapeDtypeStruct((nt, N_EXPERTS, 128), F32),
        ],
        compiler_params=pltpu.CompilerParams(
            dimension_semantics=("arbitrary",), vmem_limit_bytes=VMEM_LIMIT),
        name="moe_router",
    )(h, nw, router_t)


def _plan(cnt):
    seg = -(-cnt // ROW_ALIGN) * ROW_ALIGN
    rows_e = jnp.sum(seg, axis=0)
    tiles_e = -(-rows_e // EXP_TILE)
    cum_tiles = jnp.cumsum(tiles_e)
    base_e = (cum_tiles - tiles_e) * EXP_TILE
    dst = base_e[None, :] + jnp.cumsum(seg, axis=0) - seg
    tail = jnp.concatenate([base_e + rows_e, tiles_e * EXP_TILE - rows_e]).astype(jnp.int32)
    n_tiles = cum_tiles[-1]
    i = jnp.arange(MAX_TILES, dtype=jnp.int32)
    tile_expert = jnp.sum((cum_tiles[None, :] <= i[:, None]).astype(jnp.int32), axis=1)
    tile_expert = jnp.minimum(tile_expert, N_EXPERTS - 1)
    tile_expert = jnp.where(i < n_tiles, tile_expert, tile_expert[jnp.maximum(n_tiles - 1, 0)])
    return dst.reshape(-1).astype(jnp.int32), tail, tile_expert, n_tiles.reshape(1).astype(jnp.int32)


def _ceil_blocks(n, block):
    return (n + block - 1) >> (block.bit_length() - 1)


def _segment_copies(n_rows, src_at, dst_at, sem, max_bits=SEG_BITS):
    n = n_rows >> (ROW_ALIGN.bit_length() - 1)
    out = []
    for bit in range(max_bits - 1, -1, -1):
        size = ROW_ALIGN << bit
        off = pl.multiple_of(((n >> (bit + 1)) << (bit + 1)) * ROW_ALIGN, ROW_ALIGN)
        out.append((((n >> bit) & 1) == 1, pltpu.make_async_copy(src_at(off, size), dst_at(off, size), sem)))
    return out


def _start_all(copies):
    for pred, cp in copies:
        pl.when(pred)(cp.start)


def _wait_all(copies):
    for pred, cp in copies:
        pl.when(pred)(cp.wait)


def _dispatch_kernel(cnt_ref, dst_ref, tail_ref, hn_ref, srow_ref, xs_ref, stage, sems):
    tu, rb = FFN_TILE, ROW_BLOCK
    t = pl.program_id(0)

    def copies(e):
        slot = e % 2
        n_rows = _ceil_blocks(cnt_ref[t * N_EXPERTS + e], ROW_ALIGN) * ROW_ALIGN
        d0 = dst_ref[t * N_EXPERTS + e]
        return _segment_copies(
            n_rows,
            lambda off, size: stage.at[slot, pl.ds(off, size)],
            lambda off, size: xs_ref.at[pl.ds(pl.multiple_of(d0 + off, ROW_ALIGN), size)],
            sems.at[slot])

    for e in range(N_EXPERTS):
        slot = e % 2
        if e >= 2:
            _wait_all(copies(e - 2))
        slot_row = srow_ref[e:e + 1, :]

        def gather(b, carry, slot=slot, slot_row=slot_row):
            r0 = pl.multiple_of(b * rb, rb)
            rid = lax.broadcasted_iota(jnp.int32, (rb, tu), 0) + b * rb
            onehot = jnp.where(slot_row == rid, 1.0, 0.0).astype(BF16)
            stage[slot, pl.ds(r0, rb), :] = _dot(onehot, hn_ref[...]).astype(BF16)
            return carry

        lax.fori_loop(0, _ceil_blocks(cnt_ref[t * N_EXPERTS + e], rb), gather, 0)
        _start_all(copies(e))

    _wait_all(copies(N_EXPERTS - 2))
    _wait_all(copies(N_EXPERTS - 1))

    @pl.when(t == pl.num_programs(0) - 1)
    def _():
        stage[0, 0:EXP_TILE, :] = jnp.zeros((EXP_TILE, D_MODEL), BF16)

        def tail_copies(e):
            d0 = tail_ref[e]
            return _segment_copies(
                tail_ref[N_EXPERTS + e],
                lambda off, size: stage.at[0, pl.ds(0, size)],
                lambda off, size: xs_ref.at[pl.ds(pl.multiple_of(d0 + off, ROW_ALIGN), size)],
                sems.at[e % 2], max_bits=EXP_TILE.bit_length() - ROW_ALIGN.bit_length())

        for e in range(N_EXPERTS):
            _start_all(tail_copies(e))
        for e in range(N_EXPERTS):
            _wait_all(tail_copies(e))


def _dispatch(cnt, dst, tail, hn, slot_row):
    tm = FFN_TILE
    n_rows = MAX_TILES * EXP_TILE
    grid_spec = pltpu.PrefetchScalarGridSpec(
        num_scalar_prefetch=3,
        grid=(N_UNITS,),
        in_specs=[
            pl.BlockSpec((tm, D_MODEL), lambda t, c, d, z: (t, 0)),
            pl.BlockSpec((N_EXPERTS, tm), lambda t, c, d, z: (0, t)),
        ],
        out_specs=pl.BlockSpec(memory_space=pl.ANY),
        scratch_shapes=[pltpu.VMEM((2, tm, D_MODEL), BF16), pltpu.SemaphoreType.DMA((2,))],
    )
    return pl.pallas_call(
        _dispatch_kernel,
        grid_spec=grid_spec,
        out_shape=jax.ShapeDtypeStruct((n_rows, D_MODEL), BF16),
        compiler_params=pltpu.CompilerParams(
            dimension_semantics=("arbitrary",), vmem_limit_bytes=VMEM_LIMIT),
        name="moe_dispatch",
    )(cnt, dst, tail, hn, slot_row)


def _experts_kernel(texp_ref, ntile_ref, xs_ref, wg_ref, wu_ref, wd_ref, ys_ref, acc_ref):
    del texp_ref
    i, c = pl.program_id(0), pl.program_id(1)
    nc = pl.num_programs(1)

    @pl.when(i < ntile_ref[0])
    def _():
        @pl.when(c == 0)
        def _():
            acc_ref[...] = jnp.zeros_like(acc_ref)

        for r in range(0, EXP_TILE, SUB_ROWS):
            xb = xs_ref[r:r + SUB_ROWS, :]
            g = _dot(xb, wg_ref[...])
            u = _dot(xb, wu_ref[...])
            acc_ref[r:r + SUB_ROWS, :] += _dot((g * _sigmoid(g) * u).astype(BF16), wd_ref[...])

        @pl.when(c == nc - 1)
        def _():
            ys_ref[...] = acc_ref[...].astype(BF16)


def _experts(tile_expert, n_tiles, xs, wg, wu, wd):
    tm = EXP_TILE
    nc = D_EXPERT // EXP_CHUNK

    def row_map(i, c, te, nt):
        return (jnp.minimum(i, jnp.maximum(nt[0] - 1, 0)), 0)

    def chunk(i, c, nt):
        return jnp.where(i < nt[0], c, nc - 1)

    grid_spec = pltpu.PrefetchScalarGridSpec(
        num_scalar_prefetch=2,
        grid=(MAX_TILES, nc),
        in_specs=[
            pl.BlockSpec((tm, D_MODEL), row_map),
            pl.BlockSpec((None, D_MODEL, EXP_CHUNK), lambda i, c, te, nt: (te[i], 0, chunk(i, c, nt))),
            pl.BlockSpec((None, D_MODEL, EXP_CHUNK), lambda i, c, te, nt: (te[i], 0, chunk(i, c, nt))),
            pl.BlockSpec((None, EXP_CHUNK, D_MODEL), lambda i, c, te, nt: (te[i], chunk(i, c, nt), 0)),
        ],
        out_specs=pl.BlockSpec((tm, D_MODEL), row_map),
        scratch_shapes=[pltpu.VMEM((tm, D_MODEL), F32)],
    )
    return pl.pallas_call(
        _experts_kernel,
        grid_spec=grid_spec,
        out_shape=jax.ShapeDtypeStruct((MAX_TILES * tm, D_MODEL), BF16),
        compiler_params=pltpu.CompilerParams(
            dimension_semantics=("arbitrary", "arbitrary"), vmem_limit_bytes=VMEM_LIMIT),
        name="moe_experts",
    )(tile_expert, n_tiles, xs, wg, wu, wd)


def _combine_kernel(cnt_ref, dst_ref, ys_ref, col_ref, h_ref, nf_ref, op_ref, os_ref, stage, sems, acc_ref):
    tu, rb = FFN_TILE, 2 * ROW_BLOCK
    t = pl.program_id(0)

    @pl.when(t == 0)
    def _():
        stage[...] = jnp.zeros_like(stage)

    def copies(e):
        n_rows = _ceil_blocks(cnt_ref[t * N_EXPERTS + e], ROW_ALIGN) * ROW_ALIGN
        d0 = dst_ref[t * N_EXPERTS + e]
        return _segment_copies(
            n_rows,
            lambda off, size: ys_ref.at[pl.ds(pl.multiple_of(d0 + off, ROW_ALIGN), size)],
            lambda off, size: stage.at[e, pl.ds(off, size)],
            sems.at[e])

    for e in range(N_EXPERTS):
        _start_all(copies(e))
    acc_ref[...] = h_ref[...]
    for e in range(N_EXPERTS):
        _wait_all(copies(e))
        slot_col = col_ref[:, e:e + 1]
        gate_col = col_ref[:, e + N_EXPERTS:e + N_EXPERTS + 1]

        def scatter(b, carry, e=e, slot_col=slot_col, gate_col=gate_col):
            r0 = pl.multiple_of(b * rb, rb)
            cid = (lax.broadcasted_iota(jnp.int32, (tu, rb), 1) + b * rb).astype(F32)
            weighted = jnp.where(slot_col == cid, gate_col, 0.0).astype(BF16)
            acc_ref[...] += _dot(weighted, stage[e, pl.ds(r0, rb), :])
            return carry

        lax.fori_loop(0, _ceil_blocks(cnt_ref[t * N_EXPERTS + e], rb), scatter, 0)

    out = _rms(acc_ref[...], nf_ref[...])

    @pl.when(t < N_PROMPT // FFN_TILE)
    def _():
        op_ref[...] = out

    @pl.when(t >= N_PROMPT // FFN_TILE)
    def _():
        os_ref[...] = out


def _combine(cnt, dst, ys, cols, h, nf):
    tm = FFN_TILE
    np_tiles = N_PROMPT // tm
    assert N_SAMPLE == tm
    grid_spec = pltpu.PrefetchScalarGridSpec(
        num_scalar_prefetch=2,
        grid=(N_UNITS,),
        in_specs=[
            pl.BlockSpec(memory_space=pl.ANY),
            pl.BlockSpec((tm, 2 * N_EXPERTS), lambda t, c, d: (t, 0)),
            pl.BlockSpec((tm, D_MODEL), lambda t, c, d: (t, 0)),
            pl.BlockSpec((1, D_MODEL), lambda t, c, d: (0, 0)),
        ],
        out_specs=[
            pl.BlockSpec((tm, D_MODEL), lambda t, c, d: (jnp.minimum(t, np_tiles - 1), 0)),
            pl.BlockSpec((tm, D_MODEL), lambda t, c, d: (0, 0)),
        ],
        scratch_shapes=[pltpu.VMEM((N_EXPERTS, tm, D_MODEL), BF16), pltpu.SemaphoreType.DMA((N_EXPERTS,)),
                        pltpu.VMEM((tm, D_MODEL), F32)],
    )
    return pl.pallas_call(
        _combine_kernel,
        grid_spec=grid_spec,
        out_shape=[
            jax.ShapeDtypeStruct((N_PROMPT, D_MODEL), F32),
            jax.ShapeDtypeStruct((N_SAMPLE, D_MODEL), F32),
        ],
        compiler_params=pltpu.CompilerParams(
            dimension_semantics=("arbitrary",), vmem_limit_bytes=VMEM_LIMIT),
        name="moe_combine",
    )(cnt, dst, ys, cols, h, nf)


def kernel(x_prompt, x_sample, cache_win_k, cache_win_v, state_conv, state_pool, norm_mix, w_in, attn_sinks,
           w_attn_out, conv_w, w_conv_out, w_pool, pool_scale, w_out, norm_ffn, ffn_w_gate, ffn_w_up, ffn_w_down,
           moe_router, moe_w_gate, moe_w_up, moe_w_down, norm_final):
    mix_weights = (norm_mix.reshape(DEPTH, 1, D_MODEL), w_in.astype(BF16), w_attn_out.astype(BF16), conv_w,
                   w_conv_out.astype(BF16), w_pool.astype(BF16), pool_scale.reshape(DEPTH, 1, D_MODEL),
                   w_out.astype(BF16))
    kc = cache_win_k.reshape(DEPTH, DEC_BATCH, WINDOW, D_KV)
    vc = cache_win_v.reshape(DEPTH, DEC_BATCH, WINDOW, D_KV)

    xp = x_prompt.reshape(N_PROMPT, D_MODEL)
    xs, xs_row0 = x_sample.reshape(N_SAMPLE, D_MODEL), 0
    p_states, s_states = (), ()
    for l in range(DEPTH):
        sinks = attn_sinks[l]
        h, p_states = _prompt_mix(l, xp, sinks, mix_weights, p_states)
        h, s_states = _sample_mix(l, xs, xs_row0, sinks, kc, vc, state_conv, state_pool, mix_weights, h, s_states)
        i = l // 2
        nw = norm_ffn[l].reshape(1, D_MODEL)
        if l % 2 == 0:
            assert l + 1 < DEPTH and moe_w_gate.shape[0] == 1
            xp, eg, eu, ed = _ffn_dense(
                h, nw, ffn_w_gate[i:i + 1].astype(BF16), ffn_w_up[i:i + 1].astype(BF16),
                ffn_w_down[i:i + 1].astype(BF16),
                moe_w_gate.reshape(N_EXPERTS * D_MODEL, D_EXPERT), moe_w_up.reshape(N_EXPERTS * D_MODEL, D_EXPERT),
                moe_w_down.reshape(N_EXPERTS * D_EXPERT, D_MODEL))
            xs, xs_row0 = xp, N_PROMPT
        else:
            hn, slot_row, cols, cnt = _router(h, nw, moe_router[i].T)
            cnt = cnt[:, :, 0].astype(jnp.int32)
            dst, tail, tile_expert, n_tiles = _plan(cnt)
            cnt = cnt.reshape(-1)
            xs_sorted = _dispatch(cnt, dst, tail, hn, slot_row)
            ys_sorted = _experts(tile_expert, n_tiles, xs_sorted,
                                 eg.reshape(N_EXPERTS, D_MODEL, D_EXPERT), eu.reshape(N_EXPERTS, D_MODEL, D_EXPERT),
                                 ed.reshape(N_EXPERTS, D_EXPERT, D_MODEL))
            y_prompt, y_sample = _combine(cnt, dst, ys_sorted, cols, h, norm_final.reshape(1, D_MODEL))

    pk, pv, pc, pp = p_states
    sk, sv, sc, sp = s_states
    kv_shape_p = (DEPTH, BATCH, WINDOW, N_KV_HEADS, HEAD_DIM)
    kv_shape_s = (DEPTH, DEC_BATCH, WINDOW, N_KV_HEADS, HEAD_DIM)
    return (y_prompt.reshape(BATCH, SEQ, D_MODEL), y_sample.reshape(DEC_BATCH, DEC_SEQ, D_MODEL),
            pk.reshape(kv_shape_p), pv.reshape(kv_shape_p), pc, pp,
            sk.reshape(kv_shape_s), sv.reshape(kv_shape_s), sc, sp)
```

```python
import functools
import math

import jax
import jax.numpy as jnp
from jax import lax
from jax.experimental import pallas as pl
from jax.experimental.pallas import tpu as pltpu

F32 = jnp.float32
BF16 = jnp.bfloat16

D_MODEL = 1024
BATCH = 4
SEQ = 4096
DEPTH = 2
DEC_BATCH = 128
DEC_SEQ = 8
HEAD_DIM = 64
N_Q_HEADS = 8
N_KV_HEADS = 2
GQA_GROUP = N_Q_HEADS // N_KV_HEADS
D_ATTN = N_Q_HEADS * HEAD_DIM
D_KV = N_KV_HEADS * HEAD_DIM
WINDOW = 128
ATTN_SCALE = HEAD_DIM ** -0.5
D_CONV = D_MODEL // 2
CONV_WIDTH = 3
D_POOL = D_MODEL // 2
POOL_WINDOWS = (2, 4, 8, 16)
POOL_GROUP = D_POOL // len(POOL_WINDOWS)
POOL_OUT_GROUP = D_MODEL // len(POOL_WINDOWS)
POOL_BUF = max(POOL_WINDOWS) - 1
OFF_Q = 0
OFF_K = OFF_Q + D_ATTN
OFF_V = OFF_K + D_KV
OFF_CX = OFF_V + D_KV
OFF_CB = OFF_CX + D_CONV
OFF_CC = OFF_CB + D_CONV
OFF_P = OFF_CC + D_CONV
OFF_G = OFF_P + D_POOL
D_IN = OFF_G + 3 * D_MODEL
D_FF = 2816
N_EXPERTS = 8
D_EXPERT = 3584
EPS = 1e-5
LOG2E = math.log2(math.e)

N_PROMPT = BATCH * SEQ
N_SAMPLE = DEC_BATCH * DEC_SEQ
N_TOK = N_PROMPT + N_SAMPLE

PROMPT_TILE = 512
SAMPLE_SEQS = 32
SAMPLE_ROWS = SAMPLE_SEQS * DEC_SEQ
ATT_GROUP = 8
FFN_TILE = 1024
EXP_CHUNK = D_EXPERT // 2
SUB_ROWS = 256
CAST_STEPS = 64
ROW_BLOCK = 128
HEAD_BLOCK = 256
ROW_ALIGN = 16
EXP_TILE = 512
N_UNITS = N_TOK // FFN_TILE
MAX_TILES = -(-(2 * N_TOK + N_UNITS * N_EXPERTS * (ROW_ALIGN - 1)) // EXP_TILE) + N_EXPERTS
N_PAD = N_TOK
SEG_BITS = 7
assert FFN_TILE == ROW_ALIGN << (SEG_BITS - 1)
VMEM_LIMIT = 58 * 1024 * 1024


def _dot(a, b):
    return jnp.dot(a, b, preferred_element_type=F32)


def _dot_nt(a, b):
    return lax.dot_general(a, b, (((1,), (1,)), ((), ())), preferred_element_type=F32)


def _rms(x, g):
    return x * lax.rsqrt(jnp.mean(x * x, axis=-1, keepdims=True) + EPS) * g


def _sigmoid(x):
    return 0.5 * jnp.tanh(0.5 * x) + 0.5


def _head_slope(h):
    return float(2.0 ** (-8.0 * (h + 1) / N_Q_HEADS))


def _softmax_pv(parts, sink):
    m = sink
    for s, _ in parts:
        m = jnp.maximum(m, jnp.max(s, axis=-1, keepdims=True))
    den = jnp.exp(sink - m)
    o = None
    for s, v in parts:
        p = jnp.exp(s - m)
        den = den + jnp.sum(p, axis=-1, keepdims=True)
        pv = _dot(p.astype(BF16), v)
        o = pv if o is None else o + pv
    return o / den


def _z_part(nb, win_ref, i):
    if i == 0:
        return (_dot(nb, win_ref[:, OFF_CX:OFF_P]),)
    if i == 1:
        return (_dot(nb, win_ref[:, OFF_P:OFF_G]), _dot(nb, win_ref[:, OFF_G:OFF_G + D_MODEL]))
    return (_dot(nb, win_ref[:, OFF_G + (i - 1) * D_MODEL:OFF_G + i * D_MODEL]),)


def _mix_tail(x, att_bf, cb, yc, d_groups, gate_logits, wao_ref, wco_ref, wp_ref, ps_ref, wo_ref):
    att_o = _dot(att_bf, wao_ref[...])
    merged = _sigmoid(gate_logits[0]) * att_o
    conv_o = _dot((cb * yc).astype(BF16), wco_ref[...])
    merged = merged + _sigmoid(gate_logits[1]) * conv_o
    pool_o = jnp.concatenate([_dot(d.astype(BF16), wp_ref[g]) for g, d in enumerate(d_groups)], axis=-1)
    pool_o = pool_o * ps_ref[...]
    merged = merged + _sigmoid(gate_logits[2]) * pool_o
    return x + _dot(merged.astype(BF16), wo_ref[...])


def _prompt_mix_kernel(*refs, n_alias, n_cast):
    (sinks_ref, x_ref, nw_ref, win_ref, wao_ref, cw_ref, wco_ref, wp_ref, ps_ref, wo_ref) = refs[:10]
    cast_src = refs[10 + n_alias:10 + n_alias + n_cast]
    outs = refs[10 + n_alias + n_cast:]
    h_ref, kw_ref, vw_ref, cs_ref, pst_ref = outs[:5]
    cast_dst = outs[5:5 + n_cast]
    qbuf, kbuf, vbuf, att_ref, ubuf, pbuf, bias_ref = outs[5 + n_cast:]
    _run_casts(cast_src, cast_dst)
    tm = PROMPT_TILE
    b = pl.program_id(0)
    t = pl.program_id(1)
    x = x_ref[...]
    nb = _rms(x, nw_ref[...]).astype(BF16)
    lane = lax.broadcasted_iota(jnp.int32, (tm, D_KV), 1)
    low = lane < HEAD_DIM

    @pl.when(jnp.logical_and(b == 0, t == 0))
    def _():
        qi = lax.broadcasted_iota(jnp.int32, (WINDOW, 2 * WINDOW), 0)
        si = lax.broadcasted_iota(jnp.int32, (WINDOW, 2 * WINDOW), 1)
        dist_i = qi + WINDOW - si
        dist = dist_i.astype(F32)
        band = jnp.logical_and(dist_i >= 0, dist_i <= WINDOW)
        for h in range(N_Q_HEADS):
            bias = jnp.where(band, (-_head_slope(h) * LOG2E) * dist, -jnp.inf)
            bias_ref[0, h] = bias
            bias_ref[1, h] = jnp.where(si >= WINDOW, bias, -jnp.inf)

    @pl.when(t == 0)
    def _():
        for i in range(4):
            kbuf[i, 0:WINDOW, :] = jnp.zeros((WINDOW, D_KV), BF16)
            vbuf[i, 0:WINDOW, :] = jnp.zeros((WINDOW, D_KV), BF16)
        ubuf[0:8, :] = jnp.zeros((8, D_CONV), F32)
        pbuf[0:16, :] = jnp.zeros((16, D_POOL), F32)

    @pl.when(t > 0)
    def _():
        for i in range(4):
            kbuf[i, 0:WINDOW, :] = kbuf[i, tm:tm + WINDOW, :]
            vbuf[i, 0:WINDOW, :] = vbuf[i, tm:tm + WINDOW, :]
        ubuf[0:8, :] = ubuf[tm:tm + 8, :]
        pbuf[0:16, :] = pbuf[tm:tm + 16, :]

    qkv = _dot(nb, win_ref[:, OFF_Q:OFF_CX])
    qbuf[...] = (qkv[:, OFF_Q:OFF_K] * (ATTN_SCALE * LOG2E)).astype(BF16)
    k = qkv[:, OFF_K:OFF_V]
    v = qkv[:, OFF_V:OFF_CX]
    kw_ref[...] = k[tm - WINDOW:, :]
    vw_ref[...] = v[tm - WINDOW:, :]
    k_sw = pltpu.roll(k, HEAD_DIM, 1)
    v_sw = pltpu.roll(v, HEAD_DIM, 1)
    k_var = (jnp.where(low, k, 0.0), jnp.where(low, 0.0, k_sw), jnp.where(low, k_sw, 0.0), jnp.where(low, 0.0, k))
    v_var = (jnp.where(low, v, 1.0), jnp.where(low, 1.0, v_sw), jnp.where(low, v_sw, 1.0), jnp.where(low, 1.0, v))
    for i in range(4):
        kbuf[i, WINDOW:WINDOW + tm, :] = k_var[i].astype(BF16)
        vbuf[i, WINDOW:WINDOW + tm, :] = v_var[i].astype(BF16)

    lane_q = lax.broadcasted_iota(jnp.int32, (WINDOW, 2 * HEAD_DIM), 1)
    low_q = lane_q < HEAD_DIM

    seq_start = jnp.where(t == 0, 1, 0)
    z_parts = []
    for j in range(tm // WINDOW):
        z_parts.append(_z_part(nb, win_ref, j))
        r0 = j * WINDOW
        first = seq_start if j == 0 else 0
        for pair in range(N_Q_HEADS // 2):
            qp = qbuf[r0:r0 + WINDOW, pair * 2 * HEAD_DIM:(pair + 1) * 2 * HEAD_DIM]
            halves = []
            for half in range(2):
                h = 2 * pair + half
                var = 2 * (h // GQA_GROUP) + half
                s = _dot_nt(qp, kbuf[var, r0:r0 + 2 * WINDOW, :]) + bias_ref[first, h]
                sink = sinks_ref[h] * LOG2E
                m = jnp.maximum(jnp.max(s, axis=-1, keepdims=True), sink)
                p = jnp.exp2(s - m).astype(BF16)
                o = _dot(p, vbuf[var, r0:r0 + 2 * WINDOW, :])
                den = (o[:, HEAD_DIM:HEAD_DIM + 1] if half == 0 else o[:, 0:1]) + jnp.exp2(sink - m)
                halves.append(o / den)
            att_ref[r0:r0 + WINDOW, pair * 2 * HEAD_DIM:(pair + 1) * 2 * HEAD_DIM] = (
                jnp.where(low_q, halves[0], halves[1]).astype(BF16))

    (conv,), (pool_in, gate_att), (gate_conv,), (gate_pool,) = z_parts
    cx = conv[:, 0:D_CONV]
    cb = conv[:, D_CONV:2 * D_CONV]
    ubuf[8:8 + tm, :] = conv[:, 2 * D_CONV:3 * D_CONV] * cx
    yc = (cw_ref[0:1, :] * ubuf[6:6 + tm, :] + cw_ref[1:2, :] * ubuf[7:7 + tm, :]
          + cw_ref[2:3, :] * ubuf[8:8 + tm, :])
    cs_ref[...] = ubuf[tm + 6:tm + 8, :]

    pbuf[16:16 + tm, :] = pool_in
    pos = lax.broadcasted_iota(jnp.int32, (tm, 1), 0) + t * tm + 1
    d_groups = []
    for g, w in enumerate(POOL_WINDOWS):
        c0 = g * POOL_GROUP
        cur = pbuf[16:16 + tm, c0:c0 + POOL_GROUP]
        tot = cur
        for j in range(1, w):
            tot = tot + pbuf[16 - j:16 - j + tm, c0:c0 + POOL_GROUP]
        cnt = jnp.minimum(pos, w).astype(F32)
        d_groups.append(tot / cnt - cur)
    pst_ref[...] = pbuf[tm + 1:tm + 16, :]

    h_ref[...] = _mix_tail(x, att_ref[...], cb, yc, d_groups, (gate_att, gate_conv, gate_pool),
                           wao_ref, wco_ref, wp_ref, ps_ref, wo_ref)


def _weight_specs(l, grid_rank):
    def const(*idx):
        if grid_rank == 1:
            return lambda i: idx
        return lambda b, t: idx

    return [
        pl.BlockSpec((None, 1, D_MODEL), const(l, 0, 0)),
        pl.BlockSpec((D_MODEL, D_IN), const(0, 0)),
        pl.BlockSpec((D_ATTN, D_MODEL), const(0, 0)),
        pl.BlockSpec((None, CONV_WIDTH, D_CONV), const(l, 0, 0)),
        pl.BlockSpec((D_CONV, D_MODEL), const(0, 0)),
        pl.BlockSpec((len(POOL_WINDOWS), POOL_GROUP, POOL_OUT_GROUP), const(0, 0, 0)),
        pl.BlockSpec((None, 1, D_MODEL), const(l, 0, 0)),
        pl.BlockSpec((D_MODEL, D_MODEL), const(0, 0)),
    ]


def _cast_job(src, lead, n_steps, step_of):
    rows, cols = src.shape[1:]
    r = rows // n_steps
    assert r * n_steps == rows and r % ROW_ALIGN == 0

    def block(*g):
        return jnp.minimum(step_of(*g), n_steps - 1)

    return (pl.BlockSpec((None, r, cols), lambda *g: (lead, block(*g), 0)),
            pl.BlockSpec((r, cols), lambda *g: (block(*g), 0)),
            jax.ShapeDtypeStruct((rows, cols), BF16))


def _run_casts(src_refs, dst_refs):
    for s, d in zip(src_refs, dst_refs):
        d[...] = s[...].astype(BF16)


def _prompt_mix(l, x, sinks, weights, prev_states, casts=()):
    tm = PROMPT_TILE
    nt = SEQ // tm
    n_alias = len(prev_states)
    jobs = [_cast_job(src, lead, n, lambda b, t: b * nt + t) for src, lead, n in casts]
    in_specs = ([pl.BlockSpec(memory_space=pltpu.SMEM),
                 pl.BlockSpec((tm, D_MODEL), lambda b, t: (b * nt + t, 0))] + _weight_specs(l, 2)
                + [pl.BlockSpec(memory_space=pl.ANY)] * n_alias + [j[0] for j in jobs])
    out_specs = [
        pl.BlockSpec((tm, D_MODEL), lambda b, t: (b * nt + t, 0)),
        pl.BlockSpec((None, None, WINDOW, D_KV), lambda b, t: (l, b, 0, 0)),
        pl.BlockSpec((None, None, WINDOW, D_KV), lambda b, t: (l, b, 0, 0)),
        pl.BlockSpec((None, None, CONV_WIDTH - 1, D_CONV), lambda b, t: (l, b, 0, 0)),
        pl.BlockSpec((None, None, POOL_BUF, D_POOL), lambda b, t: (l, b, 0, 0)),
    ]
    out_shape = [
        jax.ShapeDtypeStruct((N_PAD, D_MODEL), F32),
        jax.ShapeDtypeStruct((DEPTH, BATCH, WINDOW, D_KV), F32),
        jax.ShapeDtypeStruct((DEPTH, BATCH, WINDOW, D_KV), F32),
        jax.ShapeDtypeStruct((DEPTH, BATCH, CONV_WIDTH - 1, D_CONV), F32),
        jax.ShapeDtypeStruct((DEPTH, BATCH, POOL_BUF, D_POOL), F32),
    ]
    scratch = [
        pltpu.VMEM((tm, D_ATTN), BF16),
        pltpu.VMEM((4, WINDOW + tm, D_KV), BF16),
        pltpu.VMEM((4, WINDOW + tm, D_KV), BF16),
        pltpu.VMEM((tm, D_ATTN), BF16),
        pltpu.VMEM((8 + tm, D_CONV), F32),
        pltpu.VMEM((16 + tm, D_POOL), F32),
        pltpu.VMEM((2, N_Q_HEADS, WINDOW, 2 * WINDOW), F32),
    ]
    outs = pl.pallas_call(
        functools.partial(_prompt_mix_kernel, n_alias=n_alias, n_cast=len(jobs)),
        grid=(BATCH, nt),
        in_specs=in_specs,
        out_specs=out_specs + [j[1] for j in jobs],
        out_shape=out_shape + [j[2] for j in jobs],
        scratch_shapes=scratch,
        input_output_aliases={10 + i: 1 + i for i in range(n_alias)},
        compiler_params=pltpu.CompilerParams(
            dimension_semantics=("arbitrary", "arbitrary"), vmem_limit_bytes=VMEM_LIMIT),
        name=f"prompt_mix_l{l}",
    )(sinks, x, *weights, *prev_states, *[c[0] for c in casts])
    return outs[0], tuple(outs[1:5]), tuple(outs[5:])


def _sample_mix_kernel(*refs, n_alias):
    (sinks_ref, x_ref, kc_ref, vc_ref, cst_ref, pin_ref,
     nw_ref, win_ref, wao_ref, cw_ref, wco_ref, wp_ref, ps_ref, wo_ref) = refs[:14]
    (h_ref, ko_ref, vo_ref, co_ref, po_ref,
     qbuf, knb, vnb, att_ref, cbuf, ebuf, bias_c, bias_n) = refs[14 + n_alias:]
    ns, nr, T = SAMPLE_SEQS, SAMPLE_ROWS, DEC_SEQ
    x = x_ref[...]
    nb = _rms(x, nw_ref[...]).astype(BF16)

    qkv = _dot(nb, win_ref[:, OFF_Q:OFF_CX])
    qbuf[...] = (qkv[:, OFF_Q:OFF_K] * ATTN_SCALE).astype(BF16)
    k = qkv[:, OFF_K:OFF_V]
    v = qkv[:, OFF_V:OFF_CX]
    knb[...] = k.astype(BF16)
    vnb[...] = v.astype(BF16)
    ko_ref[:, 0:WINDOW - T, :] = kc_ref[:, T:WINDOW, :]
    vo_ref[:, 0:WINDOW - T, :] = vc_ref[:, T:WINDOW, :]
    ko_ref[:, WINDOW - T:WINDOW, :] = k.reshape(ns, T, D_KV)
    vo_ref[:, WINDOW - T:WINDOW, :] = v.reshape(ns, T, D_KV)

    gr = ATT_GROUP * T
    gc = ATT_GROUP * WINDOW

    @pl.when(pl.program_id(0) == 0)
    def _():
        rq = lax.broadcasted_iota(jnp.int32, (gr, gc), 0)
        cq = lax.broadcasted_iota(jnp.int32, (gr, gc), 1)
        tq = rq & (T - 1)
        sc_pos = cq & (WINDOW - 1)
        valid_c = jnp.logical_and((rq >> 3) == (cq >> 7), sc_pos >= tq)
        dist_c = (WINDOW + tq - sc_pos).astype(F32)
        rn = lax.broadcasted_iota(jnp.int32, (gr, gr), 0)
        cn = lax.broadcasted_iota(jnp.int32, (gr, gr), 1)
        tn = rn & (T - 1)
        jn = cn & (T - 1)
        valid_n = jnp.logical_and((rn >> 3) == (cn >> 3), jn <= tn)
        dist_n = (tn - jn).astype(F32)
        for h in range(N_Q_HEADS):
            bias_c[h] = jnp.where(valid_c, -_head_slope(h) * dist_c, -jnp.inf)
            bias_n[h] = jnp.where(valid_n, -_head_slope(h) * dist_n, -jnp.inf)

    z_parts = []
    for gi in range(ns // ATT_GROUP):
        z_parts.append(_z_part(nb, win_ref, gi))
        r0 = gi * gr
        s0 = gi * ATT_GROUP
        qg = qbuf[r0:r0 + gr, :]
        kcg = kc_ref[s0:s0 + ATT_GROUP, :, :].reshape(gc, D_KV).astype(BF16)
        vcg = vc_ref[s0:s0 + ATT_GROUP, :, :].reshape(gc, D_KV).astype(BF16)
        kng = knb[r0:r0 + gr, :]
        vng = vnb[r0:r0 + gr, :]
        outs = []
        for h in range(N_Q_HEADS):
            kv = h // GQA_GROUP
            lo, hi = kv * HEAD_DIM, (kv + 1) * HEAD_DIM
            qh = qg[:, h * HEAD_DIM:(h + 1) * HEAD_DIM]
            s_c = _dot_nt(qh, kcg[:, lo:hi]) + bias_c[h]
            s_n = _dot_nt(qh, kng[:, lo:hi]) + bias_n[h]
            outs.append(_softmax_pv([(s_c, vcg[:, lo:hi]), (s_n, vng[:, lo:hi])], sinks_ref[h]))
        att_ref[r0:r0 + gr, :] = jnp.concatenate(outs, axis=-1).astype(BF16)

    (conv,), (pool_in, gate_att), (gate_conv,), (gate_pool,) = z_parts
    cx = conv[:, 0:D_CONV]
    cb = conv[:, D_CONV:2 * D_CONV]
    u = conv[:, 2 * D_CONV:3 * D_CONV] * cx
    cbuf[:, 6:8, :] = cst_ref[...]
    cbuf[:, 8:16, :] = u.reshape(ns, T, D_CONV)
    w0 = cw_ref[0:1, :].reshape(1, 1, D_CONV)
    w1 = cw_ref[1:2, :].reshape(1, 1, D_CONV)
    w2 = cw_ref[2:3, :].reshape(1, 1, D_CONV)
    yc = (w0 * cbuf[:, 6:14, :] + w1 * cbuf[:, 7:15, :] + w2 * cbuf[:, 8:16, :]).reshape(nr, D_CONV)
    co_ref[...] = cbuf[:, 14:16, :]

    ebuf[:, 1:16, :] = pin_ref[...]
    ebuf[:, 16:24, :] = pool_in.reshape(ns, T, D_POOL)
    d_groups = []
    for g, w in enumerate(POOL_WINDOWS):
        c0 = g * POOL_GROUP
        cur = ebuf[:, 16:24, c0:c0 + POOL_GROUP]
        tot = cur
        for j in range(1, w):
            tot = tot + ebuf[:, 16 - j:24 - j, c0:c0 + POOL_GROUP]
        d_groups.append((tot / float(w) - cur).reshape(nr, POOL_GROUP))
    po_ref[...] = ebuf[:, 9:24, :]

    h_ref[...] = _mix_tail(x, att_ref[...], cb, yc, d_groups, (gate_att, gate_conv, gate_pool),
                           wao_ref, wco_ref, wp_ref, ps_ref, wo_ref)


def _sample_mix(l, x, x_row0, sinks, kc, vc, cst, pst, weights, h_buf, prev_states):
    ns, nr = SAMPLE_SEQS, SAMPLE_ROWS
    n_alias = 1 + len(prev_states)
    xb0 = x_row0 // nr
    hb0 = N_PROMPT // nr
    in_specs = [
        pl.BlockSpec(memory_space=pltpu.SMEM),
        pl.BlockSpec((nr, D_MODEL), lambda i: (xb0 + i, 0)),
        pl.BlockSpec((None, ns, WINDOW, D_KV), lambda i: (l, i, 0, 0)),
        pl.BlockSpec((None, ns, WINDOW, D_KV), lambda i: (l, i, 0, 0)),
        pl.BlockSpec((None, ns, CONV_WIDTH - 1, D_CONV), lambda i: (l, i, 0, 0)),
        pl.BlockSpec((None, ns, POOL_BUF, D_POOL), lambda i: (l, i, 0, 0)),
    ] + _weight_specs(l, 1) + [pl.BlockSpec(memory_space=pl.ANY)] * n_alias
    out_specs = [
        pl.BlockSpec((nr, D_MODEL), lambda i: (hb0 + i, 0)),
        pl.BlockSpec((None, ns, WINDOW, D_KV), lambda i: (l, i, 0, 0)),
        pl.BlockSpec((None, ns, WINDOW, D_KV), lambda i: (l, i, 0, 0)),
        pl.BlockSpec((None, ns, CONV_WIDTH - 1, D_CONV), lambda i: (l, i, 0, 0)),
        pl.BlockSpec((None, ns, POOL_BUF, D_POOL), lambda i: (l, i, 0, 0)),
    ]
    out_shape = [
        jax.ShapeDtypeStruct((N_PAD, D_MODEL), F32),
        jax.ShapeDtypeStruct((DEPTH, DEC_BATCH, WINDOW, D_KV), F32),
        jax.ShapeDtypeStruct((DEPTH, DEC_BATCH, WINDOW, D_KV), F32),
        jax.ShapeDtypeStruct((DEPTH, DEC_BATCH, CONV_WIDTH - 1, D_CONV), F32),
        jax.ShapeDtypeStruct((DEPTH, DEC_BATCH, POOL_BUF, D_POOL), F32),
    ]
    scratch = [
        pltpu.VMEM((nr, D_ATTN), BF16),
        pltpu.VMEM((nr, D_KV), BF16),
        pltpu.VMEM((nr, D_KV), BF16),
        pltpu.VMEM((nr, D_ATTN), BF16),
        pltpu.VMEM((ns, 16, D_CONV), F32),
        pltpu.VMEM((ns, 24, D_POOL), F32),
        pltpu.VMEM((N_Q_HEADS, ATT_GROUP * DEC_SEQ, ATT_GROUP * WINDOW), F32),
        pltpu.VMEM((N_Q_HEADS, ATT_GROUP * DEC_SEQ, ATT_GROUP * DEC_SEQ), F32),
    ]
    outs = pl.pallas_call(
        functools.partial(_sample_mix_kernel, n_alias=n_alias),
        grid=(DEC_BATCH // ns,),
        in_specs=in_specs,
        out_specs=out_specs,
        out_shape=out_shape,
        scratch_shapes=scratch,
        input_output_aliases={14 + i: i for i in range(n_alias)},
        compiler_params=pltpu.CompilerParams(
            dimension_semantics=("arbitrary",), vmem_limit_bytes=VMEM_LIMIT),
        name=f"sample_mix_l{l}",
    )(sinks, x, kc, vc, cst, pst, *weights, h_buf, *prev_states)
    return outs[0], tuple(outs[1:])


def _ffn_kernel(*refs, n_cast):
    h_ref, nw_ref, wg_ref, wu_ref, wd_ref = refs[:5]
    o_ref = refs[5 + n_cast]
    h = h_ref[...]
    hn = _rms(h, nw_ref[...]).astype(BF16)
    g = _dot(hn, wg_ref[...])
    u = _dot(hn, wu_ref[...])
    o_ref[...] = h + _dot((g * _sigmoid(g) * u).astype(BF16), wd_ref[...])
    _run_casts(refs[5:5 + n_cast], refs[6 + n_cast:])


def _ffn_dense(h, nw, wg, wu, wd, casts):
    tm = SUB_ROWS
    steps = N_TOK // tm
    assert all(n <= steps for _, _, n in casts)
    jobs = [_cast_job(src, lead, n, lambda i: i) for src, lead, n in casts]
    resident = dict(pipeline_mode=pl.Buffered(1))
    outs = pl.pallas_call(
        functools.partial(_ffn_kernel, n_cast=len(jobs)),
        grid=(steps,),
        in_specs=[
            pl.BlockSpec((tm, D_MODEL), lambda i: (i, 0)),
            pl.BlockSpec((1, D_MODEL), lambda i: (0, 0)),
            pl.BlockSpec((D_MODEL, D_FF), lambda i: (0, 0), **resident),
            pl.BlockSpec((D_MODEL, D_FF), lambda i: (0, 0), **resident),
            pl.BlockSpec((D_FF, D_MODEL), lambda i: (0, 0), **resident),
        ] + [j[0] for j in jobs],
        out_specs=[pl.BlockSpec((tm, D_MODEL), lambda i: (i, 0))] + [j[1] for j in jobs],
        out_shape=[jax.ShapeDtypeStruct((N_PAD, D_MODEL), F32)] + [j[2] for j in jobs],
        compiler_params=pltpu.CompilerParams(
            dimension_semantics=("arbitrary",), vmem_limit_bytes=VMEM_LIMIT),
        name="ffn_dense",
    )(h, nw, wg, wu, wd, *[c[0] for c in casts])
    return outs[0], tuple(outs[1:])


def _router_kernel(h_ref, nw_ref, rt_ref, hn_ref, slot_ref, col_ref, cnt_ref):
    tm = FFN_TILE
    hn = _rms(h_ref[...], nw_ref[...])
    hn_ref[...] = hn.astype(BF16)
    logits = lax.dot_general(rt_ref[...], hn, (((1,), (1,)), ((), ())),
                             precision=lax.Precision.HIGHEST, preferred_element_type=F32)
    eidx = lax.broadcasted_iota(jnp.int32, (N_EXPERTS, tm), 0).astype(F32)
    none = float(N_EXPERTS)
    m1 = jnp.max(logits, axis=0, keepdims=True)
    i1 = jnp.min(jnp.where(logits == m1, eidx, none), axis=0, keepdims=True)
    rest = jnp.where(eidx == i1, -jnp.inf, logits)
    m2 = jnp.max(rest, axis=0, keepdims=True)
    i2 = jnp.min(jnp.where(rest == m2, eidx, none), axis=0, keepdims=True)
    e2 = jnp.exp(m2 - m1)
    w1 = 1.0 / (1.0 + e2)
    w2 = e2 / (1.0 + e2)
    sel1 = eidx == i1
    sel2 = eidx == i2
    gate = jnp.where(sel1, w1, jnp.where(sel2, w2, 0.0))
    chosen = jnp.logical_or(sel1, sel2)
    mask = jnp.where(chosen, 1.0, 0.0)
    srow = lax.broadcasted_iota(jnp.int32, (tm, tm), 0)
    scol = lax.broadcasted_iota(jnp.int32, (tm, tm), 1)
    upper = jnp.where(srow < scol, 1.0, 0.0).astype(BF16)
    mask16 = jnp.concatenate([mask, jnp.zeros_like(mask)], axis=0).astype(BF16)
    slot = jnp.where(chosen, _dot(mask16, upper)[0:N_EXPERTS, :], -1.0)
    slot_ref[...] = slot.astype(jnp.int32)
    cnt_ref[...] = jnp.broadcast_to(jnp.sum(mask, axis=1, keepdims=True), (N_EXPERTS, 128))
    both = jnp.concatenate([slot, gate, jnp.zeros((128 - 2 * N_EXPERTS, tm), F32)], axis=0)
    col_ref[...] = both.T[:, 0:2 * N_EXPERTS]


def _router(h, nw, router_t):
    tm = FFN_TILE
    nt = N_UNITS
    return pl.pallas_call(
        _router_kernel,
        grid=(nt,),
        in_specs=[
            pl.BlockSpec((tm, D_MODEL), lambda i: (i, 0)),
            pl.BlockSpec((1, D_MODEL), lambda i: (0, 0)),
            pl.BlockSpec((N_EXPERTS, D_MODEL), lambda i: (0, 0)),
        ],
        out_specs=[
            pl.BlockSpec((tm, D_MODEL), lambda i: (i, 0)),
            pl.BlockSpec((N_EXPERTS, tm), lambda i: (0, i)),
            pl.BlockSpec((tm, 2 * N_EXPERTS), lambda i: (i, 0)),
            pl.BlockSpec((None, N_EXPERTS, 128), lambda i: (i, 0, 0)),
        ],
        out_shape=[
            jax.ShapeDtypeStruct((N_PAD, D_MODEL), BF16),
            jax.ShapeDtypeStruct((N_EXPERTS, N_PAD), jnp.int32),
            jax.ShapeDtypeStruct((N_PAD, 2 * N_EXPERTS), F32),
            jax.ShapeDtypeStruct((nt, N_EXPERTS, 128), F32),
        ],
        compiler_params=pltpu.CompilerParams(
            dimension_semantics=("arbitrary",), vmem_limit_bytes=VMEM_LIMIT),
        name="moe_router",
    )(h, nw, router_t)


def _plan(cnt):
    seg = -(-cnt // ROW_ALIGN) * ROW_ALIGN
    rows_e = jnp.sum(seg, axis=0)
    tiles_e = -(-rows_e // EXP_TILE)
    cum_tiles = jnp.cumsum(tiles_e)
    base_e = (cum_tiles - tiles_e) * EXP_TILE
    dst = base_e[None, :] + jnp.cumsum(seg, axis=0) - seg
    tail = jnp.concatenate([base_e + rows_e, tiles_e * EXP_TILE - rows_e]).astype(jnp.int32)
    n_tiles = cum_tiles[-1]
    i = jnp.arange(MAX_TILES, dtype=jnp.int32)
    tile_expert = jnp.sum((cum_tiles[None, :] <= i[:, None]).astype(jnp.int32), axis=1)
    tile_expert = jnp.minimum(tile_expert, N_EXPERTS - 1)
    tile_expert = jnp.where(i < n_tiles, tile_expert, tile_expert[jnp.maximum(n_tiles - 1, 0)])
    return dst.reshape(-1).astype(jnp.int32), tail, tile_expert, n_tiles.reshape(1).astype(jnp.int32)


def _ceil_blocks(n, block):
    return (n + block - 1) >> (block.bit_length() - 1)


def _segment_copies(n_rows, src_at, dst_at, sem, max_bits=SEG_BITS):
    n = n_rows >> (ROW_ALIGN.bit_length() - 1)
    out = []
    for bit in range(max_bits - 1, -1, -1):
        size = ROW_ALIGN << bit
        off = pl.multiple_of(((n >> (bit + 1)) << (bit + 1)) * ROW_ALIGN, ROW_ALIGN)
        out.append((((n >> bit) & 1) == 1, pltpu.make_async_copy(src_at(off, size), dst_at(off, size), sem)))
    return out


def _start_all(copies):
    for pred, cp in copies:
        pl.when(pred)(cp.start)


def _wait_all(copies):
    for pred, cp in copies:
        pl.when(pred)(cp.wait)


def _dispatch_kernel(cnt_ref, dst_ref, tail_ref, hn_ref, srow_ref, xs_ref, stage, sems):
    tu, rb = FFN_TILE, ROW_BLOCK
    t = pl.program_id(0)

    def copies(e):
        slot = e % 2
        n_rows = _ceil_blocks(cnt_ref[t * N_EXPERTS + e], ROW_ALIGN) * ROW_ALIGN
        d0 = dst_ref[t * N_EXPERTS + e]
        return _segment_copies(
            n_rows,
            lambda off, size: stage.at[slot, pl.ds(off, size)],
            lambda off, size: xs_ref.at[pl.ds(pl.multiple_of(d0 + off, ROW_ALIGN), size)],
            sems.at[slot])

    for e in range(N_EXPERTS):
        slot = e % 2
        if e >= 2:
            _wait_all(copies(e - 2))
        slot_row = srow_ref[e:e + 1, :]
        n = cnt_ref[t * N_EXPERTS + e]

        def gather_rows(r0, rows, slot=slot, slot_row=slot_row):
            rid = lax.broadcasted_iota(jnp.int32, (rows, tu), 0) + r0
            onehot = jnp.where(slot_row == rid, 1.0, 0.0).astype(BF16)
            stage[slot, pl.ds(r0, rows), :] = _dot(onehot, hn_ref[...]).astype(BF16)

        @pl.when(n > 0)
        def _():
            gather_rows(0, HEAD_BLOCK)

        def gather(b, carry):
            gather_rows(pl.multiple_of(HEAD_BLOCK + b * rb, rb), rb)
            return carry

        lax.fori_loop(0, _ceil_blocks(jnp.maximum(n - HEAD_BLOCK, 0), rb), gather, 0)
        _start_all(copies(e))

    _wait_all(copies(N_EXPERTS - 2))
    _wait_all(copies(N_EXPERTS - 1))

    @pl.when(t == pl.num_programs(0) - 1)
    def _():
        stage[0, 0:EXP_TILE, :] = jnp.zeros((EXP_TILE, D_MODEL), BF16)

        def tail_copies(e):
            d0 = tail_ref[e]
            return _segment_copies(
                tail_ref[N_EXPERTS + e],
                lambda off, size: stage.at[0, pl.ds(0, size)],
                lambda off, size: xs_ref.at[pl.ds(pl.multiple_of(d0 + off, ROW_ALIGN), size)],
                sems.at[e % 2], max_bits=EXP_TILE.bit_length() - ROW_ALIGN.bit_length())

        for e in range(N_EXPERTS):
            _start_all(tail_copies(e))
        for e in range(N_EXPERTS):
            _wait_all(tail_copies(e))


def _dispatch(cnt, dst, tail, hn, slot_row):
    tm = FFN_TILE
    n_rows = MAX_TILES * EXP_TILE
    grid_spec = pltpu.PrefetchScalarGridSpec(
        num_scalar_prefetch=3,
        grid=(N_UNITS,),
        in_specs=[
            pl.BlockSpec((tm, D_MODEL), lambda t, c, d, z: (t, 0)),
            pl.BlockSpec((N_EXPERTS, tm), lambda t, c, d, z: (0, t)),
        ],
        out_specs=pl.BlockSpec(memory_space=pl.ANY),
        scratch_shapes=[pltpu.VMEM((2, tm, D_MODEL), BF16), pltpu.SemaphoreType.DMA((2,))],
    )
    return pl.pallas_call(
        _dispatch_kernel,
        grid_spec=grid_spec,
        out_shape=jax.ShapeDtypeStruct((n_rows, D_MODEL), BF16),
        compiler_params=pltpu.CompilerParams(
            dimension_semantics=("arbitrary",), vmem_limit_bytes=VMEM_LIMIT),
        name="moe_dispatch",
    )(cnt, dst, tail, hn, slot_row)


def _experts_kernel(texp_ref, ntile_ref, xs_ref, wg_ref, wu_ref, wd_ref, ys_ref, acc_ref):
    del texp_ref
    i, c = pl.program_id(0), pl.program_id(1)
    nc = pl.num_programs(1)

    @pl.when(i < ntile_ref[0])
    def _():
        @pl.when(c == 0)
        def _():
            acc_ref[...] = jnp.zeros_like(acc_ref)

        for r in range(0, EXP_TILE, SUB_ROWS):
            xb = xs_ref[r:r + SUB_ROWS, :]
            g = _dot(xb, wg_ref[...])
            u = _dot(xb, wu_ref[...])
            acc_ref[r:r + SUB_ROWS, :] += _dot((g * _sigmoid(g) * u).astype(BF16), wd_ref[...])

        @pl.when(c == nc - 1)
        def _():
            ys_ref[...] = acc_ref[...].astype(BF16)


def _experts(tile_expert, n_tiles, xs, wg, wu, wd):
    tm = EXP_TILE
    nc = D_EXPERT // EXP_CHUNK

    def row_map(i, c, te, nt):
        return (jnp.minimum(i, jnp.maximum(nt[0] - 1, 0)), 0)

    def chunk(i, c, nt):
        return jnp.where(i < nt[0], c, nc - 1)

    grid_spec = pltpu.PrefetchScalarGridSpec(
        num_scalar_prefetch=2,
        grid=(MAX_TILES, nc),
        in_specs=[
            pl.BlockSpec((tm, D_MODEL), row_map),
            pl.BlockSpec((None, D_MODEL, EXP_CHUNK), lambda i, c, te, nt: (te[i], 0, chunk(i, c, nt))),
            pl.BlockSpec((None, D_MODEL, EXP_CHUNK), lambda i, c, te, nt: (te[i], 0, chunk(i, c, nt))),
            pl.BlockSpec((None, EXP_CHUNK, D_MODEL), lambda i, c, te, nt: (te[i], chunk(i, c, nt), 0)),
        ],
        out_specs=pl.BlockSpec((tm, D_MODEL), row_map),
        scratch_shapes=[pltpu.VMEM((tm, D_MODEL), F32)],
    )
    return pl.pallas_call(
        _experts_kernel,
        grid_spec=grid_spec,
        out_shape=jax.ShapeDtypeStruct((MAX_TILES * tm, D_MODEL), BF16),
        compiler_params=pltpu.CompilerParams(
            dimension_semantics=("arbitrary", "arbitrary"), vmem_limit_bytes=VMEM_LIMIT),
        name="moe_experts",
    )(tile_expert, n_tiles, xs, wg, wu, wd)


def _combine_kernel(cnt_ref, dst_ref, ys_ref, col_ref, h_ref, nf_ref, op_ref, os_ref, stage, sems, acc_ref):
    tu, rb = FFN_TILE, 2 * ROW_BLOCK
    t = pl.program_id(0)

    @pl.when(t == 0)
    def _():
        stage[...] = jnp.zeros_like(stage)

    def copies(e):
        n_rows = _ceil_blocks(cnt_ref[t * N_EXPERTS + e], ROW_ALIGN) * ROW_ALIGN
        d0 = dst_ref[t * N_EXPERTS + e]
        return _segment_copies(
            n_rows,
            lambda off, size: ys_ref.at[pl.ds(pl.multiple_of(d0 + off, ROW_ALIGN), size)],
            lambda off, size: stage.at[e, pl.ds(off, size)],
            sems.at[e])

    for e in range(N_EXPERTS):
        _start_all(copies(e))
    acc_ref[...] = h_ref[...]
    for e in range(N_EXPERTS):
        _wait_all(copies(e))
        slot_col = col_ref[:, e:e + 1]
        gate_col = col_ref[:, e + N_EXPERTS:e + N_EXPERTS + 1]

        def scatter(b, carry, e=e, slot_col=slot_col, gate_col=gate_col):
            r0 = pl.multiple_of(b * rb, rb)
            cid = (lax.broadcasted_iota(jnp.int32, (tu, rb), 1) + b * rb).astype(F32)
            weighted = jnp.where(slot_col == cid, gate_col, 0.0).astype(BF16)
            acc_ref[...] += _dot(weighted, stage[e, pl.ds(r0, rb), :])
            return carry

        lax.fori_loop(0, _ceil_blocks(cnt_ref[t * N_EXPERTS + e], rb), scatter, 0)

    out = _rms(acc_ref[...], nf_ref[...])

    @pl.when(t < N_PROMPT // FFN_TILE)
    def _():
        op_ref[...] = out

    @pl.when(t >= N_PROMPT // FFN_TILE)
    def _():
        os_ref[...] = out


def _combine(cnt, dst, ys, cols, h, nf):
    tm = FFN_TILE
    np_tiles = N_PROMPT // tm
    assert N_PROMPT % tm == 0 and N_SAMPLE % tm == 0
    grid_spec = pltpu.PrefetchScalarGridSpec(
        num_scalar_prefetch=2,
        grid=(N_UNITS,),
        in_specs=[
            pl.BlockSpec(memory_space=pl.ANY),
            pl.BlockSpec((tm, 2 * N_EXPERTS), lambda t, c, d: (t, 0)),
            pl.BlockSpec((tm, D_MODEL), lambda t, c, d: (t, 0)),
            pl.BlockSpec((1, D_MODEL), lambda t, c, d: (0, 0)),
        ],
        out_specs=[
            pl.BlockSpec((tm, D_MODEL), lambda t, c, d: (jnp.minimum(t, np_tiles - 1), 0)),
            pl.BlockSpec((tm, D_MODEL), lambda t, c, d: (jnp.maximum(t - np_tiles, 0), 0)),
        ],
        scratch_shapes=[pltpu.VMEM((N_EXPERTS, tm, D_MODEL), BF16), pltpu.SemaphoreType.DMA((N_EXPERTS,)),
                        pltpu.VMEM((tm, D_MODEL), F32)],
    )
    return pl.pallas_call(
        _combine_kernel,
        grid_spec=grid_spec,
        out_shape=[
            jax.ShapeDtypeStruct((N_PROMPT, D_MODEL), F32),
            jax.ShapeDtypeStruct((N_SAMPLE, D_MODEL), F32),
        ],
        compiler_params=pltpu.CompilerParams(
            dimension_semantics=("arbitrary",), vmem_limit_bytes=VMEM_LIMIT),
        name="moe_combine",
    )(cnt, dst, ys, cols, h, nf)


def kernel(x_prompt, x_sample, cache_win_k, cache_win_v, state_conv, state_pool, norm_mix, w_in, attn_sinks,
           w_attn_out, conv_w, w_conv_out, w_pool, pool_scale, w_out, norm_ffn, ffn_w_gate, ffn_w_up, ffn_w_down,
           moe_router, moe_w_gate, moe_w_up, moe_w_down, norm_final):
    assert DEPTH == 2 and ffn_w_gate.shape[0] == 1 and moe_w_gate.shape[0] == 1
    norm3 = norm_mix.reshape(DEPTH, 1, D_MODEL)
    scale3 = pool_scale.reshape(DEPTH, 1, D_MODEL)
    w_pool3 = w_pool.reshape(DEPTH, D_POOL, POOL_OUT_GROUP)
    pool_shape = (len(POOL_WINDOWS), POOL_GROUP, POOL_OUT_GROUP)
    mixer_bf16 = [(w_in[0].astype(BF16), w_attn_out[0].astype(BF16), w_conv_out[0].astype(BF16),
                   w_pool[0].astype(BF16), w_out[0].astype(BF16))]
    ffn_casts = ((ffn_w_gate, 0, 32), (ffn_w_up, 0, 32), (ffn_w_down, 0, 16))
    later_casts = (
        (moe_w_gate.reshape(1, N_EXPERTS * D_MODEL, D_EXPERT), 0, CAST_STEPS),
        (moe_w_up.reshape(1, N_EXPERTS * D_MODEL, D_EXPERT), 0, CAST_STEPS),
        (moe_w_down.reshape(1, N_EXPERTS * D_EXPERT, D_MODEL), 0, CAST_STEPS),
        (w_in, 1, 64), (w_attn_out, 1, 32), (w_conv_out, 1, 32), (w_pool3, 1, 32), (w_out, 1, 64))
    kc = cache_win_k.reshape(DEPTH, DEC_BATCH, WINDOW, D_KV)
    vc = cache_win_v.reshape(DEPTH, DEC_BATCH, WINDOW, D_KV)

    xp = x_prompt.reshape(N_PROMPT, D_MODEL)
    xs, xs_row0 = x_sample.reshape(N_SAMPLE, D_MODEL), 0
    p_states, s_states = (), ()
    for l in range(DEPTH):
        sinks = attn_sinks[l]
        wi, wao, wco, wp, wo = mixer_bf16[l]
        mix_weights = (norm3, wi, wao, conv_w, wco, wp, scale3, wo)
        h, p_states, ffn_bf16 = _prompt_mix(l, xp, sinks, mix_weights, p_states, ffn_casts if l == 0 else ())
        h, s_states = _sample_mix(l, xs, xs_row0, sinks, kc, vc, state_conv, state_pool, mix_weights, h, s_states)
        i = l // 2
        nw = norm_ffn[l].reshape(1, D_MODEL)
        if l % 2 == 0:
            xp, (eg, eu, ed, wi, wao, wco, wp, wo) = _ffn_dense(h, nw, *ffn_bf16, later_casts)
            mixer_bf16.append((wi, wao, wco, wp.reshape(pool_shape), wo))
            xs, xs_row0 = xp, N_PROMPT
        else:
            hn, slot_row, cols, cnt = _router(h, nw, moe_router[i].T)
            cnt = cnt[:, :, 0].astype(jnp.int32)
            dst, tail, tile_expert, n_tiles = _plan(cnt)
            cnt = cnt.reshape(-1)
            xs_sorted = _dispatch(cnt, dst, tail, hn, slot_row)
            ys_sorted = _experts(tile_expert, n_tiles, xs_sorted,
                                 eg.reshape(N_EXPERTS, D_MODEL, D_EXPERT), eu.reshape(N_EXPERTS, D_MODEL, D_EXPERT),
                                 ed.reshape(N_EXPERTS, D_EXPERT, D_MODEL))
            y_prompt, y_sample = _combine(cnt, dst, ys_sorted, cols, h, norm_final.reshape(1, D_MODEL))

    pk, pv, pc, pp = p_states
    sk, sv, sc, sp = s_states
    kv_shape_p = (DEPTH, BATCH, WINDOW, N_KV_HEADS, HEAD_DIM)
    kv_shape_s = (DEPTH, DEC_BATCH, WINDOW, N_KV_HEADS, HEAD_DIM)
    return (y_prompt.reshape(BATCH, SEQ, D_MODEL), y_sample.reshape(DEC_BATCH, DEC_SEQ, D_MODEL),
            pk.reshape(kv_shape_p), pv.reshape(kv_shape_p), pc, pp,
            sk.reshape(kv_shape_s), sv.reshape(kv_shape_s), sc, sp)
```

```python
import functools
import math

import jax
import jax.numpy as jnp
from jax import lax
from jax.experimental import pallas as pl
from jax.experimental.pallas import tpu as pltpu

F32 = jnp.float32
BF16 = jnp.bfloat16

D_MODEL = 1024
BATCH = 4
SEQ = 4096
DEPTH = 2
DEC_BATCH = 128
DEC_SEQ = 8
HEAD_DIM = 64
N_Q_HEADS = 8
N_KV_HEADS = 2
GQA_GROUP = N_Q_HEADS // N_KV_HEADS
D_ATTN = N_Q_HEADS * HEAD_DIM
D_KV = N_KV_HEADS * HEAD_DIM
WINDOW = 128
ATTN_SCALE = HEAD_DIM ** -0.5
D_CONV = D_MODEL // 2
CONV_WIDTH = 3
D_POOL = D_MODEL // 2
POOL_WINDOWS = (2, 4, 8, 16)
POOL_GROUP = D_POOL // len(POOL_WINDOWS)
POOL_OUT_GROUP = D_MODEL // len(POOL_WINDOWS)
POOL_BUF = max(POOL_WINDOWS) - 1
OFF_Q = 0
OFF_K = OFF_Q + D_ATTN
OFF_V = OFF_K + D_KV
OFF_CX = OFF_V + D_KV
OFF_CB = OFF_CX + D_CONV
OFF_CC = OFF_CB + D_CONV
OFF_P = OFF_CC + D_CONV
OFF_G = OFF_P + D_POOL
D_IN = OFF_G + 3 * D_MODEL
D_FF = 2816
N_EXPERTS = 8
D_EXPERT = 3584
EPS = 1e-5
LOG2E = math.log2(math.e)

N_PROMPT = BATCH * SEQ
N_SAMPLE = DEC_BATCH * DEC_SEQ
N_TOK = N_PROMPT + N_SAMPLE

PROMPT_TILE = 512
SAMPLE_SEQS = 32
SAMPLE_ROWS = SAMPLE_SEQS * DEC_SEQ
ATT_GROUP = 8
FFN_TILE = 1024
EXP_CHUNK = D_EXPERT // 2
SUB_ROWS = 256
CAST_STEPS = 64
ROW_BLOCK = 128
HEAD_BLOCK = 256
ROW_ALIGN = 16
EXP_TILE = 512
N_UNITS = N_TOK // FFN_TILE
MAX_TILES = -(-(2 * N_TOK + N_UNITS * N_EXPERTS * (ROW_ALIGN - 1)) // EXP_TILE) + N_EXPERTS
N_PAD = N_TOK
SEG_BITS = 7
assert FFN_TILE == ROW_ALIGN << (SEG_BITS - 1)
VMEM_LIMIT = 58 * 1024 * 1024


def _dot(a, b):
    return jnp.dot(a, b, preferred_element_type=F32)


def _dot_nt(a, b):
    return lax.dot_general(a, b, (((1,), (1,)), ((), ())), preferred_element_type=F32)


def _rms(x, g):
    return x * lax.rsqrt(jnp.mean(x * x, axis=-1, keepdims=True) + EPS) * g


def _sigmoid(x):
    return 0.5 * jnp.tanh(0.5 * x) + 0.5


def _head_slope(h):
    return float(2.0 ** (-8.0 * (h + 1) / N_Q_HEADS))


def _z_part(nb, win_ref, i):
    if i == 0:
        return (_dot(nb, win_ref[:, OFF_CX:OFF_P]),)
    if i == 1:
        return (_dot(nb, win_ref[:, OFF_P:OFF_G]), _dot(nb, win_ref[:, OFF_G:OFF_G + D_MODEL]))
    return (_dot(nb, win_ref[:, OFF_G + (i - 1) * D_MODEL:OFF_G + i * D_MODEL]),)


def _mix_tail(x, att_bf, cb, yc, d_groups, gate_logits, wao_ref, wco_ref, wp_ref, ps_ref, wo_ref):
    att_o = _dot(att_bf, wao_ref[...])
    merged = _sigmoid(gate_logits[0]) * att_o
    conv_o = _dot((cb * yc).astype(BF16), wco_ref[...])
    merged = merged + _sigmoid(gate_logits[1]) * conv_o
    pool_o = jnp.concatenate([_dot(d.astype(BF16), wp_ref[g]) for g, d in enumerate(d_groups)], axis=-1)
    pool_o = pool_o * ps_ref[...]
    merged = merged + _sigmoid(gate_logits[2]) * pool_o
    return x + _dot(merged.astype(BF16), wo_ref[...])


def _prompt_mix_kernel(*refs, n_alias, n_cast):
    (sinks_ref, x_ref, nw_ref, win_ref, wao_ref, cw_ref, wco_ref, wp_ref, ps_ref, wo_ref) = refs[:10]
    cast_src = refs[10 + n_alias:10 + n_alias + n_cast]
    outs = refs[10 + n_alias + n_cast:]
    h_ref, kw_ref, vw_ref, cs_ref, pst_ref = outs[:5]
    cast_dst = outs[5:5 + n_cast]
    qbuf, kbuf, vbuf, att_ref, ubuf, pbuf, bias_ref = outs[5 + n_cast:]
    _run_casts(cast_src, cast_dst)
    tm = PROMPT_TILE
    b = pl.program_id(0)
    t = pl.program_id(1)
    x = x_ref[...]
    nb = _rms(x, nw_ref[...]).astype(BF16)
    lane = lax.broadcasted_iota(jnp.int32, (tm, D_KV), 1)
    low = lane < HEAD_DIM

    @pl.when(jnp.logical_and(b == 0, t == 0))
    def _():
        qi = lax.broadcasted_iota(jnp.int32, (WINDOW, 2 * WINDOW), 0)
        si = lax.broadcasted_iota(jnp.int32, (WINDOW, 2 * WINDOW), 1)
        dist_i = qi + WINDOW - si
        dist = dist_i.astype(F32)
        band = jnp.logical_and(dist_i >= 0, dist_i <= WINDOW)
        for h in range(N_Q_HEADS):
            bias = jnp.where(band, (-_head_slope(h) * LOG2E) * dist, -jnp.inf)
            bias_ref[0, h] = bias
            bias_ref[1, h] = jnp.where(si >= WINDOW, bias, -jnp.inf)

    @pl.when(t == 0)
    def _():
        for i in range(4):
            kbuf[i, 0:WINDOW, :] = jnp.zeros((WINDOW, D_KV), BF16)
            vbuf[i, 0:WINDOW, :] = jnp.zeros((WINDOW, D_KV), BF16)
        ubuf[0:8, :] = jnp.zeros((8, D_CONV), F32)
        pbuf[0:16, :] = jnp.zeros((16, D_POOL), F32)

    @pl.when(t > 0)
    def _():
        for i in range(4):
            kbuf[i, 0:WINDOW, :] = kbuf[i, tm:tm + WINDOW, :]
            vbuf[i, 0:WINDOW, :] = vbuf[i, tm:tm + WINDOW, :]
        ubuf[0:8, :] = ubuf[tm:tm + 8, :]
        pbuf[0:16, :] = pbuf[tm:tm + 16, :]

    qkv = _dot(nb, win_ref[:, OFF_Q:OFF_CX])
    qbuf[...] = (qkv[:, OFF_Q:OFF_K] * (ATTN_SCALE * LOG2E)).astype(BF16)
    k = qkv[:, OFF_K:OFF_V]
    v = qkv[:, OFF_V:OFF_CX]
    kw_ref[...] = k[tm - WINDOW:, :]
    vw_ref[...] = v[tm - WINDOW:, :]
    k_sw = pltpu.roll(k, HEAD_DIM, 1)
    v_sw = pltpu.roll(v, HEAD_DIM, 1)
    k_var = (jnp.where(low, k, 0.0), jnp.where(low, 0.0, k_sw), jnp.where(low, k_sw, 0.0), jnp.where(low, 0.0, k))
    v_var = (jnp.where(low, v, 1.0), jnp.where(low, 1.0, v_sw), jnp.where(low, v_sw, 1.0), jnp.where(low, 1.0, v))
    for i in range(4):
        kbuf[i, WINDOW:WINDOW + tm, :] = k_var[i].astype(BF16)
        vbuf[i, WINDOW:WINDOW + tm, :] = v_var[i].astype(BF16)

    lane_q = lax.broadcasted_iota(jnp.int32, (WINDOW, 2 * HEAD_DIM), 1)
    low_q = lane_q < HEAD_DIM

    seq_start = jnp.where(t == 0, 1, 0)
    variant = [2 * (h // GQA_GROUP) + h % 2 for h in range(N_Q_HEADS)]
    units = [(j, h) for j in range(tm // WINDOW) for h in range(N_Q_HEADS)]
    z_parts = [_z_part(nb, win_ref, 0)]
    scores = {}
    for j, h in units:
        r0 = j * WINDOW
        qp = qbuf[r0:r0 + WINDOW, (h // 2) * 2 * HEAD_DIM:(h // 2 + 1) * 2 * HEAD_DIM]
        first = seq_start if j == 0 else 0
        scores[j, h] = _dot_nt(qp, kbuf[variant[h], r0:r0 + 2 * WINDOW, :]) + bias_ref[first, h]
    z_parts.append(_z_part(nb, win_ref, 1))
    probs = {}
    for j, h in units:
        sink = sinks_ref[h] * LOG2E
        m = jnp.maximum(jnp.max(scores[j, h], axis=-1, keepdims=True), sink)
        probs[j, h] = (jnp.exp2(scores[j, h] - m).astype(BF16), jnp.exp2(sink - m))
    z_parts.append(_z_part(nb, win_ref, 2))
    outs = {}
    for j, h in units:
        r0 = j * WINDOW
        p, sink_term = probs[j, h]
        o = _dot(p, vbuf[variant[h], r0:r0 + 2 * WINDOW, :])
        den = (o[:, HEAD_DIM:HEAD_DIM + 1] if h % 2 == 0 else o[:, 0:1]) + sink_term
        outs[j, h] = o / den
    z_parts.append(_z_part(nb, win_ref, 3))
    for j in range(tm // WINDOW):
        for pair in range(N_Q_HEADS // 2):
            att_ref[j * WINDOW:(j + 1) * WINDOW, pair * 2 * HEAD_DIM:(pair + 1) * 2 * HEAD_DIM] = (
                jnp.where(low_q, outs[j, 2 * pair], outs[j, 2 * pair + 1]).astype(BF16))

    (conv,), (pool_in, gate_att), (gate_conv,), (gate_pool,) = z_parts
    cx = conv[:, 0:D_CONV]
    cb = conv[:, D_CONV:2 * D_CONV]
    ubuf[8:8 + tm, :] = conv[:, 2 * D_CONV:3 * D_CONV] * cx
    yc = (cw_ref[0:1, :] * ubuf[6:6 + tm, :] + cw_ref[1:2, :] * ubuf[7:7 + tm, :]
          + cw_ref[2:3, :] * ubuf[8:8 + tm, :])
    cs_ref[...] = ubuf[tm + 6:tm + 8, :]

    pbuf[16:16 + tm, :] = pool_in
    pos = lax.broadcasted_iota(jnp.int32, (tm, 1), 0) + t * tm + 1
    d_groups = []
    for g, w in enumerate(POOL_WINDOWS):
        c0 = g * POOL_GROUP
        cur = pbuf[16:16 + tm, c0:c0 + POOL_GROUP]
        tot = cur
        for j in range(1, w):
            tot = tot + pbuf[16 - j:16 - j + tm, c0:c0 + POOL_GROUP]
        cnt = jnp.minimum(pos, w).astype(F32)
        d_groups.append(tot / cnt - cur)
    pst_ref[...] = pbuf[tm + 1:tm + 16, :]

    h_ref[...] = _mix_tail(x, att_ref[...], cb, yc, d_groups, (gate_att, gate_conv, gate_pool),
                           wao_ref, wco_ref, wp_ref, ps_ref, wo_ref)


def _weight_specs(l, grid_rank):
    def const(*idx):
        if grid_rank == 1:
            return lambda i: idx
        return lambda b, t: idx

    return [
        pl.BlockSpec((None, 1, D_MODEL), const(l, 0, 0)),
        pl.BlockSpec((D_MODEL, D_IN), const(0, 0)),
        pl.BlockSpec((D_ATTN, D_MODEL), const(0, 0)),
        pl.BlockSpec((None, CONV_WIDTH, D_CONV), const(l, 0, 0)),
        pl.BlockSpec((D_CONV, D_MODEL), const(0, 0)),
        pl.BlockSpec((len(POOL_WINDOWS), POOL_GROUP, POOL_OUT_GROUP), const(0, 0, 0)),
        pl.BlockSpec((None, 1, D_MODEL), const(l, 0, 0)),
        pl.BlockSpec((D_MODEL, D_MODEL), const(0, 0)),
    ]


def _cast_job(src, lead, n_steps, step_of):
    rows, cols = src.shape[1:]
    r = rows // n_steps
    assert r * n_steps == rows and r % ROW_ALIGN == 0

    def block(*g):
        return jnp.minimum(step_of(*g), n_steps - 1)

    return (pl.BlockSpec((None, r, cols), lambda *g: (lead, block(*g), 0)),
            pl.BlockSpec((r, cols), lambda *g: (block(*g), 0)),
            jax.ShapeDtypeStruct((rows, cols), BF16))


def _run_casts(src_refs, dst_refs):
    for s, d in zip(src_refs, dst_refs):
        d[...] = s[...].astype(BF16)


def _prompt_mix(l, x, sinks, weights, prev_states, casts=()):
    tm = PROMPT_TILE
    nt = SEQ // tm
    n_alias = len(prev_states)
    jobs = [_cast_job(src, lead, n, lambda b, t: b * nt + t) for src, lead, n in casts]
    in_specs = ([pl.BlockSpec(memory_space=pltpu.SMEM),
                 pl.BlockSpec((tm, D_MODEL), lambda b, t: (b * nt + t, 0))] + _weight_specs(l, 2)
                + [pl.BlockSpec(memory_space=pl.ANY)] * n_alias + [j[0] for j in jobs])
    out_specs = [
        pl.BlockSpec((tm, D_MODEL), lambda b, t: (b * nt + t, 0)),
        pl.BlockSpec((None, None, WINDOW, D_KV), lambda b, t: (l, b, 0, 0)),
        pl.BlockSpec((None, None, WINDOW, D_KV), lambda b, t: (l, b, 0, 0)),
        pl.BlockSpec((None, None, CONV_WIDTH - 1, D_CONV), lambda b, t: (l, b, 0, 0)),
        pl.BlockSpec((None, None, POOL_BUF, D_POOL), lambda b, t: (l, b, 0, 0)),
    ]
    out_shape = [
        jax.ShapeDtypeStruct((N_PAD, D_MODEL), F32),
        jax.ShapeDtypeStruct((DEPTH, BATCH, WINDOW, D_KV), F32),
        jax.ShapeDtypeStruct((DEPTH, BATCH, WINDOW, D_KV), F32),
        jax.ShapeDtypeStruct((DEPTH, BATCH, CONV_WIDTH - 1, D_CONV), F32),
        jax.ShapeDtypeStruct((DEPTH, BATCH, POOL_BUF, D_POOL), F32),
    ]
    scratch = [
        pltpu.VMEM((tm, D_ATTN), BF16),
        pltpu.VMEM((4, WINDOW + tm, D_KV), BF16),
        pltpu.VMEM((4, WINDOW + tm, D_KV), BF16),
        pltpu.VMEM((tm, D_ATTN), BF16),
        pltpu.VMEM((8 + tm, D_CONV), F32),
        pltpu.VMEM((16 + tm, D_POOL), F32),
        pltpu.VMEM((2, N_Q_HEADS, WINDOW, 2 * WINDOW), F32),
    ]
    outs = pl.pallas_call(
        functools.partial(_prompt_mix_kernel, n_alias=n_alias, n_cast=len(jobs)),
        grid=(BATCH, nt),
        in_specs=in_specs,
        out_specs=out_specs + [j[1] for j in jobs],
        out_shape=out_shape + [j[2] for j in jobs],
        scratch_shapes=scratch,
        input_output_aliases={10 + i: 1 + i for i in range(n_alias)},
        compiler_params=pltpu.CompilerParams(
            dimension_semantics=("arbitrary", "arbitrary"), vmem_limit_bytes=VMEM_LIMIT),
        name=f"prompt_mix_l{l}",
    )(sinks, x, *weights, *prev_states, *[c[0] for c in casts])
    return outs[0], tuple(outs[1:5]), tuple(outs[5:])


def _sample_mix_kernel(*refs, n_alias):
    (sinks_ref, x_ref, kc_ref, vc_ref, cst_ref, pin_ref,
     nw_ref, win_ref, wao_ref, cw_ref, wco_ref, wp_ref, ps_ref, wo_ref) = refs[:14]
    (h_ref, ko_ref, vo_ref, co_ref, po_ref,
     qbuf, knb, vnb, att_ref, cbuf, ebuf, bias_c, bias_n) = refs[14 + n_alias:]
    ns, nr, T = SAMPLE_SEQS, SAMPLE_ROWS, DEC_SEQ
    x = x_ref[...]
    nb = _rms(x, nw_ref[...]).astype(BF16)

    qkv = _dot(nb, win_ref[:, OFF_Q:OFF_CX])
    qbuf[...] = (qkv[:, OFF_Q:OFF_K] * ATTN_SCALE).astype(BF16)
    k = qkv[:, OFF_K:OFF_V]
    v = qkv[:, OFF_V:OFF_CX]
    knb[...] = k.astype(BF16)
    vnb[...] = v.astype(BF16)
    ko_ref[:, 0:WINDOW - T, :] = kc_ref[:, T:WINDOW, :]
    vo_ref[:, 0:WINDOW - T, :] = vc_ref[:, T:WINDOW, :]
    ko_ref[:, WINDOW - T:WINDOW, :] = k.reshape(ns, T, D_KV)
    vo_ref[:, WINDOW - T:WINDOW, :] = v.reshape(ns, T, D_KV)

    gr = ATT_GROUP * T
    gc = ATT_GROUP * WINDOW

    @pl.when(pl.program_id(0) == 0)
    def _():
        rq = lax.broadcasted_iota(jnp.int32, (gr, gc), 0)
        cq = lax.broadcasted_iota(jnp.int32, (gr, gc), 1)
        tq = rq & (T - 1)
        sc_pos = cq & (WINDOW - 1)
        valid_c = jnp.logical_and((rq >> 3) == (cq >> 7), sc_pos >= tq)
        dist_c = (WINDOW + tq - sc_pos).astype(F32)
        rn = lax.broadcasted_iota(jnp.int32, (gr, gr), 0)
        cn = lax.broadcasted_iota(jnp.int32, (gr, gr), 1)
        tn = rn & (T - 1)
        jn = cn & (T - 1)
        valid_n = jnp.logical_and((rn >> 3) == (cn >> 3), jn <= tn)
        dist_n = (tn - jn).astype(F32)
        for h in range(N_Q_HEADS):
            bias_c[h] = jnp.where(valid_c, -_head_slope(h) * dist_c, -jnp.inf)
            bias_n[h] = jnp.where(valid_n, -_head_slope(h) * dist_n, -jnp.inf)

    z_parts = []
    for gi in range(ns // ATT_GROUP):
        z_parts.append(_z_part(nb, win_ref, gi))
        r0 = gi * gr
        s0 = gi * ATT_GROUP
        qg = qbuf[r0:r0 + gr, :]
        kcg = kc_ref[s0:s0 + ATT_GROUP, :, :].reshape(gc, D_KV).astype(BF16)
        vcg = vc_ref[s0:s0 + ATT_GROUP, :, :].reshape(gc, D_KV).astype(BF16)
        kng = knb[r0:r0 + gr, :]
        vng = vnb[r0:r0 + gr, :]
        kvs = [(h // GQA_GROUP) * HEAD_DIM for h in range(N_Q_HEADS)]
        scores = []
        for h in range(N_Q_HEADS):
            qh = qg[:, h * HEAD_DIM:(h + 1) * HEAD_DIM]
            scores.append((_dot_nt(qh, kcg[:, kvs[h]:kvs[h] + HEAD_DIM]) + bias_c[h],
                           _dot_nt(qh, kng[:, kvs[h]:kvs[h] + HEAD_DIM]) + bias_n[h]))
        probs = []
        for h in range(N_Q_HEADS):
            s_c, s_n = scores[h]
            sink = sinks_ref[h]
            m = jnp.maximum(jnp.maximum(jnp.max(s_c, axis=-1, keepdims=True),
                                        jnp.max(s_n, axis=-1, keepdims=True)), sink)
            p_c = jnp.exp(s_c - m)
            p_n = jnp.exp(s_n - m)
            den = jnp.exp(sink - m) + jnp.sum(p_c, axis=-1, keepdims=True) + jnp.sum(p_n, axis=-1, keepdims=True)
            probs.append((p_c.astype(BF16), p_n.astype(BF16), den))
        outs = []
        for h in range(N_Q_HEADS):
            p_c, p_n, den = probs[h]
            o = _dot(p_c, vcg[:, kvs[h]:kvs[h] + HEAD_DIM]) + _dot(p_n, vng[:, kvs[h]:kvs[h] + HEAD_DIM])
            outs.append(o / den)
        att_ref[r0:r0 + gr, :] = jnp.concatenate(outs, axis=-1).astype(BF16)

    (conv,), (pool_in, gate_att), (gate_conv,), (gate_pool,) = z_parts
    cx = conv[:, 0:D_CONV]
    cb = conv[:, D_CONV:2 * D_CONV]
    u = conv[:, 2 * D_CONV:3 * D_CONV] * cx
    cbuf[:, 6:8, :] = cst_ref[...]
    cbuf[:, 8:16, :] = u.reshape(ns, T, D_CONV)
    w0 = cw_ref[0:1, :].reshape(1, 1, D_CONV)
    w1 = cw_ref[1:2, :].reshape(1, 1, D_CONV)
    w2 = cw_ref[2:3, :].reshape(1, 1, D_CONV)
    yc = (w0 * cbuf[:, 6:14, :] + w1 * cbuf[:, 7:15, :] + w2 * cbuf[:, 8:16, :]).reshape(nr, D_CONV)
    co_ref[...] = cbuf[:, 14:16, :]

    ebuf[:, 1:16, :] = pin_ref[...]
    ebuf[:, 16:24, :] = pool_in.reshape(ns, T, D_POOL)
    d_groups = []
    for g, w in enumerate(POOL_WINDOWS):
        c0 = g * POOL_GROUP
        cur = ebuf[:, 16:24, c0:c0 + POOL_GROUP]
        tot = cur
        for j in range(1, w):
            tot = tot + ebuf[:, 16 - j:24 - j, c0:c0 + POOL_GROUP]
        d_groups.append((tot / float(w) - cur).reshape(nr, POOL_GROUP))
    po_ref[...] = ebuf[:, 9:24, :]

    h_ref[...] = _mix_tail(x, att_ref[...], cb, yc, d_groups, (gate_att, gate_conv, gate_pool),
                           wao_ref, wco_ref, wp_ref, ps_ref, wo_ref)


def _sample_mix(l, x, x_row0, sinks, kc, vc, cst, pst, weights, h_buf, prev_states):
    ns, nr = SAMPLE_SEQS, SAMPLE_ROWS
    n_alias = 1 + len(prev_states)
    xb0 = x_row0 // nr
    hb0 = N_PROMPT // nr
    in_specs = [
        pl.BlockSpec(memory_space=pltpu.SMEM),
        pl.BlockSpec((nr, D_MODEL), lambda i: (xb0 + i, 0)),
        pl.BlockSpec((None, ns, WINDOW, D_KV), lambda i: (l, i, 0, 0)),
        pl.BlockSpec((None, ns, WINDOW, D_KV), lambda i: (l, i, 0, 0)),
        pl.BlockSpec((None, ns, CONV_WIDTH - 1, D_CONV), lambda i: (l, i, 0, 0)),
        pl.BlockSpec((None, ns, POOL_BUF, D_POOL), lambda i: (l, i, 0, 0)),
    ] + _weight_specs(l, 1) + [pl.BlockSpec(memory_space=pl.ANY)] * n_alias
    out_specs = [
        pl.BlockSpec((nr, D_MODEL), lambda i: (hb0 + i, 0)),
        pl.BlockSpec((None, ns, WINDOW, D_KV), lambda i: (l, i, 0, 0)),
        pl.BlockSpec((None, ns, WINDOW, D_KV), lambda i: (l, i, 0, 0)),
        pl.BlockSpec((None, ns, CONV_WIDTH - 1, D_CONV), lambda i: (l, i, 0, 0)),
        pl.BlockSpec((None, ns, POOL_BUF, D_POOL), lambda i: (l, i, 0, 0)),
    ]
    out_shape = [
        jax.ShapeDtypeStruct((N_PAD, D_MODEL), F32),
        jax.ShapeDtypeStruct((DEPTH, DEC_BATCH, WINDOW, D_KV), F32),
        jax.ShapeDtypeStruct((DEPTH, DEC_BATCH, WINDOW, D_KV), F32),
        jax.ShapeDtypeStruct((DEPTH, DEC_BATCH, CONV_WIDTH - 1, D_CONV), F32),
        jax.ShapeDtypeStruct((DEPTH, DEC_BATCH, POOL_BUF, D_POOL), F32),
    ]
    scratch = [
        pltpu.VMEM((nr, D_ATTN), BF16),
        pltpu.VMEM((nr, D_KV), BF16),
        pltpu.VMEM((nr, D_KV), BF16),
        pltpu.VMEM((nr, D_ATTN), BF16),
        pltpu.VMEM((ns, 16, D_CONV), F32),
        pltpu.VMEM((ns, 24, D_POOL), F32),
        pltpu.VMEM((N_Q_HEADS, ATT_GROUP * DEC_SEQ, ATT_GROUP * WINDOW), F32),
        pltpu.VMEM((N_Q_HEADS, ATT_GROUP * DEC_SEQ, ATT_GROUP * DEC_SEQ), F32),
    ]
    outs = pl.pallas_call(
        functools.partial(_sample_mix_kernel, n_alias=n_alias),
        grid=(DEC_BATCH // ns,),
        in_specs=in_specs,
        out_specs=out_specs,
        out_shape=out_shape,
        scratch_shapes=scratch,
        input_output_aliases={14 + i: i for i in range(n_alias)},
        compiler_params=pltpu.CompilerParams(
            dimension_semantics=("arbitrary",), vmem_limit_bytes=VMEM_LIMIT),
        name=f"sample_mix_l{l}",
    )(sinks, x, kc, vc, cst, pst, *weights, h_buf, *prev_states)
    return outs[0], tuple(outs[1:])


def _ffn_kernel(*refs, n_cast):
    h_ref, nw_ref, wg_ref, wu_ref, wd_ref = refs[:5]
    o_ref = refs[5 + n_cast]
    h = h_ref[...]
    hn = _rms(h, nw_ref[...]).astype(BF16)
    g = _dot(hn, wg_ref[...])
    u = _dot(hn, wu_ref[...])
    o_ref[...] = h + _dot((g * _sigmoid(g) * u).astype(BF16), wd_ref[...])
    _run_casts(refs[5:5 + n_cast], refs[6 + n_cast:])


def _ffn_dense(h, nw, wg, wu, wd, casts):
    tm = SUB_ROWS
    steps = N_TOK // tm
    assert all(n <= steps for _, _, n in casts)
    jobs = [_cast_job(src, lead, n, lambda i: i) for src, lead, n in casts]
    resident = dict(pipeline_mode=pl.Buffered(1))
    outs = pl.pallas_call(
        functools.partial(_ffn_kernel, n_cast=len(jobs)),
        grid=(steps,),
        in_specs=[
            pl.BlockSpec((tm, D_MODEL), lambda i: (i, 0)),
            pl.BlockSpec((1, D_MODEL), lambda i: (0, 0)),
            pl.BlockSpec((D_MODEL, D_FF), lambda i: (0, 0), **resident),
            pl.BlockSpec((D_MODEL, D_FF), lambda i: (0, 0), **resident),
            pl.BlockSpec((D_FF, D_MODEL), lambda i: (0, 0), **resident),
        ] + [j[0] for j in jobs],
        out_specs=[pl.BlockSpec((tm, D_MODEL), lambda i: (i, 0))] + [j[1] for j in jobs],
        out_shape=[jax.ShapeDtypeStruct((N_PAD, D_MODEL), F32)] + [j[2] for j in jobs],
        compiler_params=pltpu.CompilerParams(
            dimension_semantics=("arbitrary",), vmem_limit_bytes=VMEM_LIMIT),
        name="ffn_dense",
    )(h, nw, wg, wu, wd, *[c[0] for c in casts])
    return outs[0], tuple(outs[1:])


def _router_kernel(h_ref, nw_ref, rt_ref, hn_ref, slot_ref, col_ref, cnt_ref):
    tm = FFN_TILE
    hn = _rms(h_ref[...], nw_ref[...])
    hn_ref[...] = hn.astype(BF16)
    logits = lax.dot_general(rt_ref[...], hn, (((1,), (1,)), ((), ())),
                             precision=lax.Precision.HIGHEST, preferred_element_type=F32)
    eidx = lax.broadcasted_iota(jnp.int32, (N_EXPERTS, tm), 0).astype(F32)
    none = float(N_EXPERTS)
    m1 = jnp.max(logits, axis=0, keepdims=True)
    i1 = jnp.min(jnp.where(logits == m1, eidx, none), axis=0, keepdims=True)
    rest = jnp.where(eidx == i1, -jnp.inf, logits)
    m2 = jnp.max(rest, axis=0, keepdims=True)
    i2 = jnp.min(jnp.where(rest == m2, eidx, none), axis=0, keepdims=True)
    e2 = jnp.exp(m2 - m1)
    w1 = 1.0 / (1.0 + e2)
    w2 = e2 / (1.0 + e2)
    sel1 = eidx == i1
    sel2 = eidx == i2
    gate = jnp.where(sel1, w1, jnp.where(sel2, w2, 0.0))
    chosen = jnp.logical_or(sel1, sel2)
    mask = jnp.where(chosen, 1.0, 0.0)
    srow = lax.broadcasted_iota(jnp.int32, (tm, tm), 0)
    scol = lax.broadcasted_iota(jnp.int32, (tm, tm), 1)
    upper = jnp.where(srow < scol, 1.0, 0.0).astype(BF16)
    mask16 = jnp.concatenate([mask, jnp.zeros_like(mask)], axis=0).astype(BF16)
    slot = jnp.where(chosen, _dot(mask16, upper)[0:N_EXPERTS, :], -1.0)
    slot_ref[...] = slot.astype(jnp.int32)
    cnt_ref[...] = jnp.broadcast_to(jnp.sum(mask, axis=1, keepdims=True), (N_EXPERTS, 128))
    both = jnp.concatenate([slot, gate, jnp.zeros((128 - 2 * N_EXPERTS, tm), F32)], axis=0)
    col_ref[...] = both.T[:, 0:2 * N_EXPERTS]


def _router(h, nw, router_t):
    tm = FFN_TILE
    nt = N_UNITS
    return pl.pallas_call(
        _router_kernel,
        grid=(nt,),
        in_specs=[
            pl.BlockSpec((tm, D_MODEL), lambda i: (i, 0)),
            pl.BlockSpec((1, D_MODEL), lambda i: (0, 0)),
            pl.BlockSpec((N_EXPERTS, D_MODEL), lambda i: (0, 0)),
        ],
        out_specs=[
            pl.BlockSpec((tm, D_MODEL), lambda i: (i, 0)),
            pl.BlockSpec((N_EXPERTS, tm), lambda i: (0, i)),
            pl.BlockSpec((tm, 2 * N_EXPERTS), lambda i: (i, 0)),
            pl.BlockSpec((None, N_EXPERTS, 128), lambda i: (i, 0, 0)),
        ],
        out_shape=[
            jax.ShapeDtypeStruct((N_PAD, D_MODEL), BF16),
            jax.ShapeDtypeStruct((N_EXPERTS, N_PAD), jnp.int32),
            jax.ShapeDtypeStruct((N_PAD, 2 * N_EXPERTS), F32),
            jax.ShapeDtypeStruct((nt, N_EXPERTS, 128), F32),
        ],
        compiler_params=pltpu.CompilerParams(
            dimension_semantics=("arbitrary",), vmem_limit_bytes=VMEM_LIMIT),
        name="moe_router",
    )(h, nw, router_t)


def _plan(cnt):
    seg = -(-cnt // ROW_ALIGN) * ROW_ALIGN
    rows_e = jnp.sum(seg, axis=0)
    tiles_e = -(-rows_e // EXP_TILE)
    cum_tiles = jnp.cumsum(tiles_e)
    base_e = (cum_tiles - tiles_e) * EXP_TILE
    dst = base_e[None, :] + jnp.cumsum(seg, axis=0) - seg
    tail = jnp.concatenate([base_e + rows_e, tiles_e * EXP_TILE - rows_e]).astype(jnp.int32)
    n_tiles = cum_tiles[-1]
    i = jnp.arange(MAX_TILES, dtype=jnp.int32)
    tile_expert = jnp.sum((cum_tiles[None, :] <= i[:, None]).astype(jnp.int32), axis=1)
    tile_expert = jnp.minimum(tile_expert, N_EXPERTS - 1)
    tile_expert = jnp.where(i < n_tiles, tile_expert, tile_expert[jnp.maximum(n_tiles - 1, 0)])
    return dst.reshape(-1).astype(jnp.int32), tail, tile_expert, n_tiles.reshape(1).astype(jnp.int32)


def _ceil_blocks(n, block):
    return (n + block - 1) >> (block.bit_length() - 1)


def _segment_copies(n_rows, src_at, dst_at, sem, max_bits=SEG_BITS):
    n = n_rows >> (ROW_ALIGN.bit_length() - 1)
    out = []
    for bit in range(max_bits - 1, -1, -1):
        size = ROW_ALIGN << bit
        off = pl.multiple_of(((n >> (bit + 1)) << (bit + 1)) * ROW_ALIGN, ROW_ALIGN)
        out.append((((n >> bit) & 1) == 1, pltpu.make_async_copy(src_at(off, size), dst_at(off, size), sem)))
    return out


def _start_all(copies):
    for pred, cp in copies:
        pl.when(pred)(cp.start)


def _wait_all(copies):
    for pred, cp in copies:
        pl.when(pred)(cp.wait)


def _dispatch_kernel(cnt_ref, dst_ref, tail_ref, hn_ref, srow_ref, xs_ref, stage, sems):
    tu, rb = FFN_TILE, ROW_BLOCK
    t = pl.program_id(0)

    def copies(e):
        slot = e % 2
        n_rows = _ceil_blocks(cnt_ref[t * N_EXPERTS + e], ROW_ALIGN) * ROW_ALIGN
        d0 = dst_ref[t * N_EXPERTS + e]
        return _segment_copies(
            n_rows,
            lambda off, size: stage.at[slot, pl.ds(off, size)],
            lambda off, size: xs_ref.at[pl.ds(pl.multiple_of(d0 + off, ROW_ALIGN), size)],
            sems.at[slot])

    for e in range(N_EXPERTS):
        slot = e % 2
        if e >= 2:
            _wait_all(copies(e - 2))
        slot_row = srow_ref[e:e + 1, :]
        n = cnt_ref[t * N_EXPERTS + e]

        def gather_rows(r0, rows, slot=slot, slot_row=slot_row):
            rid = lax.broadcasted_iota(jnp.int32, (rows, tu), 0) + r0
            onehot = jnp.where(slot_row == rid, 1.0, 0.0).astype(BF16)
            stage[slot, pl.ds(r0, rows), :] = _dot(onehot, hn_ref[...]).astype(BF16)

        @pl.when(n > 0)
        def _():
            gather_rows(0, HEAD_BLOCK)

        def gather(b, carry):
            gather_rows(pl.multiple_of(HEAD_BLOCK + b * rb, rb), rb)
            return carry

        lax.fori_loop(0, _ceil_blocks(jnp.maximum(n - HEAD_BLOCK, 0), rb), gather, 0)
        _start_all(copies(e))

    _wait_all(copies(N_EXPERTS - 2))
    _wait_all(copies(N_EXPERTS - 1))

    @pl.when(t == pl.num_programs(0) - 1)
    def _():
        stage[0, 0:EXP_TILE, :] = jnp.zeros((EXP_TILE, D_MODEL), BF16)

        def tail_copies(e):
            d0 = tail_ref[e]
            return _segment_copies(
                tail_ref[N_EXPERTS + e],
                lambda off, size: stage.at[0, pl.ds(0, size)],
                lambda off, size: xs_ref.at[pl.ds(pl.multiple_of(d0 + off, ROW_ALIGN), size)],
                sems.at[e % 2], max_bits=EXP_TILE.bit_length() - ROW_ALIGN.bit_length())

        for e in range(N_EXPERTS):
            _start_all(tail_copies(e))
        for e in range(N_EXPERTS):
            _wait_all(tail_copies(e))


def _dispatch(cnt, dst, tail, hn, slot_row):
    tm = FFN_TILE
    n_rows = MAX_TILES * EXP_TILE
    grid_spec = pltpu.PrefetchScalarGridSpec(
        num_scalar_prefetch=3,
        grid=(N_UNITS,),
        in_specs=[
            pl.BlockSpec((tm, D_MODEL), lambda t, c, d, z: (t, 0)),
            pl.BlockSpec((N_EXPERTS, tm), lambda t, c, d, z: (0, t)),
        ],
        out_specs=pl.BlockSpec(memory_space=pl.ANY),
        scratch_shapes=[pltpu.VMEM((2, tm, D_MODEL), BF16), pltpu.SemaphoreType.DMA((2,))],
    )
    return pl.pallas_call(
        _dispatch_kernel,
        grid_spec=grid_spec,
        out_shape=jax.ShapeDtypeStruct((n_rows, D_MODEL), BF16),
        compiler_params=pltpu.CompilerParams(
            dimension_semantics=("arbitrary",), vmem_limit_bytes=VMEM_LIMIT),
        name="moe_dispatch",
    )(cnt, dst, tail, hn, slot_row)


def _experts_kernel(texp_ref, ntile_ref, xs_ref, wg_ref, wu_ref, wd_ref, ys_ref, acc_ref):
    del texp_ref
    i, c = pl.program_id(0), pl.program_id(1)
    nc = pl.num_programs(1)

    @pl.when(i < ntile_ref[0])
    def _():
        @pl.when(c == 0)
        def _():
            acc_ref[...] = jnp.zeros_like(acc_ref)

        for r in range(0, EXP_TILE, SUB_ROWS):
            xb = xs_ref[r:r + SUB_ROWS, :]
            g = _dot(xb, wg_ref[...])
            u = _dot(xb, wu_ref[...])
            acc_ref[r:r + SUB_ROWS, :] += _dot((g * _sigmoid(g) * u).astype(BF16), wd_ref[...])

        @pl.when(c == nc - 1)
        def _():
            ys_ref[...] = acc_ref[...].astype(BF16)


def _experts(tile_expert, n_tiles, xs, wg, wu, wd):
    tm = EXP_TILE
    nc = D_EXPERT // EXP_CHUNK

    def row_map(i, c, te, nt):
        return (jnp.minimum(i, jnp.maximum(nt[0] - 1, 0)), 0)

    def chunk(i, c, nt):
        return jnp.where(i < nt[0], c, nc - 1)

    grid_spec = pltpu.PrefetchScalarGridSpec(
        num_scalar_prefetch=2,
        grid=(MAX_TILES, nc),
        in_specs=[
            pl.BlockSpec((tm, D_MODEL), row_map),
            pl.BlockSpec((None, D_MODEL, EXP_CHUNK), lambda i, c, te, nt: (te[i], 0, chunk(i, c, nt))),
            pl.BlockSpec((None, D_MODEL, EXP_CHUNK), lambda i, c, te, nt: (te[i], 0, chunk(i, c, nt))),
            pl.BlockSpec((None, EXP_CHUNK, D_MODEL), lambda i, c, te, nt: (te[i], chunk(i, c, nt), 0)),
        ],
        out_specs=pl.BlockSpec((tm, D_MODEL), row_map),
        scratch_shapes=[pltpu.VMEM((tm, D_MODEL), F32)],
    )
    return pl.pallas_call(
        _experts_kernel,
        grid_spec=grid_spec,
        out_shape=jax.ShapeDtypeStruct((MAX_TILES * tm, D_MODEL), BF16),
        compiler_params=pltpu.CompilerParams(
            dimension_semantics=("arbitrary", "arbitrary"), vmem_limit_bytes=VMEM_LIMIT),
        name="moe_experts",
    )(tile_expert, n_tiles, xs, wg, wu, wd)


def _combine_kernel(cnt_ref, dst_ref, ys_ref, col_ref, h_ref, nf_ref, op_ref, os_ref, stage, sems, acc_ref):
    tu, rb = FFN_TILE, 2 * ROW_BLOCK
    t = pl.program_id(0)

    @pl.when(t == 0)
    def _():
        stage[...] = jnp.zeros_like(stage)

    def copies(e):
        n_rows = _ceil_blocks(cnt_ref[t * N_EXPERTS + e], ROW_ALIGN) * ROW_ALIGN
        d0 = dst_ref[t * N_EXPERTS + e]
        return _segment_copies(
            n_rows,
            lambda off, size: ys_ref.at[pl.ds(pl.multiple_of(d0 + off, ROW_ALIGN), size)],
            lambda off, size: stage.at[e, pl.ds(off, size)],
            sems.at[e])

    for e in range(N_EXPERTS):
        _start_all(copies(e))
    acc_ref[...] = h_ref[...]
    for e in range(N_EXPERTS):
        _wait_all(copies(e))
        slot_col = col_ref[:, e:e + 1]
        gate_col = col_ref[:, e + N_EXPERTS:e + N_EXPERTS + 1]

        def scatter(b, carry, e=e, slot_col=slot_col, gate_col=gate_col):
            r0 = pl.multiple_of(b * rb, rb)
            cid = (lax.broadcasted_iota(jnp.int32, (tu, rb), 1) + b * rb).astype(F32)
            weighted = jnp.where(slot_col == cid, gate_col, 0.0).astype(BF16)
            acc_ref[...] += _dot(weighted, stage[e, pl.ds(r0, rb), :])
            return carry

        lax.fori_loop(0, _ceil_blocks(cnt_ref[t * N_EXPERTS + e], rb), scatter, 0)

    out = _rms(acc_ref[...], nf_ref[...])

    @pl.when(t < N_PROMPT // FFN_TILE)
    def _():
        op_ref[...] = out

    @pl.when(t >= N_PROMPT // FFN_TILE)
    def _():
        os_ref[...] = out


def _combine(cnt, dst, ys, cols, h, nf):
    tm = FFN_TILE
    np_tiles = N_PROMPT // tm
    assert N_PROMPT % tm == 0 and N_SAMPLE % tm == 0
    grid_spec = pltpu.PrefetchScalarGridSpec(
        num_scalar_prefetch=2,
        grid=(N_UNITS,),
        in_specs=[
            pl.BlockSpec(memory_space=pl.ANY),
            pl.BlockSpec((tm, 2 * N_EXPERTS), lambda t, c, d: (t, 0)),
            pl.BlockSpec((tm, D_MODEL), lambda t, c, d: (t, 0)),
            pl.BlockSpec((1, D_MODEL), lambda t, c, d: (0, 0)),
        ],
        out_specs=[
            pl.BlockSpec((tm, D_MODEL), lambda t, c, d: (jnp.minimum(t, np_tiles - 1), 0)),
            pl.BlockSpec((tm, D_MODEL), lambda t, c, d: (jnp.maximum(t - np_tiles, 0), 0)),
        ],
        scratch_shapes=[pltpu.VMEM((N_EXPERTS, tm, D_MODEL), BF16), pltpu.SemaphoreType.DMA((N_EXPERTS,)),
                        pltpu.VMEM((tm, D_MODEL), F32)],
    )
    return pl.pallas_call(
        _combine_kernel,
        grid_spec=grid_spec,
        out_shape=[
            jax.ShapeDtypeStruct((N_PROMPT, D_MODEL), F32),
            jax.ShapeDtypeStruct((N_SAMPLE, D_MODEL), F32),
        ],
        compiler_params=pltpu.CompilerParams(
            dimension_semantics=("arbitrary",), vmem_limit_bytes=VMEM_LIMIT),
        name="moe_combine",
    )(cnt, dst, ys, cols, h, nf)


def kernel(x_prompt, x_sample, cache_win_k, cache_win_v, state_conv, state_pool, norm_mix, w_in, attn_sinks,
           w_attn_out, conv_w, w_conv_out, w_pool, pool_scale, w_out, norm_ffn, ffn_w_gate, ffn_w_up, ffn_w_down,
           moe_router, moe_w_gate, moe_w_up, moe_w_down, norm_final):
    assert DEPTH == 2 and ffn_w_gate.shape[0] == 1 and moe_w_gate.shape[0] == 1
    norm3 = norm_mix.reshape(DEPTH, 1, D_MODEL)
    scale3 = pool_scale.reshape(DEPTH, 1, D_MODEL)
    w_pool3 = w_pool.reshape(DEPTH, D_POOL, POOL_OUT_GROUP)
    pool_shape = (len(POOL_WINDOWS), POOL_GROUP, POOL_OUT_GROUP)
    mixer_bf16 = [(w_in[0].astype(BF16), w_attn_out[0].astype(BF16), w_conv_out[0].astype(BF16),
                   w_pool[0].astype(BF16), w_out[0].astype(BF16))]
    ffn_casts = ((ffn_w_gate, 0, 32), (ffn_w_up, 0, 32), (ffn_w_down, 0, 16))
    later_casts = (
        (moe_w_gate.reshape(1, N_EXPERTS * D_MODEL, D_EXPERT), 0, CAST_STEPS),
        (moe_w_up.reshape(1, N_EXPERTS * D_MODEL, D_EXPERT), 0, CAST_STEPS),
        (moe_w_down.reshape(1, N_EXPERTS * D_EXPERT, D_MODEL), 0, CAST_STEPS),
        (w_in, 1, 64), (w_attn_out, 1, 32), (w_conv_out, 1, 32), (w_pool3, 1, 32), (w_out, 1, 64))
    kc = cache_win_k.reshape(DEPTH, DEC_BATCH, WINDOW, D_KV)
    vc = cache_win_v.reshape(DEPTH, DEC_BATCH, WINDOW, D_KV)

    xp = x_prompt.reshape(N_PROMPT, D_MODEL)
    xs, xs_row0 = x_sample.reshape(N_SAMPLE, D_MODEL), 0
    p_states, s_states = (), ()
    for l in range(DEPTH):
        sinks = attn_sinks[l]
        wi, wao, wco, wp, wo = mixer_bf16[l]
        mix_weights = (norm3, wi, wao, conv_w, wco, wp, scale3, wo)
        h, p_states, ffn_bf16 = _prompt_mix(l, xp, sinks, mix_weights, p_states, ffn_casts if l == 0 else ())
        h, s_states = _sample_mix(l, xs, xs_row0, sinks, kc, vc, state_conv, state_pool, mix_weights, h, s_states)
        i = l // 2
        nw = norm_ffn[l].reshape(1, D_MODEL)
        if l % 2 == 0:
            xp, (eg, eu, ed, wi, wao, wco, wp, wo) = _ffn_dense(h, nw, *ffn_bf16, later_casts)
            mixer_bf16.append((wi, wao, wco, wp.reshape(pool_shape), wo))
            xs, xs_row0 = xp, N_PROMPT
        else:
            hn, slot_row, cols, cnt = _router(h, nw, moe_router[i].T)
            cnt = cnt[:, :, 0].astype(jnp.int32)
            dst, tail, tile_expert, n_tiles = _plan(cnt)
            cnt = cnt.reshape(-1)
            xs_sorted = _dispatch(cnt, dst, tail, hn, slot_row)
            ys_sorted = _experts(tile_expert, n_tiles, xs_sorted,
                                 eg.reshape(N_EXPERTS, D_MODEL, D_EXPERT), eu.reshape(N_EXPERTS, D_MODEL, D_EXPERT),
                                 ed.reshape(N_EXPERTS, D_EXPERT, D_MODEL))
            y_prompt, y_sample = _combine(cnt, dst, ys_sorted, cols, h, norm_final.reshape(1, D_MODEL))

    pk, pv, pc, pp = p_states
    sk, sv, sc, sp = s_states
    kv_shape_p = (DEPTH, BATCH, WINDOW, N_KV_HEADS, HEAD_DIM)
    kv_shape_s = (DEPTH, DEC_BATCH, WINDOW, N_KV_HEADS, HEAD_DIM)
    return (y_prompt.reshape(BATCH, SEQ, D_MODEL), y_sample.reshape(DEC_BATCH, DEC_SEQ, D_MODEL),
            pk.reshape(kv_shape_p), pv.reshape(kv_shape_p), pc, pp,
            sk.reshape(kv_shape_s), sv.reshape(kv_shape_s), sc, sp)
```

```python
import functools
import math

import jax
import jax.numpy as jnp
from jax import lax
from jax.experimental import pallas as pl
from jax.experimental.pallas import tpu as pltpu

F32 = jnp.float32
BF16 = jnp.bfloat16

D_MODEL = 1024
BATCH = 4
SEQ = 4096
DEPTH = 2
DEC_BATCH = 128
DEC_SEQ = 8
HEAD_DIM = 64
N_Q_HEADS = 8
N_KV_HEADS = 2
GQA_GROUP = N_Q_HEADS // N_KV_HEADS
D_ATTN = N_Q_HEADS * HEAD_DIM
D_KV = N_KV_HEADS * HEAD_DIM
WINDOW = 128
ATTN_SCALE = HEAD_DIM ** -0.5
D_CONV = D_MODEL // 2
CONV_WIDTH = 3
D_POOL = D_MODEL // 2
POOL_WINDOWS = (2, 4, 8, 16)
POOL_GROUP = D_POOL // len(POOL_WINDOWS)
POOL_OUT_GROUP = D_MODEL // len(POOL_WINDOWS)
POOL_BUF = max(POOL_WINDOWS) - 1
OFF_Q = 0
OFF_K = OFF_Q + D_ATTN
OFF_V = OFF_K + D_KV
OFF_CX = OFF_V + D_KV
OFF_CB = OFF_CX + D_CONV
OFF_CC = OFF_CB + D_CONV
OFF_P = OFF_CC + D_CONV
OFF_G = OFF_P + D_POOL
D_IN = OFF_G + 3 * D_MODEL
D_FF = 2816
N_EXPERTS = 8
D_EXPERT = 3584
EPS = 1e-5
LOG2E = math.log2(math.e)

N_PROMPT = BATCH * SEQ
N_SAMPLE = DEC_BATCH * DEC_SEQ
N_TOK = N_PROMPT + N_SAMPLE

PROMPT_TILE = 512
SAMPLE_SEQS = 32
SAMPLE_ROWS = SAMPLE_SEQS * DEC_SEQ
ATT_GROUP = 8
FFN_TILE = 1024
EXP_CHUNK = D_EXPERT // 2
SUB_ROWS = 256
CAST_STEPS = 64
ROW_BLOCK = 128
HEAD_BLOCK = 256
ROW_ALIGN = 16
EXP_TILE = 512
N_UNITS = N_TOK // FFN_TILE
MAX_TILES = -(-(2 * N_TOK + N_UNITS * N_EXPERTS * (ROW_ALIGN - 1)) // EXP_TILE) + N_EXPERTS
N_PAD = N_TOK
SEG_BITS = 7
assert FFN_TILE == ROW_ALIGN << (SEG_BITS - 1)
VMEM_LIMIT = 58 * 1024 * 1024


def _dot(a, b):
    return jnp.dot(a, b, preferred_element_type=F32)


def _dot_nt(a, b):
    return lax.dot_general(a, b, (((1,), (1,)), ((), ())), preferred_element_type=F32)


def _rms(x, g):
    return x * lax.rsqrt(jnp.mean(x * x, axis=-1, keepdims=True) + EPS) * g


def _sigmoid(x):
    return 0.5 * jnp.tanh(0.5 * x) + 0.5


def _head_slope(h):
    return float(2.0 ** (-8.0 * (h + 1) / N_Q_HEADS))


def _z_part(nb, win_ref, i):
    if i == 0:
        return (_dot(nb, win_ref[:, OFF_CX:OFF_P]),)
    if i == 1:
        return (_dot(nb, win_ref[:, OFF_P:OFF_G]), _dot(nb, win_ref[:, OFF_G:OFF_G + D_MODEL]))
    return (_dot(nb, win_ref[:, OFF_G + (i - 1) * D_MODEL:OFF_G + i * D_MODEL]),)


def _mix_tail(x, att_bf, cb, yc, d_groups, gate_logits, wao_ref, wco_ref, wp_ref, ps_ref, wo_ref):
    att_o = _dot(att_bf, wao_ref[...])
    merged = _sigmoid(gate_logits[0]) * att_o
    conv_o = _dot((cb * yc).astype(BF16), wco_ref[...])
    merged = merged + _sigmoid(gate_logits[1]) * conv_o
    pool_o = jnp.concatenate([_dot(d.astype(BF16), wp_ref[g]) for g, d in enumerate(d_groups)], axis=-1)
    pool_o = pool_o * ps_ref[...]
    merged = merged + _sigmoid(gate_logits[2]) * pool_o
    return x + _dot(merged.astype(BF16), wo_ref[...])


def _prompt_mix_kernel(*refs, n_alias, n_cast):
    (sinks_ref, x_ref, nw_ref, win_ref, wao_ref, cw_ref, wco_ref, wp_ref, ps_ref, wo_ref) = refs[:10]
    cast_src = refs[10 + n_alias:10 + n_alias + n_cast]
    outs = refs[10 + n_alias + n_cast:]
    h_ref, kw_ref, vw_ref, cs_ref, pst_ref = outs[:5]
    cast_dst = outs[5:5 + n_cast]
    qbuf, kbuf, vbuf, att_ref, ubuf, pbuf, bias_ref = outs[5 + n_cast:]
    _run_casts(cast_src, cast_dst)
    tm = PROMPT_TILE
    b = pl.program_id(0)
    t = pl.program_id(1)
    x = x_ref[...]
    nb = _rms(x, nw_ref[...]).astype(BF16)
    lane = lax.broadcasted_iota(jnp.int32, (tm, D_KV), 1)
    low = lane < HEAD_DIM

    @pl.when(jnp.logical_and(b == 0, t == 0))
    def _():
        qi = lax.broadcasted_iota(jnp.int32, (WINDOW, 2 * WINDOW), 0)
        si = lax.broadcasted_iota(jnp.int32, (WINDOW, 2 * WINDOW), 1)
        dist_i = qi + WINDOW - si
        dist = dist_i.astype(F32)
        band = jnp.logical_and(dist_i >= 0, dist_i <= WINDOW)
        for h in range(N_Q_HEADS):
            bias = jnp.where(band, (-_head_slope(h) * LOG2E) * dist, -jnp.inf)
            bias_ref[0, h] = bias
            bias_ref[1, h] = jnp.where(si >= WINDOW, bias, -jnp.inf)

    @pl.when(t == 0)
    def _():
        for i in range(4):
            kbuf[i, 0:WINDOW, :] = jnp.zeros((WINDOW, D_KV), BF16)
            vbuf[i, 0:WINDOW, :] = jnp.zeros((WINDOW, D_KV), BF16)
        ubuf[0:8, :] = jnp.zeros((8, D_CONV), F32)
        pbuf[0:16, :] = jnp.zeros((16, D_POOL), F32)

    @pl.when(t > 0)
    def _():
        for i in range(4):
            kbuf[i, 0:WINDOW, :] = kbuf[i, tm:tm + WINDOW, :]
            vbuf[i, 0:WINDOW, :] = vbuf[i, tm:tm + WINDOW, :]
        ubuf[0:8, :] = ubuf[tm:tm + 8, :]
        pbuf[0:16, :] = pbuf[tm:tm + 16, :]

    qkv = _dot(nb, win_ref[:, OFF_Q:OFF_CX])
    qbuf[...] = (qkv[:, OFF_Q:OFF_K] * (ATTN_SCALE * LOG2E)).astype(BF16)
    k = qkv[:, OFF_K:OFF_V]
    v = qkv[:, OFF_V:OFF_CX]
    kw_ref[...] = k[tm - WINDOW:, :]
    vw_ref[...] = v[tm - WINDOW:, :]
    k_sw = pltpu.roll(k, HEAD_DIM, 1)
    v_sw = pltpu.roll(v, HEAD_DIM, 1)
    k_var = (jnp.where(low, k, 0.0), jnp.where(low, 0.0, k_sw), jnp.where(low, k_sw, 0.0), jnp.where(low, 0.0, k))
    v_var = (jnp.where(low, v, 1.0), jnp.where(low, 1.0, v_sw), jnp.where(low, v_sw, 1.0), jnp.where(low, 1.0, v))
    for i in range(4):
        kbuf[i, WINDOW:WINDOW + tm, :] = k_var[i].astype(BF16)
        vbuf[i, WINDOW:WINDOW + tm, :] = v_var[i].astype(BF16)

    lane_q = lax.broadcasted_iota(jnp.int32, (WINDOW, 2 * HEAD_DIM), 1)
    low_q = lane_q < HEAD_DIM

    seq_start = jnp.where(t == 0, 1, 0)
    variant = [2 * (h // GQA_GROUP) + h % 2 for h in range(N_Q_HEADS)]
    units = [(j, h) for j in range(tm // WINDOW) for h in range(N_Q_HEADS)]
    z_parts = [_z_part(nb, win_ref, 0)]
    scores = {}
    for j, h in units:
        r0 = j * WINDOW
        qp = qbuf[r0:r0 + WINDOW, (h // 2) * 2 * HEAD_DIM:(h // 2 + 1) * 2 * HEAD_DIM]
        first = seq_start if j == 0 else 0
        scores[j, h] = _dot_nt(qp, kbuf[variant[h], r0:r0 + 2 * WINDOW, :]) + bias_ref[first, h]
    z_parts.append(_z_part(nb, win_ref, 1))
    probs = {}
    for j, h in units:
        sink = sinks_ref[h] * LOG2E
        m = jnp.maximum(jnp.max(scores[j, h], axis=-1, keepdims=True), sink)
        probs[j, h] = (jnp.exp2(scores[j, h] - m).astype(BF16), jnp.exp2(sink - m))
    z_parts.append(_z_part(nb, win_ref, 2))
    outs = {}
    for j, h in units:
        r0 = j * WINDOW
        p, sink_term = probs[j, h]
        o = _dot(p, vbuf[variant[h], r0:r0 + 2 * WINDOW, :])
        den = (o[:, HEAD_DIM:HEAD_DIM + 1] if h % 2 == 0 else o[:, 0:1]) + sink_term
        outs[j, h] = o / den
    z_parts.append(_z_part(nb, win_ref, 3))
    for j in range(tm // WINDOW):
        for pair in range(N_Q_HEADS // 2):
            att_ref[j * WINDOW:(j + 1) * WINDOW, pair * 2 * HEAD_DIM:(pair + 1) * 2 * HEAD_DIM] = (
                jnp.where(low_q, outs[j, 2 * pair], outs[j, 2 * pair + 1]).astype(BF16))

    (conv,), (pool_in, gate_att), (gate_conv,), (gate_pool,) = z_parts
    cx = conv[:, 0:D_CONV]
    cb = conv[:, D_CONV:2 * D_CONV]
    ubuf[8:8 + tm, :] = conv[:, 2 * D_CONV:3 * D_CONV] * cx
    yc = (cw_ref[0:1, :] * ubuf[6:6 + tm, :] + cw_ref[1:2, :] * ubuf[7:7 + tm, :]
          + cw_ref[2:3, :] * ubuf[8:8 + tm, :])
    cs_ref[...] = ubuf[tm + 6:tm + 8, :]

    pbuf[16:16 + tm, :] = pool_in
    pos = lax.broadcasted_iota(jnp.int32, (tm, 1), 0) + t * tm + 1
    d_groups = []
    for g, w in enumerate(POOL_WINDOWS):
        c0 = g * POOL_GROUP
        cur = pbuf[16:16 + tm, c0:c0 + POOL_GROUP]
        tot = cur
        for j in range(1, w):
            tot = tot + pbuf[16 - j:16 - j + tm, c0:c0 + POOL_GROUP]
        cnt = jnp.minimum(pos, w).astype(F32)
        d_groups.append(tot / cnt - cur)
    pst_ref[...] = pbuf[tm + 1:tm + 16, :]

    h_ref[...] = _mix_tail(x, att_ref[...], cb, yc, d_groups, (gate_att, gate_conv, gate_pool),
                           wao_ref, wco_ref, wp_ref, ps_ref, wo_ref)


def _weight_specs(l, grid_rank):
    def const(*idx):
        if grid_rank == 1:
            return lambda i: idx
        return lambda b, t: idx

    return [
        pl.BlockSpec((None, 1, D_MODEL), const(l, 0, 0)),
        pl.BlockSpec((D_MODEL, D_IN), const(0, 0)),
        pl.BlockSpec((D_ATTN, D_MODEL), const(0, 0)),
        pl.BlockSpec((None, CONV_WIDTH, D_CONV), const(l, 0, 0)),
        pl.BlockSpec((D_CONV, D_MODEL), const(0, 0)),
        pl.BlockSpec((len(POOL_WINDOWS), POOL_GROUP, POOL_OUT_GROUP), const(0, 0, 0)),
        pl.BlockSpec((None, 1, D_MODEL), const(l, 0, 0)),
        pl.BlockSpec((D_MODEL, D_MODEL), const(0, 0)),
    ]


def _cast_job(src, lead, n_steps, step_of):
    rows, cols = src.shape[1:]
    r = rows // n_steps
    assert r * n_steps == rows and r % ROW_ALIGN == 0

    def block(*g):
        return jnp.minimum(step_of(*g), n_steps - 1)

    return (pl.BlockSpec((None, r, cols), lambda *g: (lead, block(*g), 0)),
            pl.BlockSpec((r, cols), lambda *g: (block(*g), 0)),
            jax.ShapeDtypeStruct((rows, cols), BF16))


def _run_casts(src_refs, dst_refs):
    for s, d in zip(src_refs, dst_refs):
        d[...] = s[...].astype(BF16)


def _prompt_mix(l, x, sinks, weights, prev_states, casts=()):
    tm = PROMPT_TILE
    nt = SEQ // tm
    n_alias = len(prev_states)
    jobs = [_cast_job(src, lead, n, lambda b, t: b * nt + t) for src, lead, n in casts]
    in_specs = ([pl.BlockSpec(memory_space=pltpu.SMEM),
                 pl.BlockSpec((tm, D_MODEL), lambda b, t: (b * nt + t, 0))] + _weight_specs(l, 2)
                + [pl.BlockSpec(memory_space=pl.ANY)] * n_alias + [j[0] for j in jobs])
    out_specs = [
        pl.BlockSpec((tm, D_MODEL), lambda b, t: (b * nt + t, 0)),
        pl.BlockSpec((None, None, WINDOW, D_KV), lambda b, t: (l, b, 0, 0)),
        pl.BlockSpec((None, None, WINDOW, D_KV), lambda b, t: (l, b, 0, 0)),
        pl.BlockSpec((None, None, CONV_WIDTH - 1, D_CONV), lambda b, t: (l, b, 0, 0)),
        pl.BlockSpec((None, None, POOL_BUF, D_POOL), lambda b, t: (l, b, 0, 0)),
    ]
    out_shape = [
        jax.ShapeDtypeStruct((N_PAD, D_MODEL), F32),
        jax.ShapeDtypeStruct((DEPTH, BATCH, WINDOW, D_KV), F32),
        jax.ShapeDtypeStruct((DEPTH, BATCH, WINDOW, D_KV), F32),
        jax.ShapeDtypeStruct((DEPTH, BATCH, CONV_WIDTH - 1, D_CONV), F32),
        jax.ShapeDtypeStruct((DEPTH, BATCH, POOL_BUF, D_POOL), F32),
    ]
    scratch = [
        pltpu.VMEM((tm, D_ATTN), BF16),
        pltpu.VMEM((4, WINDOW + tm, D_KV), BF16),
        pltpu.VMEM((4, WINDOW + tm, D_KV), BF16),
        pltpu.VMEM((tm, D_ATTN), BF16),
        pltpu.VMEM((8 + tm, D_CONV), F32),
        pltpu.VMEM((16 + tm, D_POOL), F32),
        pltpu.VMEM((2, N_Q_HEADS, WINDOW, 2 * WINDOW), F32),
    ]
    outs = pl.pallas_call(
        functools.partial(_prompt_mix_kernel, n_alias=n_alias, n_cast=len(jobs)),
        grid=(BATCH, nt),
        in_specs=in_specs,
        out_specs=out_specs + [j[1] for j in jobs],
        out_shape=out_shape + [j[2] for j in jobs],
        scratch_shapes=scratch,
        input_output_aliases={10 + i: 1 + i for i in range(n_alias)},
        compiler_params=pltpu.CompilerParams(
            dimension_semantics=("arbitrary", "arbitrary"), vmem_limit_bytes=VMEM_LIMIT),
        name=f"prompt_mix_l{l}",
    )(sinks, x, *weights, *prev_states, *[c[0] for c in casts])
    return outs[0], tuple(outs[1:5]), tuple(outs[5:])


def _sample_mix_kernel(*refs, n_alias):
    (sinks_ref, x_ref, kc_ref, vc_ref, cst_ref, pin_ref,
     nw_ref, win_ref, wao_ref, cw_ref, wco_ref, wp_ref, ps_ref, wo_ref) = refs[:14]
    (h_ref, ko_ref, vo_ref, co_ref, po_ref,
     qbuf, knb, vnb, att_ref, cbuf, ebuf, bias_c, bias_n) = refs[14 + n_alias:]
    ns, nr, T = SAMPLE_SEQS, SAMPLE_ROWS, DEC_SEQ
    x = x_ref[...]
    nb = _rms(x, nw_ref[...]).astype(BF16)

    qkv = _dot(nb, win_ref[:, OFF_Q:OFF_CX])
    qbuf[...] = (qkv[:, OFF_Q:OFF_K] * ATTN_SCALE).astype(BF16)
    k = qkv[:, OFF_K:OFF_V]
    v = qkv[:, OFF_V:OFF_CX]
    knb[...] = k.astype(BF16)
    vnb[...] = v.astype(BF16)
    ko_ref[:, 0:WINDOW - T, :] = kc_ref[:, T:WINDOW, :]
    vo_ref[:, 0:WINDOW - T, :] = vc_ref[:, T:WINDOW, :]
    ko_ref[:, WINDOW - T:WINDOW, :] = k.reshape(ns, T, D_KV)
    vo_ref[:, WINDOW - T:WINDOW, :] = v.reshape(ns, T, D_KV)

    gr = ATT_GROUP * T
    gc = ATT_GROUP * WINDOW

    @pl.when(pl.program_id(0) == 0)
    def _():
        rq = lax.broadcasted_iota(jnp.int32, (gr, gc), 0)
        cq = lax.broadcasted_iota(jnp.int32, (gr, gc), 1)
        tq = rq & (T - 1)
        sc_pos = cq & (WINDOW - 1)
        valid_c = jnp.logical_and((rq >> 3) == (cq >> 7), sc_pos >= tq)
        dist_c = (WINDOW + tq - sc_pos).astype(F32)
        rn = lax.broadcasted_iota(jnp.int32, (gr, gr), 0)
        cn = lax.broadcasted_iota(jnp.int32, (gr, gr), 1)
        tn = rn & (T - 1)
        jn = cn & (T - 1)
        valid_n = jnp.logical_and((rn >> 3) == (cn >> 3), jn <= tn)
        dist_n = (tn - jn).astype(F32)
        for h in range(N_Q_HEADS):
            bias_c[h] = jnp.where(valid_c, -_head_slope(h) * dist_c, -jnp.inf)
            bias_n[h] = jnp.where(valid_n, -_head_slope(h) * dist_n, -jnp.inf)

    z_parts = []
    for gi in range(ns // ATT_GROUP):
        z_parts.append(_z_part(nb, win_ref, gi))
        r0 = gi * gr
        s0 = gi * ATT_GROUP
        qg = qbuf[r0:r0 + gr, :]
        kcg = kc_ref[s0:s0 + ATT_GROUP, :, :].reshape(gc, D_KV).astype(BF16)
        vcg = vc_ref[s0:s0 + ATT_GROUP, :, :].reshape(gc, D_KV).astype(BF16)
        kng = knb[r0:r0 + gr, :]
        vng = vnb[r0:r0 + gr, :]
        kvs = [(h // GQA_GROUP) * HEAD_DIM for h in range(N_Q_HEADS)]
        scores = []
        for h in range(N_Q_HEADS):
            qh = qg[:, h * HEAD_DIM:(h + 1) * HEAD_DIM]
            scores.append((_dot_nt(qh, kcg[:, kvs[h]:kvs[h] + HEAD_DIM]) + bias_c[h],
                           _dot_nt(qh, kng[:, kvs[h]:kvs[h] + HEAD_DIM]) + bias_n[h]))
        probs = []
        for h in range(N_Q_HEADS):
            s_c, s_n = scores[h]
            sink = sinks_ref[h]
            m = jnp.maximum(jnp.maximum(jnp.max(s_c, axis=-1, keepdims=True),
                                        jnp.max(s_n, axis=-1, keepdims=True)), sink)
            p_c = jnp.exp(s_c - m)
            p_n = jnp.exp(s_n - m)
            den = jnp.exp(sink - m) + jnp.sum(p_c, axis=-1, keepdims=True) + jnp.sum(p_n, axis=-1, keepdims=True)
            probs.append((p_c.astype(BF16), p_n.astype(BF16), den))
        outs = []
        for h in range(N_Q_HEADS):
            p_c, p_n, den = probs[h]
            o = _dot(p_c, vcg[:, kvs[h]:kvs[h] + HEAD_DIM]) + _dot(p_n, vng[:, kvs[h]:kvs[h] + HEAD_DIM])
            outs.append(o / den)
        att_ref[r0:r0 + gr, :] = jnp.concatenate(outs, axis=-1).astype(BF16)

    (conv,), (pool_in, gate_att), (gate_conv,), (gate_pool,) = z_parts
    cx = conv[:, 0:D_CONV]
    cb = conv[:, D_CONV:2 * D_CONV]
    u = conv[:, 2 * D_CONV:3 * D_CONV] * cx
    cbuf[:, 6:8, :] = cst_ref[...]
    cbuf[:, 8:16, :] = u.reshape(ns, T, D_CONV)
    w0 = cw_ref[0:1, :].reshape(1, 1, D_CONV)
    w1 = cw_ref[1:2, :].reshape(1, 1, D_CONV)
    w2 = cw_ref[2:3, :].reshape(1, 1, D_CONV)
    yc = (w0 * cbuf[:, 6:14, :] + w1 * cbuf[:, 7:15, :] + w2 * cbuf[:, 8:16, :]).reshape(nr, D_CONV)
    co_ref[...] = cbuf[:, 14:16, :]

    ebuf[:, 1:16, :] = pin_ref[...]
    ebuf[:, 16:24, :] = pool_in.reshape(ns, T, D_POOL)
    d_groups = []
    for g, w in enumerate(POOL_WINDOWS):
        c0 = g * POOL_GROUP
        cur = ebuf[:, 16:24, c0:c0 + POOL_GROUP]
        tot = cur
        for j in range(1, w):
            tot = tot + ebuf[:, 16 - j:24 - j, c0:c0 + POOL_GROUP]
        d_groups.append((tot / float(w) - cur).reshape(nr, POOL_GROUP))
    po_ref[...] = ebuf[:, 9:24, :]

    h_ref[...] = _mix_tail(x, att_ref[...], cb, yc, d_groups, (gate_att, gate_conv, gate_pool),
                           wao_ref, wco_ref, wp_ref, ps_ref, wo_ref)


def _sample_mix(l, x, x_row0, sinks, kc, vc, cst, pst, weights, h_buf, prev_states):
    ns, nr = SAMPLE_SEQS, SAMPLE_ROWS
    n_alias = 1 + len(prev_states)
    xb0 = x_row0 // nr
    hb0 = N_PROMPT // nr
    in_specs = [
        pl.BlockSpec(memory_space=pltpu.SMEM),
        pl.BlockSpec((nr, D_MODEL), lambda i: (xb0 + i, 0)),
        pl.BlockSpec((None, ns, WINDOW, D_KV), lambda i: (l, i, 0, 0)),
        pl.BlockSpec((None, ns, WINDOW, D_KV), lambda i: (l, i, 0, 0)),
        pl.BlockSpec((None, ns, CONV_WIDTH - 1, D_CONV), lambda i: (l, i, 0, 0)),
        pl.BlockSpec((None, ns, POOL_BUF, D_POOL), lambda i: (l, i, 0, 0)),
    ] + _weight_specs(l, 1) + [pl.BlockSpec(memory_space=pl.ANY)] * n_alias
    out_specs = [
        pl.BlockSpec((nr, D_MODEL), lambda i: (hb0 + i, 0)),
        pl.BlockSpec((None, ns, WINDOW, D_KV), lambda i: (l, i, 0, 0)),
        pl.BlockSpec((None, ns, WINDOW, D_KV), lambda i: (l, i, 0, 0)),
        pl.BlockSpec((None, ns, CONV_WIDTH - 1, D_CONV), lambda i: (l, i, 0, 0)),
        pl.BlockSpec((None, ns, POOL_BUF, D_POOL), lambda i: (l, i, 0, 0)),
    ]
    out_shape = [
        jax.ShapeDtypeStruct((N_PAD, D_MODEL), F32),
        jax.ShapeDtypeStruct((DEPTH, DEC_BATCH, WINDOW, D_KV), F32),
        jax.ShapeDtypeStruct((DEPTH, DEC_BATCH, WINDOW, D_KV), F32),
        jax.ShapeDtypeStruct((DEPTH, DEC_BATCH, CONV_WIDTH - 1, D_CONV), F32),
        jax.ShapeDtypeStruct((DEPTH, DEC_BATCH, POOL_BUF, D_POOL), F32),
    ]
    scratch = [
        pltpu.VMEM((nr, D_ATTN), BF16),
        pltpu.VMEM((nr, D_KV), BF16),
        pltpu.VMEM((nr, D_KV), BF16),
        pltpu.VMEM((nr, D_ATTN), BF16),
        pltpu.VMEM((ns, 16, D_CONV), F32),
        pltpu.VMEM((ns, 24, D_POOL), F32),
        pltpu.VMEM((N_Q_HEADS, ATT_GROUP * DEC_SEQ, ATT_GROUP * WINDOW), F32),
        pltpu.VMEM((N_Q_HEADS, ATT_GROUP * DEC_SEQ, ATT_GROUP * DEC_SEQ), F32),
    ]
    outs = pl.pallas_call(
        functools.partial(_sample_mix_kernel, n_alias=n_alias),
        grid=(DEC_BATCH // ns,),
        in_specs=in_specs,
        out_specs=out_specs,
        out_shape=out_shape,
        scratch_shapes=scratch,
        input_output_aliases={14 + i: i for i in range(n_alias)},
        compiler_params=pltpu.CompilerParams(
            dimension_semantics=("arbitrary",), vmem_limit_bytes=VMEM_LIMIT),
        name=f"sample_mix_l{l}",
    )(sinks, x, kc, vc, cst, pst, *weights, h_buf, *prev_states)
    return outs[0], tuple(outs[1:])


def _ffn_kernel(*refs, n_cast):
    h_ref, nw_ref, wg_ref, wu_ref, wd_ref = refs[:5]
    o_ref = refs[5 + n_cast]
    h = h_ref[...]
    hn = _rms(h, nw_ref[...]).astype(BF16)
    g = _dot(hn, wg_ref[...])
    u = _dot(hn, wu_ref[...])
    o_ref[...] = h + _dot((g * _sigmoid(g) * u).astype(BF16), wd_ref[...])
    _run_casts(refs[5:5 + n_cast], refs[6 + n_cast:])


def _ffn_dense(h, nw, wg, wu, wd, casts):
    tm = SUB_ROWS
    steps = N_TOK // tm
    assert all(n <= steps for _, _, n in casts)
    jobs = [_cast_job(src, lead, n, lambda i: i) for src, lead, n in casts]
    resident = dict(pipeline_mode=pl.Buffered(1))
    outs = pl.pallas_call(
        functools.partial(_ffn_kernel, n_cast=len(jobs)),
        grid=(steps,),
        in_specs=[
            pl.BlockSpec((tm, D_MODEL), lambda i: (i, 0)),
            pl.BlockSpec((1, D_MODEL), lambda i: (0, 0)),
            pl.BlockSpec((D_MODEL, D_FF), lambda i: (0, 0), **resident),
            pl.BlockSpec((D_MODEL, D_FF), lambda i: (0, 0), **resident),
            pl.BlockSpec((D_FF, D_MODEL), lambda i: (0, 0), **resident),
        ] + [j[0] for j in jobs],
        out_specs=[pl.BlockSpec((tm, D_MODEL), lambda i: (i, 0))] + [j[1] for j in jobs],
        out_shape=[jax.ShapeDtypeStruct((N_PAD, D_MODEL), F32)] + [j[2] for j in jobs],
        compiler_params=pltpu.CompilerParams(
            dimension_semantics=("arbitrary",), vmem_limit_bytes=VMEM_LIMIT),
        name="ffn_dense",
    )(h, nw, wg, wu, wd, *[c[0] for c in casts])
    return outs[0], tuple(outs[1:])


def _router_kernel(h_ref, nw_ref, rt_ref, hn_ref, slot_ref, col_ref, cnt_ref):
    tm = FFN_TILE
    hn = _rms(h_ref[...], nw_ref[...])
    hn_hi = hn.astype(BF16)
    hn_ref[...] = hn_hi
    hn_lo = (hn - hn_hi.astype(F32)).astype(BF16)
    rt = rt_ref[...]
    rt_hi = rt.astype(BF16)
    rt_lo = (rt - rt_hi.astype(F32)).astype(BF16)
    rt16 = jnp.concatenate([rt_hi, rt_lo], axis=0)
    by_hi = _dot_nt(rt16, hn_hi)
    logits = by_hi[0:N_EXPERTS] + by_hi[N_EXPERTS:2 * N_EXPERTS] + _dot_nt(rt16, hn_lo)[0:N_EXPERTS]
    eidx = lax.broadcasted_iota(jnp.int32, (N_EXPERTS, tm), 0).astype(F32)
    none = float(N_EXPERTS)
    m1 = jnp.max(logits, axis=0, keepdims=True)
    i1 = jnp.min(jnp.where(logits == m1, eidx, none), axis=0, keepdims=True)
    rest = jnp.where(eidx == i1, -jnp.inf, logits)
    m2 = jnp.max(rest, axis=0, keepdims=True)
    i2 = jnp.min(jnp.where(rest == m2, eidx, none), axis=0, keepdims=True)
    e2 = jnp.exp(m2 - m1)
    w1 = 1.0 / (1.0 + e2)
    w2 = e2 / (1.0 + e2)
    sel1 = eidx == i1
    sel2 = eidx == i2
    gate = jnp.where(sel1, w1, jnp.where(sel2, w2, 0.0))
    chosen = jnp.logical_or(sel1, sel2)
    mask = jnp.where(chosen, 1.0, 0.0)
    srow = lax.broadcasted_iota(jnp.int32, (tm, tm), 0)
    scol = lax.broadcasted_iota(jnp.int32, (tm, tm), 1)
    upper = jnp.where(srow < scol, 1.0, 0.0).astype(BF16)
    mask16 = jnp.concatenate([mask, jnp.zeros_like(mask)], axis=0).astype(BF16)
    slot = jnp.where(chosen, _dot(mask16, upper)[0:N_EXPERTS, :], -1.0)
    slot_ref[...] = slot.astype(jnp.int32)
    cnt_ref[...] = jnp.broadcast_to(jnp.sum(mask, axis=1, keepdims=True), (N_EXPERTS, 128))
    both = jnp.concatenate([slot, gate, jnp.zeros((128 - 2 * N_EXPERTS, tm), F32)], axis=0)
    col_ref[...] = both.T[:, 0:2 * N_EXPERTS]


def _router(h, nw, router_t):
    tm = FFN_TILE
    nt = N_UNITS
    return pl.pallas_call(
        _router_kernel,
        grid=(nt,),
        in_specs=[
            pl.BlockSpec((tm, D_MODEL), lambda i: (i, 0)),
            pl.BlockSpec((1, D_MODEL), lambda i: (0, 0)),
            pl.BlockSpec((N_EXPERTS, D_MODEL), lambda i: (0, 0)),
        ],
        out_specs=[
            pl.BlockSpec((tm, D_MODEL), lambda i: (i, 0)),
            pl.BlockSpec((N_EXPERTS, tm), lambda i: (0, i)),
            pl.BlockSpec((tm, 2 * N_EXPERTS), lambda i: (i, 0)),
            pl.BlockSpec((None, N_EXPERTS, 128), lambda i: (i, 0, 0)),
        ],
        out_shape=[
            jax.ShapeDtypeStruct((N_PAD, D_MODEL), BF16),
            jax.ShapeDtypeStruct((N_EXPERTS, N_PAD), jnp.int32),
            jax.ShapeDtypeStruct((N_PAD, 2 * N_EXPERTS), F32),
            jax.ShapeDtypeStruct((nt, N_EXPERTS, 128), F32),
        ],
        compiler_params=pltpu.CompilerParams(
            dimension_semantics=("arbitrary",), vmem_limit_bytes=VMEM_LIMIT),
        name="moe_router",
    )(h, nw, router_t)


def _plan(cnt):
    seg = -(-cnt // ROW_ALIGN) * ROW_ALIGN
    rows_e = jnp.sum(seg, axis=0)
    tiles_e = -(-rows_e // EXP_TILE)
    cum_tiles = jnp.cumsum(tiles_e)
    base_e = (cum_tiles - tiles_e) * EXP_TILE
    dst = base_e[None, :] + jnp.cumsum(seg, axis=0) - seg
    tail = jnp.concatenate([base_e + rows_e, tiles_e * EXP_TILE - rows_e]).astype(jnp.int32)
    n_tiles = cum_tiles[-1]
    i = jnp.arange(MAX_TILES, dtype=jnp.int32)
    tile_expert = jnp.sum((cum_tiles[None, :] <= i[:, None]).astype(jnp.int32), axis=1)
    tile_expert = jnp.minimum(tile_expert, N_EXPERTS - 1)
    tile_expert = jnp.where(i < n_tiles, tile_expert, tile_expert[jnp.maximum(n_tiles - 1, 0)])
    return dst.reshape(-1).astype(jnp.int32), tail, tile_expert, n_tiles.reshape(1).astype(jnp.int32)


def _ceil_blocks(n, block):
    return (n + block - 1) >> (block.bit_length() - 1)


def _segment_copies(n_rows, src_at, dst_at, sem, max_bits=SEG_BITS):
    n = n_rows >> (ROW_ALIGN.bit_length() - 1)
    out = []
    for bit in range(max_bits - 1, -1, -1):
        size = ROW_ALIGN << bit
        off = pl.multiple_of(((n >> (bit + 1)) << (bit + 1)) * ROW_ALIGN, ROW_ALIGN)
        out.append((((n >> bit) & 1) == 1, pltpu.make_async_copy(src_at(off, size), dst_at(off, size), sem)))
    return out


def _start_all(copies):
    for pred, cp in copies:
        pl.when(pred)(cp.start)


def _wait_all(copies):
    for pred, cp in copies:
        pl.when(pred)(cp.wait)


def _dispatch_kernel(cnt_ref, dst_ref, tail_ref, hn_ref, srow_ref, xs_ref, stage, sems):
    tu, rb = FFN_TILE, ROW_BLOCK
    t = pl.program_id(0)

    def copies(e):
        slot = e % 2
        n_rows = _ceil_blocks(cnt_ref[t * N_EXPERTS + e], ROW_ALIGN) * ROW_ALIGN
        d0 = dst_ref[t * N_EXPERTS + e]
        return _segment_copies(
            n_rows,
            lambda off, size: stage.at[slot, pl.ds(off, size)],
            lambda off, size: xs_ref.at[pl.ds(pl.multiple_of(d0 + off, ROW_ALIGN), size)],
            sems.at[slot])

    for e in range(N_EXPERTS):
        slot = e % 2
        if e >= 2:
            _wait_all(copies(e - 2))
        slot_row = srow_ref[e:e + 1, :]
        n = cnt_ref[t * N_EXPERTS + e]

        def gather_rows(r0, rows, slot=slot, slot_row=slot_row):
            rid = lax.broadcasted_iota(jnp.int32, (rows, tu), 0) + r0
            onehot = jnp.where(slot_row == rid, 1.0, 0.0).astype(BF16)
            stage[slot, pl.ds(r0, rows), :] = _dot(onehot, hn_ref[...]).astype(BF16)

        @pl.when(n > 0)
        def _():
            gather_rows(0, HEAD_BLOCK)

        def gather(b, carry):
            gather_rows(pl.multiple_of(HEAD_BLOCK + b * rb, rb), rb)
            return carry

        lax.fori_loop(0, _ceil_blocks(jnp.maximum(n - HEAD_BLOCK, 0), rb), gather, 0)
        _start_all(copies(e))

    _wait_all(copies(N_EXPERTS - 2))
    _wait_all(copies(N_EXPERTS - 1))

    @pl.when(t == pl.num_programs(0) - 1)
    def _():
        stage[0, 0:EXP_TILE, :] = jnp.zeros((EXP_TILE, D_MODEL), BF16)

        def tail_copies(e):
            d0 = tail_ref[e]
            return _segment_copies(
                tail_ref[N_EXPERTS + e],
                lambda off, size: stage.at[0, pl.ds(0, size)],
                lambda off, size: xs_ref.at[pl.ds(pl.multiple_of(d0 + off, ROW_ALIGN), size)],
                sems.at[e % 2], max_bits=EXP_TILE.bit_length() - ROW_ALIGN.bit_length())

        for e in range(N_EXPERTS):
            _start_all(tail_copies(e))
        for e in range(N_EXPERTS):
            _wait_all(tail_copies(e))


def _dispatch(cnt, dst, tail, hn, slot_row):
    tm = FFN_TILE
    n_rows = MAX_TILES * EXP_TILE
    grid_spec = pltpu.PrefetchScalarGridSpec(
        num_scalar_prefetch=3,
        grid=(N_UNITS,),
        in_specs=[
            pl.BlockSpec((tm, D_MODEL), lambda t, c, d, z: (t, 0)),
            pl.BlockSpec((N_EXPERTS, tm), lambda t, c, d, z: (0, t)),
        ],
        out_specs=pl.BlockSpec(memory_space=pl.ANY),
        scratch_shapes=[pltpu.VMEM((2, tm, D_MODEL), BF16), pltpu.SemaphoreType.DMA((2,))],
    )
    return pl.pallas_call(
        _dispatch_kernel,
        grid_spec=grid_spec,
        out_shape=jax.ShapeDtypeStruct((n_rows, D_MODEL), BF16),
        compiler_params=pltpu.CompilerParams(
            dimension_semantics=("arbitrary",), vmem_limit_bytes=VMEM_LIMIT),
        name="moe_dispatch",
    )(cnt, dst, tail, hn, slot_row)


def _experts_kernel(texp_ref, ntile_ref, xs_ref, wg_ref, wu_ref, wd_ref, ys_ref, acc_ref):
    del texp_ref
    i, c = pl.program_id(0), pl.program_id(1)
    nc = pl.num_programs(1)

    @pl.when(i < ntile_ref[0])
    def _():
        @pl.when(c == 0)
        def _():
            acc_ref[...] = jnp.zeros_like(acc_ref)

        subs = range(0, EXP_TILE, SUB_ROWS)
        gu = [(_dot(xs_ref[r:r + SUB_ROWS, :], wg_ref[...]), _dot(xs_ref[r:r + SUB_ROWS, :], wu_ref[...]))
              for r in subs]
        act = [(g * _sigmoid(g) * u).astype(BF16) for g, u in gu]
        for r, a in zip(subs, act):
            acc_ref[r:r + SUB_ROWS, :] += _dot(a, wd_ref[...])

        @pl.when(c == nc - 1)
        def _():
            ys_ref[...] = acc_ref[...].astype(BF16)


def _experts(tile_expert, n_tiles, xs, wg, wu, wd):
    tm = EXP_TILE
    nc = D_EXPERT // EXP_CHUNK

    def row_map(i, c, te, nt):
        return (jnp.minimum(i, jnp.maximum(nt[0] - 1, 0)), 0)

    def chunk(i, c, nt):
        return jnp.where(i < nt[0], c, nc - 1)

    grid_spec = pltpu.PrefetchScalarGridSpec(
        num_scalar_prefetch=2,
        grid=(MAX_TILES, nc),
        in_specs=[
            pl.BlockSpec((tm, D_MODEL), row_map),
            pl.BlockSpec((None, D_MODEL, EXP_CHUNK), lambda i, c, te, nt: (te[i], 0, chunk(i, c, nt))),
            pl.BlockSpec((None, D_MODEL, EXP_CHUNK), lambda i, c, te, nt: (te[i], 0, chunk(i, c, nt))),
            pl.BlockSpec((None, EXP_CHUNK, D_MODEL), lambda i, c, te, nt: (te[i], chunk(i, c, nt), 0)),
        ],
        out_specs=pl.BlockSpec((tm, D_MODEL), row_map),
        scratch_shapes=[pltpu.VMEM((tm, D_MODEL), F32)],
    )
    return pl.pallas_call(
        _experts_kernel,
        grid_spec=grid_spec,
        out_shape=jax.ShapeDtypeStruct((MAX_TILES * tm, D_MODEL), BF16),
        compiler_params=pltpu.CompilerParams(
            dimension_semantics=("arbitrary", "arbitrary"), vmem_limit_bytes=VMEM_LIMIT),
        name="moe_experts",
    )(tile_expert, n_tiles, xs, wg, wu, wd)


def _combine_kernel(cnt_ref, dst_ref, ys_ref, col_ref, h_ref, nf_ref, op_ref, os_ref, stage, sems, acc_ref):
    tu, rb = FFN_TILE, 2 * ROW_BLOCK
    t = pl.program_id(0)

    @pl.when(t == 0)
    def _():
        stage[...] = jnp.zeros_like(stage)

    def copies(e):
        n_rows = _ceil_blocks(cnt_ref[t * N_EXPERTS + e], ROW_ALIGN) * ROW_ALIGN
        d0 = dst_ref[t * N_EXPERTS + e]
        return _segment_copies(
            n_rows,
            lambda off, size: ys_ref.at[pl.ds(pl.multiple_of(d0 + off, ROW_ALIGN), size)],
            lambda off, size: stage.at[e, pl.ds(off, size)],
            sems.at[e])

    for e in range(N_EXPERTS):
        _start_all(copies(e))
    for e in range(N_EXPERTS):
        _wait_all(copies(e))

    def weighted_block(e, r0):
        cid = (lax.broadcasted_iota(jnp.int32, (tu, rb), 1) + r0).astype(F32)
        return jnp.where(col_ref[:, e:e + 1] == cid, col_ref[:, e + N_EXPERTS:e + N_EXPERTS + 1], 0.0).astype(BF16)

    acc = h_ref[...]
    for e in range(N_EXPERTS):
        acc = acc + _dot(weighted_block(e, 0), stage[e, 0:rb, :])
    acc_ref[...] = acc
    for e in range(N_EXPERTS):
        def scatter(b, carry, e=e):
            r0 = pl.multiple_of(b * rb, rb)
            acc_ref[...] += _dot(weighted_block(e, r0), stage[e, pl.ds(r0, rb), :])
            return carry

        lax.fori_loop(1, _ceil_blocks(cnt_ref[t * N_EXPERTS + e], rb), scatter, 0)

    out = _rms(acc_ref[...], nf_ref[...])

    @pl.when(t < N_PROMPT // FFN_TILE)
    def _():
        op_ref[...] = out

    @pl.when(t >= N_PROMPT // FFN_TILE)
    def _():
        os_ref[...] = out


def _combine(cnt, dst, ys, cols, h, nf):
    tm = FFN_TILE
    np_tiles = N_PROMPT // tm
    assert N_PROMPT % tm == 0 and N_SAMPLE % tm == 0
    grid_spec = pltpu.PrefetchScalarGridSpec(
        num_scalar_prefetch=2,
        grid=(N_UNITS,),
        in_specs=[
            pl.BlockSpec(memory_space=pl.ANY),
            pl.BlockSpec((tm, 2 * N_EXPERTS), lambda t, c, d: (t, 0)),
            pl.BlockSpec((tm, D_MODEL), lambda t, c, d: (t, 0)),
            pl.BlockSpec((1, D_MODEL), lambda t, c, d: (0, 0)),
        ],
        out_specs=[
            pl.BlockSpec((tm, D_MODEL), lambda t, c, d: (jnp.minimum(t, np_tiles - 1), 0)),
            pl.BlockSpec((tm, D_MODEL), lambda t, c, d: (jnp.maximum(t - np_tiles, 0), 0)),
        ],
        scratch_shapes=[pltpu.VMEM((N_EXPERTS, tm, D_MODEL), BF16), pltpu.SemaphoreType.DMA((N_EXPERTS,)),
                        pltpu.VMEM((tm, D_MODEL), F32)],
    )
    return pl.pallas_call(
        _combine_kernel,
        grid_spec=grid_spec,
        out_shape=[
            jax.ShapeDtypeStruct((N_PROMPT, D_MODEL), F32),
            jax.ShapeDtypeStruct((N_SAMPLE, D_MODEL), F32),
        ],
        compiler_params=pltpu.CompilerParams(
            dimension_semantics=("arbitrary",), vmem_limit_bytes=VMEM_LIMIT),
        name="moe_combine",
    )(cnt, dst, ys, cols, h, nf)


def kernel(x_prompt, x_sample, cache_win_k, cache_win_v, state_conv, state_pool, norm_mix, w_in, attn_sinks,
           w_attn_out, conv_w, w_conv_out, w_pool, pool_scale, w_out, norm_ffn, ffn_w_gate, ffn_w_up, ffn_w_down,
           moe_router, moe_w_gate, moe_w_up, moe_w_down, norm_final):
    assert DEPTH == 2 and ffn_w_gate.shape[0] == 1 and moe_w_gate.shape[0] == 1
    norm3 = norm_mix.reshape(DEPTH, 1, D_MODEL)
    scale3 = pool_scale.reshape(DEPTH, 1, D_MODEL)
    w_pool3 = w_pool.reshape(DEPTH, D_POOL, POOL_OUT_GROUP)
    pool_shape = (len(POOL_WINDOWS), POOL_GROUP, POOL_OUT_GROUP)
    mixer_bf16 = [(w_in[0].astype(BF16), w_attn_out[0].astype(BF16), w_conv_out[0].astype(BF16),
                   w_pool[0].astype(BF16), w_out[0].astype(BF16))]
    ffn_casts = ((ffn_w_gate, 0, 32), (ffn_w_up, 0, 32), (ffn_w_down, 0, 16))
    later_casts = (
        (moe_w_gate.reshape(1, N_EXPERTS * D_MODEL, D_EXPERT), 0, CAST_STEPS),
        (moe_w_up.reshape(1, N_EXPERTS * D_MODEL, D_EXPERT), 0, CAST_STEPS),
        (moe_w_down.reshape(1, N_EXPERTS * D_EXPERT, D_MODEL), 0, CAST_STEPS),
        (w_in, 1, 64), (w_attn_out, 1, 32), (w_conv_out, 1, 32), (w_pool3, 1, 32), (w_out, 1, 64))
    kc = cache_win_k.reshape(DEPTH, DEC_BATCH, WINDOW, D_KV)
    vc = cache_win_v.reshape(DEPTH, DEC_BATCH, WINDOW, D_KV)

    xp = x_prompt.reshape(N_PROMPT, D_MODEL)
    xs, xs_row0 = x_sample.reshape(N_SAMPLE, D_MODEL), 0
    p_states, s_states = (), ()
    for l in range(DEPTH):
        sinks = attn_sinks[l]
        wi, wao, wco, wp, wo = mixer_bf16[l]
        mix_weights = (norm3, wi, wao, conv_w, wco, wp, scale3, wo)
        h, p_states, ffn_bf16 = _prompt_mix(l, xp, sinks, mix_weights, p_states, ffn_casts if l == 0 else ())
        h, s_states = _sample_mix(l, xs, xs_row0, sinks, kc, vc, state_conv, state_pool, mix_weights, h, s_states)
        i = l // 2
        nw = norm_ffn[l].reshape(1, D_MODEL)
        if l % 2 == 0:
            xp, (eg, eu, ed, wi, wao, wco, wp, wo) = _ffn_dense(h, nw, *ffn_bf16, later_casts)
            mixer_bf16.append((wi, wao, wco, wp.reshape(pool_shape), wo))
            xs, xs_row0 = xp, N_PROMPT
        else:
            hn, slot_row, cols, cnt = _router(h, nw, moe_router[i].T)
            cnt = cnt[:, :, 0].astype(jnp.int32)
            dst, tail, tile_expert, n_tiles = _plan(cnt)
            cnt = cnt.reshape(-1)
            xs_sorted = _dispatch(cnt, dst, tail, hn, slot_row)
            ys_sorted = _experts(tile_expert, n_tiles, xs_sorted,
                                 eg.reshape(N_EXPERTS, D_MODEL, D_EXPERT), eu.reshape(N_EXPERTS, D_MODEL, D_EXPERT),
                                 ed.reshape(N_EXPERTS, D_EXPERT, D_MODEL))
            y_prompt, y_sample = _combine(cnt, dst, ys_sorted, cols, h, norm_final.reshape(1, D_MODEL))

    pk, pv, pc, pp = p_states
    sk, sv, sc, sp = s_states
    kv_shape_p = (DEPTH, BATCH, WINDOW, N_KV_HEADS, HEAD_DIM)
    kv_shape_s = (DEPTH, DEC_BATCH, WINDOW, N_KV_HEADS, HEAD_DIM)
    return (y_prompt.reshape(BATCH, SEQ, D_MODEL), y_sample.reshape(DEC_BATCH, DEC_SEQ, D_MODEL),
            pk.reshape(kv_shape_p), pv.reshape(kv_shape_p), pc, pp,
            sk.reshape(kv_shape_s), sv.reshape(kv_shape_s), sc, sp)
```

```python
import functools
import math

import jax
import jax.numpy as jnp
from jax import lax
from jax.experimental import pallas as pl
from jax.experimental.pallas import tpu as pltpu

F32 = jnp.float32
BF16 = jnp.bfloat16

D_MODEL = 1024
BATCH = 4
SEQ = 4096
DEPTH = 2
DEC_BATCH = 128
DEC_SEQ = 8
HEAD_DIM = 64
N_Q_HEADS = 8
N_KV_HEADS = 2
GQA_GROUP = N_Q_HEADS // N_KV_HEADS
D_ATTN = N_Q_HEADS * HEAD_DIM
D_KV = N_KV_HEADS * HEAD_DIM
WINDOW = 128
ATTN_SCALE = HEAD_DIM ** -0.5
D_CONV = D_MODEL // 2
CONV_WIDTH = 3
D_POOL = D_MODEL // 2
POOL_WINDOWS = (2, 4, 8, 16)
POOL_GROUP = D_POOL // len(POOL_WINDOWS)
POOL_OUT_GROUP = D_MODEL // len(POOL_WINDOWS)
POOL_BUF = max(POOL_WINDOWS) - 1
OFF_Q = 0
OFF_K = OFF_Q + D_ATTN
OFF_V = OFF_K + D_KV
OFF_CX = OFF_V + D_KV
OFF_CB = OFF_CX + D_CONV
OFF_CC = OFF_CB + D_CONV
OFF_P = OFF_CC + D_CONV
OFF_G = OFF_P + D_POOL
D_IN = OFF_G + 3 * D_MODEL
D_FF = 2816
N_EXPERTS = 8
D_EXPERT = 3584
EPS = 1e-5
LOG2E = math.log2(math.e)

N_PROMPT = BATCH * SEQ
N_SAMPLE = DEC_BATCH * DEC_SEQ
N_TOK = N_PROMPT + N_SAMPLE

PROMPT_TILE = 512
SAMPLE_SEQS = 32
SAMPLE_ROWS = SAMPLE_SEQS * DEC_SEQ
ATT_GROUP = 8
FFN_TILE = 1024
EXP_CHUNK = D_EXPERT // 2
SUB_ROWS = 256
CAST_STEPS = 64
ROUTE_FIELDS = 8
ROW_ALIGN = 16
STAGE_ROWS = -(-(2 * FFN_TILE + N_EXPERTS * (ROW_ALIGN - 1)) // 256) * 256
EXP_TILE = 512
N_UNITS = N_TOK // FFN_TILE
MAX_TILES = -(-(2 * N_TOK + N_UNITS * N_EXPERTS * (ROW_ALIGN - 1)) // EXP_TILE) + N_EXPERTS
N_PAD = N_TOK
SEG_BITS = 7
assert FFN_TILE == ROW_ALIGN << (SEG_BITS - 1)
VMEM_LIMIT = 58 * 1024 * 1024


def _dot(a, b):
    return jnp.dot(a, b, preferred_element_type=F32)


def _dot_nt(a, b):
    return lax.dot_general(a, b, (((1,), (1,)), ((), ())), preferred_element_type=F32)


def _rms(x, g):
    return x * lax.rsqrt(jnp.mean(x * x, axis=-1, keepdims=True) + EPS) * g


def _sigmoid(x):
    return 0.5 * jnp.tanh(0.5 * x) + 0.5


def _head_slope(h):
    return float(2.0 ** (-8.0 * (h + 1) / N_Q_HEADS))


def _z_part(nb, win_ref, i):
    if i == 0:
        return (_dot(nb, win_ref[:, OFF_CX:OFF_P]),)
    if i == 1:
        return (_dot(nb, win_ref[:, OFF_P:OFF_G]), _dot(nb, win_ref[:, OFF_G:OFF_G + D_MODEL]))
    return (_dot(nb, win_ref[:, OFF_G + (i - 1) * D_MODEL:OFF_G + i * D_MODEL]),)


def _mix_tail(x, att_bf, cb, yc, d_groups, gate_logits, wao_ref, wco_ref, wp_ref, ps_ref, wo_ref):
    att_o = _dot(att_bf, wao_ref[...])
    merged = _sigmoid(gate_logits[0]) * att_o
    conv_o = _dot((cb * yc).astype(BF16), wco_ref[...])
    merged = merged + _sigmoid(gate_logits[1]) * conv_o
    pool_o = jnp.concatenate([_dot(d.astype(BF16), wp_ref[g]) for g, d in enumerate(d_groups)], axis=-1)
    pool_o = pool_o * ps_ref[...]
    merged = merged + _sigmoid(gate_logits[2]) * pool_o
    return x + _dot(merged.astype(BF16), wo_ref[...])


def _prompt_mix_kernel(*refs, n_alias, n_cast):
    (sinks_ref, x_ref, nw_ref, win_ref, wao_ref, cw_ref, wco_ref, wp_ref, ps_ref, wo_ref) = refs[:10]
    cast_src = refs[10 + n_alias:10 + n_alias + n_cast]
    outs = refs[10 + n_alias + n_cast:]
    h_ref, kw_ref, vw_ref, cs_ref, pst_ref = outs[:5]
    cast_dst = outs[5:5 + n_cast]
    qbuf, kbuf, vbuf, att_ref, ubuf, pbuf, bias_ref = outs[5 + n_cast:]
    _run_casts(cast_src, cast_dst)
    tm = PROMPT_TILE
    b = pl.program_id(0)
    t = pl.program_id(1)
    x = x_ref[...]
    nb = _rms(x, nw_ref[...]).astype(BF16)
    lane = lax.broadcasted_iota(jnp.int32, (tm, D_KV), 1)
    low = lane < HEAD_DIM

    @pl.when(jnp.logical_and(b == 0, t == 0))
    def _():
        qi = lax.broadcasted_iota(jnp.int32, (WINDOW, 2 * WINDOW), 0)
        si = lax.broadcasted_iota(jnp.int32, (WINDOW, 2 * WINDOW), 1)
        dist_i = qi + WINDOW - si
        dist = dist_i.astype(F32)
        band = jnp.logical_and(dist_i >= 0, dist_i <= WINDOW)
        for h in range(N_Q_HEADS):
            bias = jnp.where(band, (-_head_slope(h) * LOG2E) * dist, -jnp.inf)
            bias_ref[0, h] = bias
            bias_ref[1, h] = jnp.where(si >= WINDOW, bias, -jnp.inf)

    @pl.when(t == 0)
    def _():
        for i in range(4):
            kbuf[i, 0:WINDOW, :] = jnp.zeros((WINDOW, D_KV), BF16)
            vbuf[i, 0:WINDOW, :] = jnp.zeros((WINDOW, D_KV), BF16)
        ubuf[0:8, :] = jnp.zeros((8, D_CONV), F32)
        pbuf[0:16, :] = jnp.zeros((16, D_POOL), F32)

    @pl.when(t > 0)
    def _():
        for i in range(4):
            kbuf[i, 0:WINDOW, :] = kbuf[i, tm:tm + WINDOW, :]
            vbuf[i, 0:WINDOW, :] = vbuf[i, tm:tm + WINDOW, :]
        ubuf[0:8, :] = ubuf[tm:tm + 8, :]
        pbuf[0:16, :] = pbuf[tm:tm + 16, :]

    qkv = _dot(nb, win_ref[:, OFF_Q:OFF_CX])
    qbuf[...] = (qkv[:, OFF_Q:OFF_K] * (ATTN_SCALE * LOG2E)).astype(BF16)
    k = qkv[:, OFF_K:OFF_V]
    v = qkv[:, OFF_V:OFF_CX]
    kw_ref[...] = k[tm - WINDOW:, :]
    vw_ref[...] = v[tm - WINDOW:, :]
    k_sw = pltpu.roll(k, HEAD_DIM, 1)
    v_sw = pltpu.roll(v, HEAD_DIM, 1)
    k_var = (jnp.where(low, k, 0.0), jnp.where(low, 0.0, k_sw), jnp.where(low, k_sw, 0.0), jnp.where(low, 0.0, k))
    v_var = (jnp.where(low, v, 1.0), jnp.where(low, 1.0, v_sw), jnp.where(low, v_sw, 1.0), jnp.where(low, 1.0, v))
    for i in range(4):
        kbuf[i, WINDOW:WINDOW + tm, :] = k_var[i].astype(BF16)
        vbuf[i, WINDOW:WINDOW + tm, :] = v_var[i].astype(BF16)

    lane_q = lax.broadcasted_iota(jnp.int32, (WINDOW, 2 * HEAD_DIM), 1)
    low_q = lane_q < HEAD_DIM

    seq_start = jnp.where(t == 0, 1, 0)
    variant = [2 * (h // GQA_GROUP) + h % 2 for h in range(N_Q_HEADS)]
    units = [(j, h) for j in range(tm // WINDOW) for h in range(N_Q_HEADS)]
    z_parts = [_z_part(nb, win_ref, 0)]
    scores = {}
    for j, h in units:
        r0 = j * WINDOW
        qp = qbuf[r0:r0 + WINDOW, (h // 2) * 2 * HEAD_DIM:(h // 2 + 1) * 2 * HEAD_DIM]
        first = seq_start if j == 0 else 0
        scores[j, h] = _dot_nt(qp, kbuf[variant[h], r0:r0 + 2 * WINDOW, :]) + bias_ref[first, h]
    z_parts.append(_z_part(nb, win_ref, 1))
    probs = {}
    for j, h in units:
        sink = sinks_ref[h] * LOG2E
        m = jnp.maximum(jnp.max(scores[j, h], axis=-1, keepdims=True), sink)
        probs[j, h] = (jnp.exp2(scores[j, h] - m).astype(BF16), jnp.exp2(sink - m))
    z_parts.append(_z_part(nb, win_ref, 2))
    outs = {}
    for j, h in units:
        r0 = j * WINDOW
        p, sink_term = probs[j, h]
        o = _dot(p, vbuf[variant[h], r0:r0 + 2 * WINDOW, :])
        den = (o[:, HEAD_DIM:HEAD_DIM + 1] if h % 2 == 0 else o[:, 0:1]) + sink_term
        outs[j, h] = o / den
    z_parts.append(_z_part(nb, win_ref, 3))
    for j in range(tm // WINDOW):
        for pair in range(N_Q_HEADS // 2):
            att_ref[j * WINDOW:(j + 1) * WINDOW, pair * 2 * HEAD_DIM:(pair + 1) * 2 * HEAD_DIM] = (
                jnp.where(low_q, outs[j, 2 * pair], outs[j, 2 * pair + 1]).astype(BF16))

    (conv,), (pool_in, gate_att), (gate_conv,), (gate_pool,) = z_parts
    cx = conv[:, 0:D_CONV]
    cb = conv[:, D_CONV:2 * D_CONV]
    ubuf[8:8 + tm, :] = conv[:, 2 * D_CONV:3 * D_CONV] * cx
    yc = (cw_ref[0:1, :] * ubuf[6:6 + tm, :] + cw_ref[1:2, :] * ubuf[7:7 + tm, :]
          + cw_ref[2:3, :] * ubuf[8:8 + tm, :])
    cs_ref[...] = ubuf[tm + 6:tm + 8, :]

    pbuf[16:16 + tm, :] = pool_in
    pos = lax.broadcasted_iota(jnp.int32, (tm, 1), 0) + t * tm + 1
    d_groups = []
    for g, w in enumerate(POOL_WINDOWS):
        c0 = g * POOL_GROUP
        cur = pbuf[16:16 + tm, c0:c0 + POOL_GROUP]
        tot = cur
        for j in range(1, w):
            tot = tot + pbuf[16 - j:16 - j + tm, c0:c0 + POOL_GROUP]
        cnt = jnp.minimum(pos, w).astype(F32)
        d_groups.append(tot / cnt - cur)
    pst_ref[...] = pbuf[tm + 1:tm + 16, :]

    h_ref[...] = _mix_tail(x, att_ref[...], cb, yc, d_groups, (gate_att, gate_conv, gate_pool),
                           wao_ref, wco_ref, wp_ref, ps_ref, wo_ref)


def _weight_specs(l, grid_rank):
    def const(*idx):
        if grid_rank == 1:
            return lambda i: idx
        return lambda b, t: idx

    return [
        pl.BlockSpec((None, 1, D_MODEL), const(l, 0, 0)),
        pl.BlockSpec((D_MODEL, D_IN), const(0, 0)),
        pl.BlockSpec((D_ATTN, D_MODEL), const(0, 0)),
        pl.BlockSpec((None, CONV_WIDTH, D_CONV), const(l, 0, 0)),
        pl.BlockSpec((D_CONV, D_MODEL), const(0, 0)),
        pl.BlockSpec((len(POOL_WINDOWS), POOL_GROUP, POOL_OUT_GROUP), const(0, 0, 0)),
        pl.BlockSpec((None, 1, D_MODEL), const(l, 0, 0)),
        pl.BlockSpec((D_MODEL, D_MODEL), const(0, 0)),
    ]


def _cast_job(src, lead, n_steps, step_of):
    rows, cols = src.shape[1:]
    r = rows // n_steps
    assert r * n_steps == rows and r % ROW_ALIGN == 0

    def block(*g):
        return jnp.minimum(step_of(*g), n_steps - 1)

    return (pl.BlockSpec((None, r, cols), lambda *g: (lead, block(*g), 0)),
            pl.BlockSpec((r, cols), lambda *g: (block(*g), 0)),
            jax.ShapeDtypeStruct((rows, cols), BF16))


def _run_casts(src_refs, dst_refs):
    for s, d in zip(src_refs, dst_refs):
        d[...] = s[...].astype(BF16)


def _prompt_mix(l, x, sinks, weights, prev_states, casts=()):
    tm = PROMPT_TILE
    nt = SEQ // tm
    n_alias = len(prev_states)
    jobs = [_cast_job(src, lead, n, lambda b, t: b * nt + t) for src, lead, n in casts]
    in_specs = ([pl.BlockSpec(memory_space=pltpu.SMEM),
                 pl.BlockSpec((tm, D_MODEL), lambda b, t: (b * nt + t, 0))] + _weight_specs(l, 2)
                + [pl.BlockSpec(memory_space=pl.ANY)] * n_alias + [j[0] for j in jobs])
    out_specs = [
        pl.BlockSpec((tm, D_MODEL), lambda b, t: (b * nt + t, 0)),
        pl.BlockSpec((None, None, WINDOW, D_KV), lambda b, t: (l, b, 0, 0)),
        pl.BlockSpec((None, None, WINDOW, D_KV), lambda b, t: (l, b, 0, 0)),
        pl.BlockSpec((None, None, CONV_WIDTH - 1, D_CONV), lambda b, t: (l, b, 0, 0)),
        pl.BlockSpec((None, None, POOL_BUF, D_POOL), lambda b, t: (l, b, 0, 0)),
    ]
    out_shape = [
        jax.ShapeDtypeStruct((N_PAD, D_MODEL), F32),
        jax.ShapeDtypeStruct((DEPTH, BATCH, WINDOW, D_KV), F32),
        jax.ShapeDtypeStruct((DEPTH, BATCH, WINDOW, D_KV), F32),
        jax.ShapeDtypeStruct((DEPTH, BATCH, CONV_WIDTH - 1, D_CONV), F32),
        jax.ShapeDtypeStruct((DEPTH, BATCH, POOL_BUF, D_POOL), F32),
    ]
    scratch = [
        pltpu.VMEM((tm, D_ATTN), BF16),
        pltpu.VMEM((4, WINDOW + tm, D_KV), BF16),
        pltpu.VMEM((4, WINDOW + tm, D_KV), BF16),
        pltpu.VMEM((tm, D_ATTN), BF16),
        pltpu.VMEM((8 + tm, D_CONV), F32),
        pltpu.VMEM((16 + tm, D_POOL), F32),
        pltpu.VMEM((2, N_Q_HEADS, WINDOW, 2 * WINDOW), F32),
    ]
    outs = pl.pallas_call(
        functools.partial(_prompt_mix_kernel, n_alias=n_alias, n_cast=len(jobs)),
        grid=(BATCH, nt),
        in_specs=in_specs,
        out_specs=out_specs + [j[1] for j in jobs],
        out_shape=out_shape + [j[2] for j in jobs],
        scratch_shapes=scratch,
        input_output_aliases={10 + i: 1 + i for i in range(n_alias)},
        compiler_params=pltpu.CompilerParams(
            dimension_semantics=("arbitrary", "arbitrary"), vmem_limit_bytes=VMEM_LIMIT),
        name=f"prompt_mix_l{l}",
    )(sinks, x, *weights, *prev_states, *[c[0] for c in casts])
    return outs[0], tuple(outs[1:5]), tuple(outs[5:])


def _sample_mix_kernel(*refs, n_alias):
    (sinks_ref, x_ref, kc_ref, vc_ref, cst_ref, pin_ref,
     nw_ref, win_ref, wao_ref, cw_ref, wco_ref, wp_ref, ps_ref, wo_ref) = refs[:14]
    (h_ref, ko_ref, vo_ref, co_ref, po_ref,
     qbuf, knb, vnb, att_ref, cbuf, ebuf, bias_c, bias_n) = refs[14 + n_alias:]
    ns, nr, T = SAMPLE_SEQS, SAMPLE_ROWS, DEC_SEQ
    x = x_ref[...]
    nb = _rms(x, nw_ref[...]).astype(BF16)

    qkv = _dot(nb, win_ref[:, OFF_Q:OFF_CX])
    qbuf[...] = (qkv[:, OFF_Q:OFF_K] * ATTN_SCALE).astype(BF16)
    k = qkv[:, OFF_K:OFF_V]
    v = qkv[:, OFF_V:OFF_CX]
    knb[...] = k.astype(BF16)
    vnb[...] = v.astype(BF16)
    ko_ref[:, 0:WINDOW - T, :] = kc_ref[:, T:WINDOW, :]
    vo_ref[:, 0:WINDOW - T, :] = vc_ref[:, T:WINDOW, :]
    ko_ref[:, WINDOW - T:WINDOW, :] = k.reshape(ns, T, D_KV)
    vo_ref[:, WINDOW - T:WINDOW, :] = v.reshape(ns, T, D_KV)

    gr = ATT_GROUP * T
    gc = ATT_GROUP * WINDOW

    @pl.when(pl.program_id(0) == 0)
    def _():
        rq = lax.broadcasted_iota(jnp.int32, (gr, gc), 0)
        cq = lax.broadcasted_iota(jnp.int32, (gr, gc), 1)
        tq = rq & (T - 1)
        sc_pos = cq & (WINDOW - 1)
        valid_c = jnp.logical_and((rq >> 3) == (cq >> 7), sc_pos >= tq)
        dist_c = (WINDOW + tq - sc_pos).astype(F32)
        rn = lax.broadcasted_iota(jnp.int32, (gr, gr), 0)
        cn = lax.broadcasted_iota(jnp.int32, (gr, gr), 1)
        tn = rn & (T - 1)
        jn = cn & (T - 1)
        valid_n = jnp.logical_and((rn >> 3) == (cn >> 3), jn <= tn)
        dist_n = (tn - jn).astype(F32)
        for h in range(N_Q_HEADS):
            bias_c[h] = jnp.where(valid_c, -_head_slope(h) * dist_c, -jnp.inf)
            bias_n[h] = jnp.where(valid_n, -_head_slope(h) * dist_n, -jnp.inf)

    z_parts = []
    for gi in range(ns // ATT_GROUP):
        z_parts.append(_z_part(nb, win_ref, gi))
        r0 = gi * gr
        s0 = gi * ATT_GROUP
        qg = qbuf[r0:r0 + gr, :]
        kcg = kc_ref[s0:s0 + ATT_GROUP, :, :].reshape(gc, D_KV).astype(BF16)
        vcg = vc_ref[s0:s0 + ATT_GROUP, :, :].reshape(gc, D_KV).astype(BF16)
        kng = knb[r0:r0 + gr, :]
        vng = vnb[r0:r0 + gr, :]
        kvs = [(h // GQA_GROUP) * HEAD_DIM for h in range(N_Q_HEADS)]
        scores = []
        for h in range(N_Q_HEADS):
            qh = qg[:, h * HEAD_DIM:(h + 1) * HEAD_DIM]
            scores.append((_dot_nt(qh, kcg[:, kvs[h]:kvs[h] + HEAD_DIM]) + bias_c[h],
                           _dot_nt(qh, kng[:, kvs[h]:kvs[h] + HEAD_DIM]) + bias_n[h]))
        probs = []
        for h in range(N_Q_HEADS):
            s_c, s_n = scores[h]
            sink = sinks_ref[h]
            m = jnp.maximum(jnp.maximum(jnp.max(s_c, axis=-1, keepdims=True),
                                        jnp.max(s_n, axis=-1, keepdims=True)), sink)
            p_c = jnp.exp(s_c - m)
            p_n = jnp.exp(s_n - m)
            den = jnp.exp(sink - m) + jnp.sum(p_c, axis=-1, keepdims=True) + jnp.sum(p_n, axis=-1, keepdims=True)
            probs.append((p_c.astype(BF16), p_n.astype(BF16), den))
        outs = []
        for h in range(N_Q_HEADS):
            p_c, p_n, den = probs[h]
            o = _dot(p_c, vcg[:, kvs[h]:kvs[h] + HEAD_DIM]) + _dot(p_n, vng[:, kvs[h]:kvs[h] + HEAD_DIM])
            outs.append(o / den)
        att_ref[r0:r0 + gr, :] = jnp.concatenate(outs, axis=-1).astype(BF16)

    (conv,), (pool_in, gate_att), (gate_conv,), (gate_pool,) = z_parts
    cx = conv[:, 0:D_CONV]
    cb = conv[:, D_CONV:2 * D_CONV]
    u = conv[:, 2 * D_CONV:3 * D_CONV] * cx
    cbuf[:, 6:8, :] = cst_ref[...]
    cbuf[:, 8:16, :] = u.reshape(ns, T, D_CONV)
    w0 = cw_ref[0:1, :].reshape(1, 1, D_CONV)
    w1 = cw_ref[1:2, :].reshape(1, 1, D_CONV)
    w2 = cw_ref[2:3, :].reshape(1, 1, D_CONV)
    yc = (w0 * cbuf[:, 6:14, :] + w1 * cbuf[:, 7:15, :] + w2 * cbuf[:, 8:16, :]).reshape(nr, D_CONV)
    co_ref[...] = cbuf[:, 14:16, :]

    ebuf[:, 1:16, :] = pin_ref[...]
    ebuf[:, 16:24, :] = pool_in.reshape(ns, T, D_POOL)
    d_groups = []
    for g, w in enumerate(POOL_WINDOWS):
        c0 = g * POOL_GROUP
        cur = ebuf[:, 16:24, c0:c0 + POOL_GROUP]
        tot = cur
        for j in range(1, w):
            tot = tot + ebuf[:, 16 - j:24 - j, c0:c0 + POOL_GROUP]
        d_groups.append((tot / float(w) - cur).reshape(nr, POOL_GROUP))
    po_ref[...] = ebuf[:, 9:24, :]

    h_ref[...] = _mix_tail(x, att_ref[...], cb, yc, d_groups, (gate_att, gate_conv, gate_pool),
                           wao_ref, wco_ref, wp_ref, ps_ref, wo_ref)


def _sample_mix(l, x, x_row0, sinks, kc, vc, cst, pst, weights, h_buf, prev_states):
    ns, nr = SAMPLE_SEQS, SAMPLE_ROWS
    n_alias = 1 + len(prev_states)
    xb0 = x_row0 // nr
    hb0 = N_PROMPT // nr
    in_specs = [
        pl.BlockSpec(memory_space=pltpu.SMEM),
        pl.BlockSpec((nr, D_MODEL), lambda i: (xb0 + i, 0)),
        pl.BlockSpec((None, ns, WINDOW, D_KV), lambda i: (l, i, 0, 0)),
        pl.BlockSpec((None, ns, WINDOW, D_KV), lambda i: (l, i, 0, 0)),
        pl.BlockSpec((None, ns, CONV_WIDTH - 1, D_CONV), lambda i: (l, i, 0, 0)),
        pl.BlockSpec((None, ns, POOL_BUF, D_POOL), lambda i: (l, i, 0, 0)),
    ] + _weight_specs(l, 1) + [pl.BlockSpec(memory_space=pl.ANY)] * n_alias
    out_specs = [
        pl.BlockSpec((nr, D_MODEL), lambda i: (hb0 + i, 0)),
        pl.BlockSpec((None, ns, WINDOW, D_KV), lambda i: (l, i, 0, 0)),
        pl.BlockSpec((None, ns, WINDOW, D_KV), lambda i: (l, i, 0, 0)),
        pl.BlockSpec((None, ns, CONV_WIDTH - 1, D_CONV), lambda i: (l, i, 0, 0)),
        pl.BlockSpec((None, ns, POOL_BUF, D_POOL), lambda i: (l, i, 0, 0)),
    ]
    out_shape = [
        jax.ShapeDtypeStruct((N_PAD, D_MODEL), F32),
        jax.ShapeDtypeStruct((DEPTH, DEC_BATCH, WINDOW, D_KV), F32),
        jax.ShapeDtypeStruct((DEPTH, DEC_BATCH, WINDOW, D_KV), F32),
        jax.ShapeDtypeStruct((DEPTH, DEC_BATCH, CONV_WIDTH - 1, D_CONV), F32),
        jax.ShapeDtypeStruct((DEPTH, DEC_BATCH, POOL_BUF, D_POOL), F32),
    ]
    scratch = [
        pltpu.VMEM((nr, D_ATTN), BF16),
        pltpu.VMEM((nr, D_KV), BF16),
        pltpu.VMEM((nr, D_KV), BF16),
        pltpu.VMEM((nr, D_ATTN), BF16),
        pltpu.VMEM((ns, 16, D_CONV), F32),
        pltpu.VMEM((ns, 24, D_POOL), F32),
        pltpu.VMEM((N_Q_HEADS, ATT_GROUP * DEC_SEQ, ATT_GROUP * WINDOW), F32),
        pltpu.VMEM((N_Q_HEADS, ATT_GROUP * DEC_SEQ, ATT_GROUP * DEC_SEQ), F32),
    ]
    outs = pl.pallas_call(
        functools.partial(_sample_mix_kernel, n_alias=n_alias),
        grid=(DEC_BATCH // ns,),
        in_specs=in_specs,
        out_specs=out_specs,
        out_shape=out_shape,
        scratch_shapes=scratch,
        input_output_aliases={14 + i: i for i in range(n_alias)},
        compiler_params=pltpu.CompilerParams(
            dimension_semantics=("arbitrary",), vmem_limit_bytes=VMEM_LIMIT),
        name=f"sample_mix_l{l}",
    )(sinks, x, kc, vc, cst, pst, *weights, h_buf, *prev_states)
    return outs[0], tuple(outs[1:])


def _ffn_kernel(*refs, n_cast):
    h_ref, nw_ref, wg_ref, wu_ref, wd_ref = refs[:5]
    o_ref = refs[5 + n_cast]
    h = h_ref[...]
    hn = _rms(h, nw_ref[...]).astype(BF16)
    g = _dot(hn, wg_ref[...])
    u = _dot(hn, wu_ref[...])
    o_ref[...] = h + _dot((g * _sigmoid(g) * u).astype(BF16), wd_ref[...])
    _run_casts(refs[5:5 + n_cast], refs[6 + n_cast:])


def _ffn_dense(h, nw, wg, wu, wd, casts):
    tm = SUB_ROWS
    steps = N_TOK // tm
    assert all(n <= steps for _, _, n in casts)
    jobs = [_cast_job(src, lead, n, lambda i: i) for src, lead, n in casts]
    resident = dict(pipeline_mode=pl.Buffered(1))
    outs = pl.pallas_call(
        functools.partial(_ffn_kernel, n_cast=len(jobs)),
        grid=(steps,),
        in_specs=[
            pl.BlockSpec((tm, D_MODEL), lambda i: (i, 0)),
            pl.BlockSpec((1, D_MODEL), lambda i: (0, 0)),
            pl.BlockSpec((D_MODEL, D_FF), lambda i: (0, 0), **resident),
            pl.BlockSpec((D_MODEL, D_FF), lambda i: (0, 0), **resident),
            pl.BlockSpec((D_FF, D_MODEL), lambda i: (0, 0), **resident),
        ] + [j[0] for j in jobs],
        out_specs=[pl.BlockSpec((tm, D_MODEL), lambda i: (i, 0))] + [j[1] for j in jobs],
        out_shape=[jax.ShapeDtypeStruct((N_PAD, D_MODEL), F32)] + [j[2] for j in jobs],
        compiler_params=pltpu.CompilerParams(
            dimension_semantics=("arbitrary",), vmem_limit_bytes=VMEM_LIMIT),
        name="ffn_dense",
    )(h, nw, wg, wu, wd, *[c[0] for c in casts])
    return outs[0], tuple(outs[1:])


def _router_kernel(h_ref, nw_ref, rt_ref, hn_ref, prow_ref, pcol_ref, cnt_ref):
    tm = FFN_TILE
    hn = _rms(h_ref[...], nw_ref[...])
    hn_hi = hn.astype(BF16)
    hn_ref[...] = hn_hi
    hn_lo = (hn - hn_hi.astype(F32)).astype(BF16)
    rt = rt_ref[...]
    rt_hi = rt.astype(BF16)
    rt_lo = (rt - rt_hi.astype(F32)).astype(BF16)
    rt16 = jnp.concatenate([rt_hi, rt_lo], axis=0)
    by_hi = _dot_nt(rt16, hn_hi)
    logits = by_hi[0:N_EXPERTS] + by_hi[N_EXPERTS:2 * N_EXPERTS] + _dot_nt(rt16, hn_lo)[0:N_EXPERTS]
    eidx = lax.broadcasted_iota(jnp.int32, (N_EXPERTS, tm), 0).astype(F32)
    none = float(N_EXPERTS)
    m1 = jnp.max(logits, axis=0, keepdims=True)
    i1 = jnp.min(jnp.where(logits == m1, eidx, none), axis=0, keepdims=True)
    rest = jnp.where(eidx == i1, -jnp.inf, logits)
    m2 = jnp.max(rest, axis=0, keepdims=True)
    i2 = jnp.min(jnp.where(rest == m2, eidx, none), axis=0, keepdims=True)
    e2 = jnp.exp(m2 - m1)
    w1 = 1.0 / (1.0 + e2)
    w2 = e2 / (1.0 + e2)
    sel1 = eidx == i1
    sel2 = eidx == i2
    mask = jnp.where(jnp.logical_or(sel1, sel2), 1.0, 0.0)
    srow = lax.broadcasted_iota(jnp.int32, (tm, tm), 0)
    scol = lax.broadcasted_iota(jnp.int32, (tm, tm), 1)
    upper = jnp.where(srow < scol, 1.0, 0.0).astype(BF16)
    mask16 = jnp.concatenate([mask, jnp.zeros_like(mask)], axis=0).astype(BF16)
    slot = _dot(mask16, upper)[0:N_EXPERTS, :]
    cnt = jnp.broadcast_to(jnp.sum(mask, axis=1, keepdims=True), (N_EXPERTS, 128))
    cnt_ref[...] = cnt
    seg = jnp.ceil(cnt * (1.0 / ROW_ALIGN)) * ROW_ALIGN
    starts, run = [], jnp.zeros((1, 128), F32)
    for e in range(N_EXPERTS):
        starts.append(run)
        run = run + seg[e:e + 1, :]
    pos = slot + jnp.concatenate(starts, axis=0)[:, 0:1]
    p1 = jnp.sum(jnp.where(sel1, pos, 0.0), axis=0, keepdims=True)
    p2 = jnp.sum(jnp.where(sel2, pos, 0.0), axis=0, keepdims=True)
    rows = jnp.concatenate([p1, p2, w1, w2, jnp.zeros((124, tm), F32)], axis=0)
    prow_ref[...] = rows[0:ROUTE_FIELDS, :]
    pcol_ref[...] = rows.T[:, 0:ROUTE_FIELDS]


def _router(h, nw, router_t):
    tm = FFN_TILE
    nt = N_UNITS
    return pl.pallas_call(
        _router_kernel,
        grid=(nt,),
        in_specs=[
            pl.BlockSpec((tm, D_MODEL), lambda i: (i, 0)),
            pl.BlockSpec((1, D_MODEL), lambda i: (0, 0)),
            pl.BlockSpec((N_EXPERTS, D_MODEL), lambda i: (0, 0)),
        ],
        out_specs=[
            pl.BlockSpec((tm, D_MODEL), lambda i: (i, 0)),
            pl.BlockSpec((ROUTE_FIELDS, tm), lambda i: (0, i)),
            pl.BlockSpec((tm, ROUTE_FIELDS), lambda i: (i, 0)),
            pl.BlockSpec((None, N_EXPERTS, 128), lambda i: (i, 0, 0)),
        ],
        out_shape=[
            jax.ShapeDtypeStruct((N_PAD, D_MODEL), BF16),
            jax.ShapeDtypeStruct((ROUTE_FIELDS, N_PAD), F32),
            jax.ShapeDtypeStruct((N_PAD, ROUTE_FIELDS), F32),
            jax.ShapeDtypeStruct((nt, N_EXPERTS, 128), F32),
        ],
        compiler_params=pltpu.CompilerParams(
            dimension_semantics=("arbitrary",), vmem_limit_bytes=VMEM_LIMIT),
        name="moe_router",
    )(h, nw, router_t)


def _plan(cnt):
    seg = -(-cnt // ROW_ALIGN) * ROW_ALIGN
    src = (jnp.cumsum(seg, axis=1) - seg).reshape(-1).astype(jnp.int32)
    rows_e = jnp.sum(seg, axis=0)
    tiles_e = -(-rows_e // EXP_TILE)
    cum_tiles = jnp.cumsum(tiles_e)
    base_e = (cum_tiles - tiles_e) * EXP_TILE
    dst = base_e[None, :] + jnp.cumsum(seg, axis=0) - seg
    tail = jnp.concatenate([base_e + rows_e, tiles_e * EXP_TILE - rows_e]).astype(jnp.int32)
    n_tiles = cum_tiles[-1]
    i = jnp.arange(MAX_TILES, dtype=jnp.int32)
    tile_expert = jnp.sum((cum_tiles[None, :] <= i[:, None]).astype(jnp.int32), axis=1)
    tile_expert = jnp.minimum(tile_expert, N_EXPERTS - 1)
    tile_expert = jnp.where(i < n_tiles, tile_expert, tile_expert[jnp.maximum(n_tiles - 1, 0)])
    return dst.reshape(-1).astype(jnp.int32), src, tail, tile_expert, n_tiles.reshape(1).astype(jnp.int32)


def _ceil_blocks(n, block):
    return (n + block - 1) >> (block.bit_length() - 1)


def _segment_copies(n_rows, src_at, dst_at, sem, max_bits=SEG_BITS):
    n = n_rows >> (ROW_ALIGN.bit_length() - 1)
    out = []
    for bit in range(max_bits - 1, -1, -1):
        size = ROW_ALIGN << bit
        off = pl.multiple_of(((n >> (bit + 1)) << (bit + 1)) * ROW_ALIGN, ROW_ALIGN)
        out.append((((n >> bit) & 1) == 1, pltpu.make_async_copy(src_at(off, size), dst_at(off, size), sem)))
    return out


def _start_all(copies):
    for pred, cp in copies:
        pl.when(pred)(cp.start)


def _wait_all(copies):
    for pred, cp in copies:
        pl.when(pred)(cp.wait)


def _dispatch_kernel(cnt_ref, dst_ref, src_ref, tail_ref, hn_ref, prow_ref, xs_ref, stage, sems):
    tu = FFN_TILE
    t = pl.program_id(0)
    last = pl.num_programs(0) - 1

    def copies(step, e):
        slot = step % 2
        n_rows = _ceil_blocks(cnt_ref[step * N_EXPERTS + e], ROW_ALIGN) * ROW_ALIGN
        s0 = src_ref[step * N_EXPERTS + e]
        d0 = dst_ref[step * N_EXPERTS + e]
        return _segment_copies(
            n_rows,
            lambda off, size: stage.at[slot, pl.ds(pl.multiple_of(s0 + off, ROW_ALIGN), size)],
            lambda off, size: xs_ref.at[pl.ds(pl.multiple_of(d0 + off, ROW_ALIGN), size)],
            sems.at[slot])

    @pl.when(t >= 2)
    def _():
        for e in range(N_EXPERTS):
            _wait_all(copies(t - 2, e))

    rid = lax.broadcasted_iota(jnp.int32, (STAGE_ROWS, tu), 0).astype(F32)
    hit = jnp.logical_or(prow_ref[0:1, :] == rid, prow_ref[1:2, :] == rid)
    onehot = jnp.where(hit, 1.0, 0.0).astype(BF16)
    stage[t % 2] = _dot(onehot, hn_ref[...]).astype(BF16)
    for e in range(N_EXPERTS):
        _start_all(copies(t, e))

    @pl.when(t == last)
    def _():
        @pl.when(t >= 1)
        def _():
            for e in range(N_EXPERTS):
                _wait_all(copies(t - 1, e))

        for e in range(N_EXPERTS):
            _wait_all(copies(t, e))

        stage[0, 0:EXP_TILE, :] = jnp.zeros((EXP_TILE, D_MODEL), BF16)

        def tail_copies(e):
            d0 = tail_ref[e]
            return _segment_copies(
                tail_ref[N_EXPERTS + e],
                lambda off, size: stage.at[0, pl.ds(0, size)],
                lambda off, size: xs_ref.at[pl.ds(pl.multiple_of(d0 + off, ROW_ALIGN), size)],
                sems.at[e % 2], max_bits=EXP_TILE.bit_length() - ROW_ALIGN.bit_length())

        for e in range(N_EXPERTS):
            _start_all(tail_copies(e))
        for e in range(N_EXPERTS):
            _wait_all(tail_copies(e))


def _dispatch(cnt, dst, src, tail, hn, prow):
    tm = FFN_TILE
    n_rows = MAX_TILES * EXP_TILE
    grid_spec = pltpu.PrefetchScalarGridSpec(
        num_scalar_prefetch=4,
        grid=(N_UNITS,),
        in_specs=[
            pl.BlockSpec((tm, D_MODEL), lambda t, c, d, s, z: (t, 0)),
            pl.BlockSpec((ROUTE_FIELDS, tm), lambda t, c, d, s, z: (0, t)),
        ],
        out_specs=pl.BlockSpec(memory_space=pl.ANY),
        scratch_shapes=[pltpu.VMEM((2, STAGE_ROWS, D_MODEL), BF16), pltpu.SemaphoreType.DMA((2,))],
    )
    return pl.pallas_call(
        _dispatch_kernel,
        grid_spec=grid_spec,
        out_shape=jax.ShapeDtypeStruct((n_rows, D_MODEL), BF16),
        compiler_params=pltpu.CompilerParams(
            dimension_semantics=("arbitrary",), vmem_limit_bytes=VMEM_LIMIT),
        name="moe_dispatch",
    )(cnt, dst, src, tail, hn, prow)


def _experts_kernel(texp_ref, ntile_ref, xs_ref, wg_ref, wu_ref, wd_ref, ys_ref, acc_ref):
    del texp_ref
    i, c = pl.program_id(0), pl.program_id(1)
    nc = pl.num_programs(1)

    @pl.when(i < ntile_ref[0])
    def _():
        @pl.when(c == 0)
        def _():
            acc_ref[...] = jnp.zeros_like(acc_ref)

        subs = range(0, EXP_TILE, SUB_ROWS)
        gu = [(_dot(xs_ref[r:r + SUB_ROWS, :], wg_ref[...]), _dot(xs_ref[r:r + SUB_ROWS, :], wu_ref[...]))
              for r in subs]
        act = [(g * _sigmoid(g) * u).astype(BF16) for g, u in gu]
        for r, a in zip(subs, act):
            acc_ref[r:r + SUB_ROWS, :] += _dot(a, wd_ref[...])

        @pl.when(c == nc - 1)
        def _():
            ys_ref[...] = acc_ref[...].astype(BF16)


def _experts(tile_expert, n_tiles, xs, wg, wu, wd):
    tm = EXP_TILE
    nc = D_EXPERT // EXP_CHUNK

    def row_map(i, c, te, nt):
        return (jnp.minimum(i, jnp.maximum(nt[0] - 1, 0)), 0)

    def chunk(i, c, nt):
        return jnp.where(i < nt[0], c, nc - 1)

    grid_spec = pltpu.PrefetchScalarGridSpec(
        num_scalar_prefetch=2,
        grid=(MAX_TILES, nc),
        in_specs=[
            pl.BlockSpec((tm, D_MODEL), row_map),
            pl.BlockSpec((None, D_MODEL, EXP_CHUNK), lambda i, c, te, nt: (te[i], 0, chunk(i, c, nt))),
            pl.BlockSpec((None, D_MODEL, EXP_CHUNK), lambda i, c, te, nt: (te[i], 0, chunk(i, c, nt))),
            pl.BlockSpec((None, EXP_CHUNK, D_MODEL), lambda i, c, te, nt: (te[i], chunk(i, c, nt), 0)),
        ],
        out_specs=pl.BlockSpec((tm, D_MODEL), row_map),
        scratch_shapes=[pltpu.VMEM((tm, D_MODEL), F32)],
    )
    return pl.pallas_call(
        _experts_kernel,
        grid_spec=grid_spec,
        out_shape=jax.ShapeDtypeStruct((MAX_TILES * tm, D_MODEL), BF16),
        compiler_params=pltpu.CompilerParams(
            dimension_semantics=("arbitrary", "arbitrary"), vmem_limit_bytes=VMEM_LIMIT),
        name="moe_experts",
    )(tile_expert, n_tiles, xs, wg, wu, wd)


def _combine_kernel(cnt_ref, dst_ref, src_ref, ys_ref, pcol_ref, h_ref, nf_ref, op_ref, os_ref, stage, sems):
    tu = FFN_TILE
    t = pl.program_id(0)
    last = pl.num_programs(0) - 1

    def copies(step, e):
        slot = step % 2
        n_rows = _ceil_blocks(cnt_ref[step * N_EXPERTS + e], ROW_ALIGN) * ROW_ALIGN
        s0 = src_ref[step * N_EXPERTS + e]
        d0 = dst_ref[step * N_EXPERTS + e]
        return _segment_copies(
            n_rows,
            lambda off, size: ys_ref.at[pl.ds(pl.multiple_of(d0 + off, ROW_ALIGN), size)],
            lambda off, size: stage.at[slot, pl.ds(pl.multiple_of(s0 + off, ROW_ALIGN), size)],
            sems.at[slot])

    @pl.when(t == 0)
    def _():
        stage[...] = jnp.zeros_like(stage)
        for e in range(N_EXPERTS):
            _start_all(copies(0, e))

    @pl.when(t < last)
    def _():
        for e in range(N_EXPERTS):
            _start_all(copies(t + 1, e))

    for e in range(N_EXPERTS):
        _wait_all(copies(t, e))

    cid = lax.broadcasted_iota(jnp.int32, (tu, STAGE_ROWS), 1).astype(F32)
    weighted = jnp.where(pcol_ref[:, 0:1] == cid, pcol_ref[:, 2:3],
                         jnp.where(pcol_ref[:, 1:2] == cid, pcol_ref[:, 3:4], 0.0)).astype(BF16)
    out = _rms(h_ref[...] + _dot(weighted, stage[t % 2]), nf_ref[...])

    @pl.when(t < N_PROMPT // FFN_TILE)
    def _():
        op_ref[...] = out

    @pl.when(t >= N_PROMPT // FFN_TILE)
    def _():
        os_ref[...] = out


def _combine(cnt, dst, src, ys, pcol, h, nf):
    tm = FFN_TILE
    np_tiles = N_PROMPT // tm
    assert N_PROMPT % tm == 0 and N_SAMPLE % tm == 0
    grid_spec = pltpu.PrefetchScalarGridSpec(
        num_scalar_prefetch=3,
        grid=(N_UNITS,),
        in_specs=[
            pl.BlockSpec(memory_space=pl.ANY),
            pl.BlockSpec((tm, ROUTE_FIELDS), lambda t, c, d, s: (t, 0)),
            pl.BlockSpec((tm, D_MODEL), lambda t, c, d, s: (t, 0)),
            pl.BlockSpec((1, D_MODEL), lambda t, c, d, s: (0, 0)),
        ],
        out_specs=[
            pl.BlockSpec((tm, D_MODEL), lambda t, c, d, s: (jnp.minimum(t, np_tiles - 1), 0)),
            pl.BlockSpec((tm, D_MODEL), lambda t, c, d, s: (jnp.maximum(t - np_tiles, 0), 0)),
        ],
        scratch_shapes=[pltpu.VMEM((2, STAGE_ROWS, D_MODEL), BF16), pltpu.SemaphoreType.DMA((2,))],
    )
    return pl.pallas_call(
        _combine_kernel,
        grid_spec=grid_spec,
        out_shape=[
            jax.ShapeDtypeStruct((N_PROMPT, D_MODEL), F32),
            jax.ShapeDtypeStruct((N_SAMPLE, D_MODEL), F32),
        ],
        compiler_params=pltpu.CompilerParams(
            dimension_semantics=("arbitrary",), vmem_limit_bytes=VMEM_LIMIT),
        name="moe_combine",
    )(cnt, dst, src, ys, pcol, h, nf)


def kernel(x_prompt, x_sample, cache_win_k, cache_win_v, state_conv, state_pool, norm_mix, w_in, attn_sinks,
           w_attn_out, conv_w, w_conv_out, w_pool, pool_scale, w_out, norm_ffn, ffn_w_gate, ffn_w_up, ffn_w_down,
           moe_router, moe_w_gate, moe_w_up, moe_w_down, norm_final):
    assert DEPTH == 2 and ffn_w_gate.shape[0] == 1 and moe_w_gate.shape[0] == 1
    norm3 = norm_mix.reshape(DEPTH, 1, D_MODEL)
    scale3 = pool_scale.reshape(DEPTH, 1, D_MODEL)
    w_pool3 = w_pool.reshape(DEPTH, D_POOL, POOL_OUT_GROUP)
    pool_shape = (len(POOL_WINDOWS), POOL_GROUP, POOL_OUT_GROUP)
    mixer_bf16 = [(w_in[0].astype(BF16), w_attn_out[0].astype(BF16), w_conv_out[0].astype(BF16),
                   w_pool[0].astype(BF16), w_out[0].astype(BF16))]
    ffn_casts = ((ffn_w_gate, 0, 32), (ffn_w_up, 0, 32), (ffn_w_down, 0, 16))
    later_casts = (
        (moe_w_gate.reshape(1, N_EXPERTS * D_MODEL, D_EXPERT), 0, CAST_STEPS),
        (moe_w_up.reshape(1, N_EXPERTS * D_MODEL, D_EXPERT), 0, CAST_STEPS),
        (moe_w_down.reshape(1, N_EXPERTS * D_EXPERT, D_MODEL), 0, CAST_STEPS),
        (w_in, 1, 64), (w_attn_out, 1, 32), (w_conv_out, 1, 32), (w_pool3, 1, 32), (w_out, 1, 64))
    kc = cache_win_k.reshape(DEPTH, DEC_BATCH, WINDOW, D_KV)
    vc = cache_win_v.reshape(DEPTH, DEC_BATCH, WINDOW, D_KV)

    xp = x_prompt.reshape(N_PROMPT, D_MODEL)
    xs, xs_row0 = x_sample.reshape(N_SAMPLE, D_MODEL), 0
    p_states, s_states = (), ()
    for l in range(DEPTH):
        sinks = attn_sinks[l]
        wi, wao, wco, wp, wo = mixer_bf16[l]
        mix_weights = (norm3, wi, wao, conv_w, wco, wp, scale3, wo)
        h, p_states, ffn_bf16 = _prompt_mix(l, xp, sinks, mix_weights, p_states, ffn_casts if l == 0 else ())
        h, s_states = _sample_mix(l, xs, xs_row0, sinks, kc, vc, state_conv, state_pool, mix_weights, h, s_states)
        i = l // 2
        nw = norm_ffn[l].reshape(1, D_MODEL)
        if l % 2 == 0:
            xp, (eg, eu, ed, wi, wao, wco, wp, wo) = _ffn_dense(h, nw, *ffn_bf16, later_casts)
            mixer_bf16.append((wi, wao, wco, wp.reshape(pool_shape), wo))
            xs, xs_row0 = xp, N_PROMPT
        else:
            hn, prow, pcol, cnt = _router(h, nw, moe_router[i].T)
            cnt = cnt[:, :, 0].astype(jnp.int32)
            dst, src, tail, tile_expert, n_tiles = _plan(cnt)
            cnt = cnt.reshape(-1)
            xs_sorted = _dispatch(cnt, dst, src, tail, hn, prow)
            ys_sorted = _experts(tile_expert, n_tiles, xs_sorted,
                                 eg.reshape(N_EXPERTS, D_MODEL, D_EXPERT), eu.reshape(N_EXPERTS, D_MODEL, D_EXPERT),
                                 ed.reshape(N_EXPERTS, D_EXPERT, D_MODEL))
            y_prompt, y_sample = _combine(cnt, dst, src, ys_sorted, pcol, h, norm_final.reshape(1, D_MODEL))

    pk, pv, pc, pp = p_states
    sk, sv, sc, sp = s_states
    kv_shape_p = (DEPTH, BATCH, WINDOW, N_KV_HEADS, HEAD_DIM)
    kv_shape_s = (DEPTH, DEC_BATCH, WINDOW, N_KV_HEADS, HEAD_DIM)
    return (y_prompt.reshape(BATCH, SEQ, D_MODEL), y_sample.reshape(DEC_BATCH, DEC_SEQ, D_MODEL),
            pk.reshape(kv_shape_p), pv.reshape(kv_shape_p), pc, pp,
            sk.reshape(kv_shape_s), sv.reshape(kv_shape_s), sc, sp)
```

```python
import functools
import math

import jax
import jax.numpy as jnp
from jax import lax
from jax.experimental import pallas as pl
from jax.experimental.pallas import tpu as pltpu

F32 = jnp.float32
BF16 = jnp.bfloat16

D_MODEL = 1024
BATCH = 4
SEQ = 4096
DEPTH = 2
DEC_BATCH = 128
DEC_SEQ = 8
HEAD_DIM = 64
N_Q_HEADS = 8
N_KV_HEADS = 2
GQA_GROUP = N_Q_HEADS // N_KV_HEADS
D_ATTN = N_Q_HEADS * HEAD_DIM
D_KV = N_KV_HEADS * HEAD_DIM
WINDOW = 128
ATTN_SCALE = HEAD_DIM ** -0.5
D_CONV = D_MODEL // 2
CONV_WIDTH = 3
D_POOL = D_MODEL // 2
POOL_WINDOWS = (2, 4, 8, 16)
POOL_GROUP = D_POOL // len(POOL_WINDOWS)
POOL_OUT_GROUP = D_MODEL // len(POOL_WINDOWS)
POOL_BUF = max(POOL_WINDOWS) - 1
OFF_Q = 0
OFF_K = OFF_Q + D_ATTN
OFF_V = OFF_K + D_KV
OFF_CX = OFF_V + D_KV
OFF_CB = OFF_CX + D_CONV
OFF_CC = OFF_CB + D_CONV
OFF_P = OFF_CC + D_CONV
OFF_G = OFF_P + D_POOL
D_IN = OFF_G + 3 * D_MODEL
D_FF = 2816
N_EXPERTS = 8
D_EXPERT = 3584
EPS = 1e-5
LOG2E = math.log2(math.e)

LANES = 128
MXU_DEPTH = 256

N_PROMPT = BATCH * SEQ
N_SAMPLE = DEC_BATCH * DEC_SEQ
N_TOK = N_PROMPT + N_SAMPLE

PROMPT_TILE = 512
SAMPLE_SEQS = 32
SAMPLE_ROWS = SAMPLE_SEQS * DEC_SEQ
ATT_GROUP = 8
FFN_TILE = 1024
EXP_CHUNK = D_EXPERT // 2
assert EXP_CHUNK % MXU_DEPTH == 0
SUB_ROWS = 256
CAST_STEPS = 64
ROUTE_FIELDS = 8
ROW_ALIGN = 16
STAGE_ROWS = -(-(2 * FFN_TILE + N_EXPERTS * (ROW_ALIGN - 1)) // MXU_DEPTH) * MXU_DEPTH
EXP_TILE = 512
N_UNITS = N_TOK // FFN_TILE
MAX_TILES = -(-(2 * N_TOK + N_UNITS * N_EXPERTS * (ROW_ALIGN - 1)) // EXP_TILE) + N_EXPERTS
N_PAD = N_TOK
SEG_BITS = 7
assert FFN_TILE == ROW_ALIGN << (SEG_BITS - 1)
VMEM_LIMIT = 58 * 1024 * 1024


def _dot(a, b):
    return jnp.dot(a, b, preferred_element_type=F32)


def _dot_nt(a, b):
    return lax.dot_general(a, b, (((1,), (1,)), ((), ())), preferred_element_type=F32)


def _rms(x, g):
    return x * lax.rsqrt(jnp.mean(x * x, axis=-1, keepdims=True) + EPS) * g


def _sigmoid(x):
    return 0.5 * jnp.tanh(0.5 * x) + 0.5


def _head_slope(h):
    return float(2.0 ** (-8.0 * (h + 1) / N_Q_HEADS))


def _z_part(nb, win_ref, i):
    if i == 0:
        return (_dot(nb, win_ref[:, OFF_CX:OFF_P]),)
    if i == 1:
        return (_dot(nb, win_ref[:, OFF_P:OFF_G]), _dot(nb, win_ref[:, OFF_G:OFF_G + D_MODEL]))
    return (_dot(nb, win_ref[:, OFF_G + (i - 1) * D_MODEL:OFF_G + i * D_MODEL]),)


def _mix_tail(x, att_bf, cb, yc, d_groups, gate_logits, wao_ref, wco_ref, wp_ref, ps_ref, wo_ref):
    att_o = _dot(att_bf, wao_ref[...])
    merged = _sigmoid(gate_logits[0]) * att_o
    conv_o = _dot((cb * yc).astype(BF16), wco_ref[...])
    merged = merged + _sigmoid(gate_logits[1]) * conv_o
    pool_o = jnp.concatenate([_dot(d.astype(BF16), wp_ref[g]) for g, d in enumerate(d_groups)], axis=-1)
    pool_o = pool_o * ps_ref[...]
    merged = merged + _sigmoid(gate_logits[2]) * pool_o
    return x + _dot(merged.astype(BF16), wo_ref[...])


def _prompt_mix_kernel(*refs, n_alias, n_cast):
    (sinks_ref, x_ref, nw_ref, win_ref, wao_ref, cw_ref, wco_ref, wp_ref, ps_ref, wo_ref) = refs[:10]
    cast_src = refs[10 + n_alias:10 + n_alias + n_cast]
    outs = refs[10 + n_alias + n_cast:]
    h_ref, kw_ref, vw_ref, cs_ref, pst_ref = outs[:5]
    cast_dst = outs[5:5 + n_cast]
    qbuf, kbuf, vbuf, att_ref, ubuf, pbuf, bias_ref = outs[5 + n_cast:]
    _run_casts(cast_src, cast_dst)
    tm = PROMPT_TILE
    b = pl.program_id(0)
    t = pl.program_id(1)
    x = x_ref[...]
    nb = _rms(x, nw_ref[...]).astype(BF16)
    lane = lax.broadcasted_iota(jnp.int32, (tm, D_KV), 1)
    low = lane < HEAD_DIM

    @pl.when(jnp.logical_and(b == 0, t == 0))
    def _():
        qi = lax.broadcasted_iota(jnp.int32, (WINDOW, 2 * WINDOW), 0)
        si = lax.broadcasted_iota(jnp.int32, (WINDOW, 2 * WINDOW), 1)
        dist_i = qi + WINDOW - si
        dist = dist_i.astype(F32)
        band = jnp.logical_and(dist_i >= 0, dist_i <= WINDOW)
        for h in range(N_Q_HEADS):
            bias = jnp.where(band, (-_head_slope(h) * LOG2E) * dist, -jnp.inf)
            bias_ref[0, h] = bias
            bias_ref[1, h] = jnp.where(si >= WINDOW, bias, -jnp.inf)

    @pl.when(t == 0)
    def _():
        for i in range(4):
            kbuf[i, 0:WINDOW, :] = jnp.zeros((WINDOW, D_KV), BF16)
            vbuf[i, 0:WINDOW, :] = jnp.zeros((WINDOW, D_KV), BF16)
        ubuf[0:8, :] = jnp.zeros((8, D_CONV), F32)
        pbuf[0:16, :] = jnp.zeros((16, D_POOL), F32)

    @pl.when(t > 0)
    def _():
        for i in range(4):
            kbuf[i, 0:WINDOW, :] = kbuf[i, tm:tm + WINDOW, :]
            vbuf[i, 0:WINDOW, :] = vbuf[i, tm:tm + WINDOW, :]
        ubuf[0:8, :] = ubuf[tm:tm + 8, :]
        pbuf[0:16, :] = pbuf[tm:tm + 16, :]

    qkv = _dot(nb, win_ref[:, OFF_Q:OFF_CX])
    qbuf[...] = (qkv[:, OFF_Q:OFF_K] * (ATTN_SCALE * LOG2E)).astype(BF16)
    k = qkv[:, OFF_K:OFF_V]
    v = qkv[:, OFF_V:OFF_CX]
    kw_ref[...] = k[tm - WINDOW:, :]
    vw_ref[...] = v[tm - WINDOW:, :]
    k_sw = pltpu.roll(k, HEAD_DIM, 1)
    v_sw = pltpu.roll(v, HEAD_DIM, 1)
    k_var = (jnp.where(low, k, 0.0), jnp.where(low, 0.0, k_sw), jnp.where(low, k_sw, 0.0), jnp.where(low, 0.0, k))
    v_var = (jnp.where(low, v, 1.0), jnp.where(low, 1.0, v_sw), jnp.where(low, v_sw, 1.0), jnp.where(low, 1.0, v))
    for i in range(4):
        kbuf[i, WINDOW:WINDOW + tm, :] = k_var[i].astype(BF16)
        vbuf[i, WINDOW:WINDOW + tm, :] = v_var[i].astype(BF16)

    lane_q = lax.broadcasted_iota(jnp.int32, (WINDOW, 2 * HEAD_DIM), 1)
    low_q = lane_q < HEAD_DIM

    seq_start = jnp.where(t == 0, 1, 0)
    variant = [2 * (h // GQA_GROUP) + h % 2 for h in range(N_Q_HEADS)]
    units = [(j, h) for j in range(tm // WINDOW) for h in range(N_Q_HEADS)]
    z_parts = [_z_part(nb, win_ref, 0)]
    scores = {}
    for j, h in units:
        r0 = j * WINDOW
        qp = qbuf[r0:r0 + WINDOW, (h // 2) * 2 * HEAD_DIM:(h // 2 + 1) * 2 * HEAD_DIM]
        first = seq_start if j == 0 else 0
        scores[j, h] = _dot_nt(qp, kbuf[variant[h], r0:r0 + 2 * WINDOW, :]) + bias_ref[first, h]
    z_parts.append(_z_part(nb, win_ref, 1))
    probs = {}
    for j, h in units:
        sink = sinks_ref[h] * LOG2E
        m = jnp.maximum(jnp.max(scores[j, h], axis=-1, keepdims=True), sink)
        probs[j, h] = (jnp.exp2(scores[j, h] - m).astype(BF16), jnp.exp2(sink - m))
    z_parts.append(_z_part(nb, win_ref, 2))
    outs = {}
    for j, h in units:
        r0 = j * WINDOW
        p, sink_term = probs[j, h]
        o = _dot(p, vbuf[variant[h], r0:r0 + 2 * WINDOW, :])
        den = (o[:, HEAD_DIM:HEAD_DIM + 1] if h % 2 == 0 else o[:, 0:1]) + sink_term
        outs[j, h] = o / den
    z_parts.append(_z_part(nb, win_ref, 3))
    for j in range(tm // WINDOW):
        for pair in range(N_Q_HEADS // 2):
            att_ref[j * WINDOW:(j + 1) * WINDOW, pair * 2 * HEAD_DIM:(pair + 1) * 2 * HEAD_DIM] = (
                jnp.where(low_q, outs[j, 2 * pair], outs[j, 2 * pair + 1]).astype(BF16))

    (conv,), (pool_in, gate_att), (gate_conv,), (gate_pool,) = z_parts
    cx = conv[:, 0:D_CONV]
    cb = conv[:, D_CONV:2 * D_CONV]
    ubuf[8:8 + tm, :] = conv[:, 2 * D_CONV:3 * D_CONV] * cx
    yc = (cw_ref[0:1, :] * ubuf[6:6 + tm, :] + cw_ref[1:2, :] * ubuf[7:7 + tm, :]
          + cw_ref[2:3, :] * ubuf[8:8 + tm, :])
    cs_ref[...] = ubuf[tm + 6:tm + 8, :]

    pbuf[16:16 + tm, :] = pool_in
    pos = lax.broadcasted_iota(jnp.int32, (tm, 1), 0) + t * tm + 1
    d_groups = []
    for g, w in enumerate(POOL_WINDOWS):
        c0 = g * POOL_GROUP
        cur = pbuf[16:16 + tm, c0:c0 + POOL_GROUP]
        tot = cur
        for j in range(1, w):
            tot = tot + pbuf[16 - j:16 - j + tm, c0:c0 + POOL_GROUP]
        cnt = jnp.minimum(pos, w).astype(F32)
        d_groups.append(tot / cnt - cur)
    pst_ref[...] = pbuf[tm + 1:tm + 16, :]

    h_ref[...] = _mix_tail(x, att_ref[...], cb, yc, d_groups, (gate_att, gate_conv, gate_pool),
                           wao_ref, wco_ref, wp_ref, ps_ref, wo_ref)


def _weight_specs(l, grid_rank):
    def const(*idx):
        if grid_rank == 1:
            return lambda i: idx
        return lambda b, t: idx

    return [
        pl.BlockSpec((None, 1, D_MODEL), const(l, 0, 0)),
        pl.BlockSpec((D_MODEL, D_IN), const(0, 0)),
        pl.BlockSpec((D_ATTN, D_MODEL), const(0, 0)),
        pl.BlockSpec((None, CONV_WIDTH, D_CONV), const(l, 0, 0)),
        pl.BlockSpec((D_CONV, D_MODEL), const(0, 0)),
        pl.BlockSpec((len(POOL_WINDOWS), POOL_GROUP, POOL_OUT_GROUP), const(0, 0, 0)),
        pl.BlockSpec((None, 1, D_MODEL), const(l, 0, 0)),
        pl.BlockSpec((D_MODEL, D_MODEL), const(0, 0)),
    ]


def _cast_job(src, lead, n_steps, step_of):
    rows, cols = src.shape[1:]
    r = rows // n_steps
    assert r * n_steps == rows and r % ROW_ALIGN == 0

    def block(*g):
        return jnp.minimum(step_of(*g), n_steps - 1)

    return (pl.BlockSpec((None, r, cols), lambda *g: (lead, block(*g), 0)),
            pl.BlockSpec((r, cols), lambda *g: (block(*g), 0)),
            jax.ShapeDtypeStruct((rows, cols), BF16))


def _run_casts(src_refs, dst_refs):
    for s, d in zip(src_refs, dst_refs):
        d[...] = s[...].astype(BF16)


def _prompt_mix(l, x, sinks, weights, prev_states, casts=()):
    tm = PROMPT_TILE
    nt = SEQ // tm
    n_alias = len(prev_states)
    jobs = [_cast_job(src, lead, n, lambda b, t: b * nt + t) for src, lead, n in casts]
    in_specs = ([pl.BlockSpec(memory_space=pltpu.SMEM),
                 pl.BlockSpec((tm, D_MODEL), lambda b, t: (b * nt + t, 0))] + _weight_specs(l, 2)
                + [pl.BlockSpec(memory_space=pl.ANY)] * n_alias + [j[0] for j in jobs])
    out_specs = [
        pl.BlockSpec((tm, D_MODEL), lambda b, t: (b * nt + t, 0)),
        pl.BlockSpec((None, None, WINDOW, D_KV), lambda b, t: (l, b, 0, 0)),
        pl.BlockSpec((None, None, WINDOW, D_KV), lambda b, t: (l, b, 0, 0)),
        pl.BlockSpec((None, None, CONV_WIDTH - 1, D_CONV), lambda b, t: (l, b, 0, 0)),
        pl.BlockSpec((None, None, POOL_BUF, D_POOL), lambda b, t: (l, b, 0, 0)),
    ]
    out_shape = [
        jax.ShapeDtypeStruct((N_PAD, D_MODEL), F32),
        jax.ShapeDtypeStruct((DEPTH, BATCH, WINDOW, D_KV), F32),
        jax.ShapeDtypeStruct((DEPTH, BATCH, WINDOW, D_KV), F32),
        jax.ShapeDtypeStruct((DEPTH, BATCH, CONV_WIDTH - 1, D_CONV), F32),
        jax.ShapeDtypeStruct((DEPTH, BATCH, POOL_BUF, D_POOL), F32),
    ]
    scratch = [
        pltpu.VMEM((tm, D_ATTN), BF16),
        pltpu.VMEM((4, WINDOW + tm, D_KV), BF16),
        pltpu.VMEM((4, WINDOW + tm, D_KV), BF16),
        pltpu.VMEM((tm, D_ATTN), BF16),
        pltpu.VMEM((8 + tm, D_CONV), F32),
        pltpu.VMEM((16 + tm, D_POOL), F32),
        pltpu.VMEM((2, N_Q_HEADS, WINDOW, 2 * WINDOW), F32),
    ]
    outs = pl.pallas_call(
        functools.partial(_prompt_mix_kernel, n_alias=n_alias, n_cast=len(jobs)),
        grid=(BATCH, nt),
        in_specs=in_specs,
        out_specs=out_specs + [j[1] for j in jobs],
        out_shape=out_shape + [j[2] for j in jobs],
        scratch_shapes=scratch,
        input_output_aliases={10 + i: 1 + i for i in range(n_alias)},
        compiler_params=pltpu.CompilerParams(
            dimension_semantics=("arbitrary", "arbitrary"), vmem_limit_bytes=VMEM_LIMIT),
        name=f"prompt_mix_l{l}",
    )(sinks, x, *weights, *prev_states, *[c[0] for c in casts])
    return outs[0], tuple(outs[1:5]), tuple(outs[5:])


def _sample_mix_kernel(*refs, n_alias):
    (sinks_ref, x_ref, kc_ref, vc_ref, cst_ref, pin_ref,
     nw_ref, win_ref, wao_ref, cw_ref, wco_ref, wp_ref, ps_ref, wo_ref) = refs[:14]
    (h_ref, ko_ref, vo_ref, co_ref, po_ref,
     qbuf, knb, vnb, att_ref, cbuf, ebuf, bias_c, bias_n) = refs[14 + n_alias:]
    ns, nr, T = SAMPLE_SEQS, SAMPLE_ROWS, DEC_SEQ
    x = x_ref[...]
    nb = _rms(x, nw_ref[...]).astype(BF16)

    qkv = _dot(nb, win_ref[:, OFF_Q:OFF_CX])
    qbuf[...] = (qkv[:, OFF_Q:OFF_K] * ATTN_SCALE).astype(BF16)
    k = qkv[:, OFF_K:OFF_V]
    v = qkv[:, OFF_V:OFF_CX]
    knb[...] = k.astype(BF16)
    vnb[...] = v.astype(BF16)
    ko_ref[:, 0:WINDOW - T, :] = kc_ref[:, T:WINDOW, :]
    vo_ref[:, 0:WINDOW - T, :] = vc_ref[:, T:WINDOW, :]
    ko_ref[:, WINDOW - T:WINDOW, :] = k.reshape(ns, T, D_KV)
    vo_ref[:, WINDOW - T:WINDOW, :] = v.reshape(ns, T, D_KV)

    gr = ATT_GROUP * T
    gc = ATT_GROUP * WINDOW

    @pl.when(pl.program_id(0) == 0)
    def _():
        rq = lax.broadcasted_iota(jnp.int32, (gr, gc), 0)
        cq = lax.broadcasted_iota(jnp.int32, (gr, gc), 1)
        tq = rq % T
        sc_pos = cq % WINDOW
        valid_c = jnp.logical_and(rq // T == cq // WINDOW, sc_pos >= tq)
        dist_c = (WINDOW + tq - sc_pos).astype(F32)
        rn = lax.broadcasted_iota(jnp.int32, (gr, gr), 0)
        cn = lax.broadcasted_iota(jnp.int32, (gr, gr), 1)
        tn = rn % T
        jn = cn % T
        valid_n = jnp.logical_and(rn // T == cn // T, jn <= tn)
        dist_n = (tn - jn).astype(F32)
        for h in range(N_Q_HEADS):
            bias_c[h] = jnp.where(valid_c, -_head_slope(h) * dist_c, -jnp.inf)
            bias_n[h] = jnp.where(valid_n, -_head_slope(h) * dist_n, -jnp.inf)

    z_parts = []
    for gi in range(ns // ATT_GROUP):
        z_parts.append(_z_part(nb, win_ref, gi))
        r0 = gi * gr
        s0 = gi * ATT_GROUP
        qg = qbuf[r0:r0 + gr, :]
        kcg = kc_ref[s0:s0 + ATT_GROUP, :, :].reshape(gc, D_KV).astype(BF16)
        vcg = vc_ref[s0:s0 + ATT_GROUP, :, :].reshape(gc, D_KV).astype(BF16)
        kng = knb[r0:r0 + gr, :]
        vng = vnb[r0:r0 + gr, :]
        kvs = [(h // GQA_GROUP) * HEAD_DIM for h in range(N_Q_HEADS)]
        scores = []
        for h in range(N_Q_HEADS):
            qh = qg[:, h * HEAD_DIM:(h + 1) * HEAD_DIM]
            scores.append((_dot_nt(qh, kcg[:, kvs[h]:kvs[h] + HEAD_DIM]) + bias_c[h],
                           _dot_nt(qh, kng[:, kvs[h]:kvs[h] + HEAD_DIM]) + bias_n[h]))
        probs = []
        for h in range(N_Q_HEADS):
            s_c, s_n = scores[h]
            sink = sinks_ref[h]
            m = jnp.maximum(jnp.maximum(jnp.max(s_c, axis=-1, keepdims=True),
                                        jnp.max(s_n, axis=-1, keepdims=True)), sink)
            p_c = jnp.exp(s_c - m)
            p_n = jnp.exp(s_n - m)
            den = jnp.exp(sink - m) + jnp.sum(p_c, axis=-1, keepdims=True) + jnp.sum(p_n, axis=-1, keepdims=True)
            probs.append((p_c.astype(BF16), p_n.astype(BF16), den))
        outs = []
        for h in range(N_Q_HEADS):
            p_c, p_n, den = probs[h]
            o = _dot(p_c, vcg[:, kvs[h]:kvs[h] + HEAD_DIM]) + _dot(p_n, vng[:, kvs[h]:kvs[h] + HEAD_DIM])
            outs.append(o / den)
        att_ref[r0:r0 + gr, :] = jnp.concatenate(outs, axis=-1).astype(BF16)

    (conv,), (pool_in, gate_att), (gate_conv,), (gate_pool,) = z_parts
    cx = conv[:, 0:D_CONV]
    cb = conv[:, D_CONV:2 * D_CONV]
    u = conv[:, 2 * D_CONV:3 * D_CONV] * cx
    cbuf[:, 6:8, :] = cst_ref[...]
    cbuf[:, 8:16, :] = u.reshape(ns, T, D_CONV)
    w0 = cw_ref[0:1, :].reshape(1, 1, D_CONV)
    w1 = cw_ref[1:2, :].reshape(1, 1, D_CONV)
    w2 = cw_ref[2:3, :].reshape(1, 1, D_CONV)
    yc = (w0 * cbuf[:, 6:14, :] + w1 * cbuf[:, 7:15, :] + w2 * cbuf[:, 8:16, :]).reshape(nr, D_CONV)
    co_ref[...] = cbuf[:, 14:16, :]

    ebuf[:, 1:16, :] = pin_ref[...]
    ebuf[:, 16:24, :] = pool_in.reshape(ns, T, D_POOL)
    d_groups = []
    for g, w in enumerate(POOL_WINDOWS):
        c0 = g * POOL_GROUP
        cur = ebuf[:, 16:24, c0:c0 + POOL_GROUP]
        tot = cur
        for j in range(1, w):
            tot = tot + ebuf[:, 16 - j:24 - j, c0:c0 + POOL_GROUP]
        d_groups.append((tot / float(w) - cur).reshape(nr, POOL_GROUP))
    po_ref[...] = ebuf[:, 9:24, :]

    h_ref[...] = _mix_tail(x, att_ref[...], cb, yc, d_groups, (gate_att, gate_conv, gate_pool),
                           wao_ref, wco_ref, wp_ref, ps_ref, wo_ref)


def _sample_mix(l, x, x_row0, sinks, kc, vc, cst, pst, weights, h_buf, prev_states):
    ns, nr = SAMPLE_SEQS, SAMPLE_ROWS
    n_alias = 1 + len(prev_states)
    xb0 = x_row0 // nr
    hb0 = N_PROMPT // nr
    in_specs = [
        pl.BlockSpec(memory_space=pltpu.SMEM),
        pl.BlockSpec((nr, D_MODEL), lambda i: (xb0 + i, 0)),
        pl.BlockSpec((None, ns, WINDOW, D_KV), lambda i: (l, i, 0, 0)),
        pl.BlockSpec((None, ns, WINDOW, D_KV), lambda i: (l, i, 0, 0)),
        pl.BlockSpec((None, ns, CONV_WIDTH - 1, D_CONV), lambda i: (l, i, 0, 0)),
        pl.BlockSpec((None, ns, POOL_BUF, D_POOL), lambda i: (l, i, 0, 0)),
    ] + _weight_specs(l, 1) + [pl.BlockSpec(memory_space=pl.ANY)] * n_alias
    out_specs = [
        pl.BlockSpec((nr, D_MODEL), lambda i: (hb0 + i, 0)),
        pl.BlockSpec((None, ns, WINDOW, D_KV), lambda i: (l, i, 0, 0)),
        pl.BlockSpec((None, ns, WINDOW, D_KV), lambda i: (l, i, 0, 0)),
        pl.BlockSpec((None, ns, CONV_WIDTH - 1, D_CONV), lambda i: (l, i, 0, 0)),
        pl.BlockSpec((None, ns, POOL_BUF, D_POOL), lambda i: (l, i, 0, 0)),
    ]
    out_shape = [
        jax.ShapeDtypeStruct((N_PAD, D_MODEL), F32),
        jax.ShapeDtypeStruct((DEPTH, DEC_BATCH, WINDOW, D_KV), F32),
        jax.ShapeDtypeStruct((DEPTH, DEC_BATCH, WINDOW, D_KV), F32),
        jax.ShapeDtypeStruct((DEPTH, DEC_BATCH, CONV_WIDTH - 1, D_CONV), F32),
        jax.ShapeDtypeStruct((DEPTH, DEC_BATCH, POOL_BUF, D_POOL), F32),
    ]
    scratch = [
        pltpu.VMEM((nr, D_ATTN), BF16),
        pltpu.VMEM((nr, D_KV), BF16),
        pltpu.VMEM((nr, D_KV), BF16),
        pltpu.VMEM((nr, D_ATTN), BF16),
        pltpu.VMEM((ns, 16, D_CONV), F32),
        pltpu.VMEM((ns, 24, D_POOL), F32),
        pltpu.VMEM((N_Q_HEADS, ATT_GROUP * DEC_SEQ, ATT_GROUP * WINDOW), F32),
        pltpu.VMEM((N_Q_HEADS, ATT_GROUP * DEC_SEQ, ATT_GROUP * DEC_SEQ), F32),
    ]
    outs = pl.pallas_call(
        functools.partial(_sample_mix_kernel, n_alias=n_alias),
        grid=(DEC_BATCH // ns,),
        in_specs=in_specs,
        out_specs=out_specs,
        out_shape=out_shape,
        scratch_shapes=scratch,
        input_output_aliases={14 + i: i for i in range(n_alias)},
        compiler_params=pltpu.CompilerParams(
            dimension_semantics=("arbitrary",), vmem_limit_bytes=VMEM_LIMIT),
        name=f"sample_mix_l{l}",
    )(sinks, x, kc, vc, cst, pst, *weights, h_buf, *prev_states)
    return outs[0], tuple(outs[1:])


def _ffn_kernel(*refs, n_cast):
    h_ref, nw_ref, wg_ref, wu_ref, wd_ref = refs[:5]
    o_ref = refs[5 + n_cast]
    h = h_ref[...]
    hn = _rms(h, nw_ref[...]).astype(BF16)
    g = _dot(hn, wg_ref[...])
    u = _dot(hn, wu_ref[...])
    o_ref[...] = h + _dot((g * _sigmoid(g) * u).astype(BF16), wd_ref[...])
    _run_casts(refs[5:5 + n_cast], refs[6 + n_cast:])


def _ffn_dense(h, nw, wg, wu, wd, casts):
    tm = SUB_ROWS
    steps = N_TOK // tm
    assert all(n <= steps for _, _, n in casts)
    jobs = [_cast_job(src, lead, n, lambda i: i) for src, lead, n in casts]
    resident = dict(pipeline_mode=pl.Buffered(1))
    outs = pl.pallas_call(
        functools.partial(_ffn_kernel, n_cast=len(jobs)),
        grid=(steps,),
        in_specs=[
            pl.BlockSpec((tm, D_MODEL), lambda i: (i, 0)),
            pl.BlockSpec((1, D_MODEL), lambda i: (0, 0)),
            pl.BlockSpec((D_MODEL, D_FF), lambda i: (0, 0), **resident),
            pl.BlockSpec((D_MODEL, D_FF), lambda i: (0, 0), **resident),
            pl.BlockSpec((D_FF, D_MODEL), lambda i: (0, 0), **resident),
        ] + [j[0] for j in jobs],
        out_specs=[pl.BlockSpec((tm, D_MODEL), lambda i: (i, 0))] + [j[1] for j in jobs],
        out_shape=[jax.ShapeDtypeStruct((N_PAD, D_MODEL), F32)] + [j[2] for j in jobs],
        compiler_params=pltpu.CompilerParams(
            dimension_semantics=("arbitrary",), vmem_limit_bytes=VMEM_LIMIT),
        name="ffn_dense",
    )(h, nw, wg, wu, wd, *[c[0] for c in casts])
    return outs[0], tuple(outs[1:])


def _router_kernel(h_ref, nw_ref, rt_ref, hn_ref, prow_ref, pcol_ref, cnt_ref):
    tm = FFN_TILE
    hn = _rms(h_ref[...], nw_ref[...])
    hn_hi = hn.astype(BF16)
    hn_ref[...] = hn_hi
    hn_lo = (hn - hn_hi.astype(F32)).astype(BF16)
    rt = rt_ref[...]
    rt_hi = rt.astype(BF16)
    rt_lo = (rt - rt_hi.astype(F32)).astype(BF16)
    rt16 = jnp.concatenate([rt_hi, rt_lo], axis=0)
    by_hi = _dot_nt(rt16, hn_hi)
    logits = by_hi[0:N_EXPERTS] + by_hi[N_EXPERTS:2 * N_EXPERTS] + _dot_nt(rt16, hn_lo)[0:N_EXPERTS]
    eidx = lax.broadcasted_iota(jnp.int32, (N_EXPERTS, tm), 0).astype(F32)
    none = float(N_EXPERTS)
    m1 = jnp.max(logits, axis=0, keepdims=True)
    i1 = jnp.min(jnp.where(logits == m1, eidx, none), axis=0, keepdims=True)
    rest = jnp.where(eidx == i1, -jnp.inf, logits)
    m2 = jnp.max(rest, axis=0, keepdims=True)
    i2 = jnp.min(jnp.where(rest == m2, eidx, none), axis=0, keepdims=True)
    e2 = jnp.exp(m2 - m1)
    w1 = 1.0 / (1.0 + e2)
    w2 = e2 / (1.0 + e2)
    sel1 = eidx == i1
    sel2 = eidx == i2
    mask = jnp.where(jnp.logical_or(sel1, sel2), 1.0, 0.0)
    srow = lax.broadcasted_iota(jnp.int32, (tm, tm), 0)
    scol = lax.broadcasted_iota(jnp.int32, (tm, tm), 1)
    upper = jnp.where(srow < scol, 1.0, 0.0).astype(BF16)
    mask16 = jnp.concatenate([mask, jnp.zeros_like(mask)], axis=0).astype(BF16)
    slot = _dot(mask16, upper)[0:N_EXPERTS, :]
    cnt = jnp.broadcast_to(jnp.sum(mask, axis=1, keepdims=True), (N_EXPERTS, LANES))
    cnt_ref[...] = cnt
    seg = jnp.ceil(cnt * (1.0 / ROW_ALIGN)) * ROW_ALIGN
    starts, run = [], jnp.zeros((1, LANES), F32)
    for e in range(N_EXPERTS):
        starts.append(run)
        run = run + seg[e:e + 1, :]
    pos = slot + jnp.concatenate(starts, axis=0)[:, 0:1]
    p1 = jnp.sum(jnp.where(sel1, pos, 0.0), axis=0, keepdims=True)
    p2 = jnp.sum(jnp.where(sel2, pos, 0.0), axis=0, keepdims=True)
    rows = jnp.concatenate([p1, p2, w1, w2, jnp.zeros((LANES - 4, tm), F32)], axis=0)
    prow_ref[...] = rows[0:ROUTE_FIELDS, :]
    pcol_ref[...] = rows.T[:, 0:ROUTE_FIELDS]


def _router(h, nw, router_t):
    tm = FFN_TILE
    nt = N_UNITS
    return pl.pallas_call(
        _router_kernel,
        grid=(nt,),
        in_specs=[
            pl.BlockSpec((tm, D_MODEL), lambda i: (i, 0)),
            pl.BlockSpec((1, D_MODEL), lambda i: (0, 0)),
            pl.BlockSpec((N_EXPERTS, D_MODEL), lambda i: (0, 0)),
        ],
        out_specs=[
            pl.BlockSpec((tm, D_MODEL), lambda i: (i, 0)),
            pl.BlockSpec((ROUTE_FIELDS, tm), lambda i: (0, i)),
            pl.BlockSpec((tm, ROUTE_FIELDS), lambda i: (i, 0)),
            pl.BlockSpec((None, N_EXPERTS, LANES), lambda i: (i, 0, 0)),
        ],
        out_shape=[
            jax.ShapeDtypeStruct((N_PAD, D_MODEL), BF16),
            jax.ShapeDtypeStruct((ROUTE_FIELDS, N_PAD), F32),
            jax.ShapeDtypeStruct((N_PAD, ROUTE_FIELDS), F32),
            jax.ShapeDtypeStruct((nt, N_EXPERTS, LANES), F32),
        ],
        compiler_params=pltpu.CompilerParams(
            dimension_semantics=("arbitrary",), vmem_limit_bytes=VMEM_LIMIT),
        name="moe_router",
    )(h, nw, router_t)


def _plan(cnt):
    seg = -(-cnt // ROW_ALIGN) * ROW_ALIGN
    src = (jnp.cumsum(seg, axis=1) - seg).reshape(-1).astype(jnp.int32)
    rows_e = jnp.sum(seg, axis=0)
    tiles_e = -(-rows_e // EXP_TILE)
    cum_tiles = jnp.cumsum(tiles_e)
    base_e = (cum_tiles - tiles_e) * EXP_TILE
    dst = base_e[None, :] + jnp.cumsum(seg, axis=0) - seg
    tail = jnp.concatenate([base_e + rows_e, tiles_e * EXP_TILE - rows_e]).astype(jnp.int32)
    n_tiles = cum_tiles[-1]
    i = jnp.arange(MAX_TILES, dtype=jnp.int32)
    tile_expert = jnp.sum((cum_tiles[None, :] <= i[:, None]).astype(jnp.int32), axis=1)
    tile_expert = jnp.minimum(tile_expert, N_EXPERTS - 1)
    tile_expert = jnp.where(i < n_tiles, tile_expert, tile_expert[jnp.maximum(n_tiles - 1, 0)])
    return dst.reshape(-1).astype(jnp.int32), src, tail, tile_expert, n_tiles.reshape(1).astype(jnp.int32)


def _ceil_blocks(n, block):
    return (n + block - 1) >> (block.bit_length() - 1)


def _segment_copies(n_rows, src_at, dst_at, sem, max_bits=SEG_BITS):
    n = n_rows >> (ROW_ALIGN.bit_length() - 1)
    out = []
    for bit in range(max_bits - 1, -1, -1):
        size = ROW_ALIGN << bit
        off = pl.multiple_of(((n >> (bit + 1)) << (bit + 1)) * ROW_ALIGN, ROW_ALIGN)
        out.append((((n >> bit) & 1) == 1, pltpu.make_async_copy(src_at(off, size), dst_at(off, size), sem)))
    return out


def _start_all(copies):
    for pred, cp in copies:
        pl.when(pred)(cp.start)


def _wait_all(copies):
    for pred, cp in copies:
        pl.when(pred)(cp.wait)


def _dispatch_kernel(cnt_ref, dst_ref, src_ref, tail_ref, hn_ref, prow_ref, xs_ref, stage, sems):
    tu = FFN_TILE
    t = pl.program_id(0)
    last = pl.num_programs(0) - 1

    def copies(step, e):
        slot = step % 2
        n_rows = _ceil_blocks(cnt_ref[step * N_EXPERTS + e], ROW_ALIGN) * ROW_ALIGN
        s0 = src_ref[step * N_EXPERTS + e]
        d0 = dst_ref[step * N_EXPERTS + e]
        return _segment_copies(
            n_rows,
            lambda off, size: stage.at[slot, pl.ds(pl.multiple_of(s0 + off, ROW_ALIGN), size)],
            lambda off, size: xs_ref.at[pl.ds(pl.multiple_of(d0 + off, ROW_ALIGN), size)],
            sems.at[slot])

    @pl.when(t >= 2)
    def _():
        for e in range(N_EXPERTS):
            _wait_all(copies(t - 2, e))

    rid = lax.broadcasted_iota(jnp.int32, (STAGE_ROWS, tu), 0).astype(F32)
    hit = jnp.logical_or(prow_ref[0:1, :] == rid, prow_ref[1:2, :] == rid)
    onehot = jnp.where(hit, 1.0, 0.0).astype(BF16)
    stage[t % 2] = _dot(onehot, hn_ref[...]).astype(BF16)
    for e in range(N_EXPERTS):
        _start_all(copies(t, e))

    @pl.when(t == last)
    def _():
        @pl.when(t >= 1)
        def _():
            for e in range(N_EXPERTS):
                _wait_all(copies(t - 1, e))

        for e in range(N_EXPERTS):
            _wait_all(copies(t, e))

        stage[0, 0:EXP_TILE, :] = jnp.zeros((EXP_TILE, D_MODEL), BF16)

        def tail_copies(e):
            d0 = tail_ref[e]
            return _segment_copies(
                tail_ref[N_EXPERTS + e],
                lambda off, size: stage.at[0, pl.ds(0, size)],
                lambda off, size: xs_ref.at[pl.ds(pl.multiple_of(d0 + off, ROW_ALIGN), size)],
                sems.at[e % 2], max_bits=EXP_TILE.bit_length() - ROW_ALIGN.bit_length())

        for e in range(N_EXPERTS):
            _start_all(tail_copies(e))
        for e in range(N_EXPERTS):
            _wait_all(tail_copies(e))


def _dispatch(cnt, dst, src, tail, hn, prow):
    tm = FFN_TILE
    n_rows = MAX_TILES * EXP_TILE
    grid_spec = pltpu.PrefetchScalarGridSpec(
        num_scalar_prefetch=4,
        grid=(N_UNITS,),
        in_specs=[
            pl.BlockSpec((tm, D_MODEL), lambda t, c, d, s, z: (t, 0)),
            pl.BlockSpec((ROUTE_FIELDS, tm), lambda t, c, d, s, z: (0, t)),
        ],
        out_specs=pl.BlockSpec(memory_space=pl.ANY),
        scratch_shapes=[pltpu.VMEM((2, STAGE_ROWS, D_MODEL), BF16), pltpu.SemaphoreType.DMA((2,))],
    )
    return pl.pallas_call(
        _dispatch_kernel,
        grid_spec=grid_spec,
        out_shape=jax.ShapeDtypeStruct((n_rows, D_MODEL), BF16),
        compiler_params=pltpu.CompilerParams(
            dimension_semantics=("arbitrary",), vmem_limit_bytes=VMEM_LIMIT),
        name="moe_dispatch",
    )(cnt, dst, src, tail, hn, prow)


def _experts_kernel(texp_ref, ntile_ref, xs_ref, wg_ref, wu_ref, wd_ref, ys_ref, acc_ref):
    del texp_ref
    i, c = pl.program_id(0), pl.program_id(1)
    nc = pl.num_programs(1)

    @pl.when(i < ntile_ref[0])
    def _():
        @pl.when(c == 0)
        def _():
            acc_ref[...] = jnp.zeros_like(acc_ref)

        subs = range(0, EXP_TILE, SUB_ROWS)
        gu = [(_dot(xs_ref[r:r + SUB_ROWS, :], wg_ref[...]), _dot(xs_ref[r:r + SUB_ROWS, :], wu_ref[...]))
              for r in subs]
        act = [(g * _sigmoid(g) * u).astype(BF16) for g, u in gu]
        for r, a in zip(subs, act):
            acc_ref[r:r + SUB_ROWS, :] += _dot(a, wd_ref[...])

        @pl.when(c == nc - 1)
        def _():
            ys_ref[...] = acc_ref[...].astype(BF16)


def _experts(tile_expert, n_tiles, xs, wg, wu, wd):
    tm = EXP_TILE
    nc = D_EXPERT // EXP_CHUNK

    def row_map(i, c, te, nt):
        return (jnp.minimum(i, jnp.maximum(nt[0] - 1, 0)), 0)

    def chunk(i, c, nt):
        return jnp.where(i < nt[0], c, nc - 1)

    grid_spec = pltpu.PrefetchScalarGridSpec(
        num_scalar_prefetch=2,
        grid=(MAX_TILES, nc),
        in_specs=[
            pl.BlockSpec((tm, D_MODEL), row_map),
            pl.BlockSpec((None, D_MODEL, EXP_CHUNK), lambda i, c, te, nt: (te[i], 0, chunk(i, c, nt))),
            pl.BlockSpec((None, D_MODEL, EXP_CHUNK), lambda i, c, te, nt: (te[i], 0, chunk(i, c, nt))),
            pl.BlockSpec((None, EXP_CHUNK, D_MODEL), lambda i, c, te, nt: (te[i], chunk(i, c, nt), 0)),
        ],
        out_specs=pl.BlockSpec((tm, D_MODEL), row_map),
        scratch_shapes=[pltpu.VMEM((tm, D_MODEL), F32)],
    )
    return pl.pallas_call(
        _experts_kernel,
        grid_spec=grid_spec,
        out_shape=jax.ShapeDtypeStruct((MAX_TILES * tm, D_MODEL), BF16),
        compiler_params=pltpu.CompilerParams(
            dimension_semantics=("arbitrary", "arbitrary"), vmem_limit_bytes=VMEM_LIMIT),
        name="moe_experts",
    )(tile_expert, n_tiles, xs, wg, wu, wd)


def _combine_kernel(cnt_ref, dst_ref, src_ref, ys_ref, pcol_ref, h_ref, nf_ref, op_ref, os_ref, stage, sems):
    tu = FFN_TILE
    t = pl.program_id(0)
    last = pl.num_programs(0) - 1

    def copies(step, e):
        slot = step % 2
        n_rows = _ceil_blocks(cnt_ref[step * N_EXPERTS + e], ROW_ALIGN) * ROW_ALIGN
        s0 = src_ref[step * N_EXPERTS + e]
        d0 = dst_ref[step * N_EXPERTS + e]
        return _segment_copies(
            n_rows,
            lambda off, size: ys_ref.at[pl.ds(pl.multiple_of(d0 + off, ROW_ALIGN), size)],
            lambda off, size: stage.at[slot, pl.ds(pl.multiple_of(s0 + off, ROW_ALIGN), size)],
            sems.at[slot])

    @pl.when(t == 0)
    def _():
        stage[...] = jnp.zeros_like(stage)
        for e in range(N_EXPERTS):
            _start_all(copies(0, e))

    @pl.when(t < last)
    def _():
        for e in range(N_EXPERTS):
            _start_all(copies(t + 1, e))

    for e in range(N_EXPERTS):
        _wait_all(copies(t, e))

    cid = lax.broadcasted_iota(jnp.int32, (tu, STAGE_ROWS), 1).astype(F32)
    weighted = jnp.where(pcol_ref[:, 0:1] == cid, pcol_ref[:, 2:3],
                         jnp.where(pcol_ref[:, 1:2] == cid, pcol_ref[:, 3:4], 0.0)).astype(BF16)
    out = _rms(h_ref[...] + _dot(weighted, stage[t % 2]), nf_ref[...])

    @pl.when(t < N_PROMPT // FFN_TILE)
    def _():
        op_ref[...] = out

    @pl.when(t >= N_PROMPT // FFN_TILE)
    def _():
        os_ref[...] = out


def _combine(cnt, dst, src, ys, pcol, h, nf):
    tm = FFN_TILE
    np_tiles = N_PROMPT // tm
    assert N_PROMPT % tm == 0 and N_SAMPLE % tm == 0
    grid_spec = pltpu.PrefetchScalarGridSpec(
        num_scalar_prefetch=3,
        grid=(N_UNITS,),
        in_specs=[
            pl.BlockSpec(memory_space=pl.ANY),
            pl.BlockSpec((tm, ROUTE_FIELDS), lambda t, c, d, s: (t, 0)),
            pl.BlockSpec((tm, D_MODEL), lambda t, c, d, s: (t, 0)),
            pl.BlockSpec((1, D_MODEL), lambda t, c, d, s: (0, 0)),
        ],
        out_specs=[
            pl.BlockSpec((tm, D_MODEL), lambda t, c, d, s: (jnp.minimum(t, np_tiles - 1), 0)),
            pl.BlockSpec((tm, D_MODEL), lambda t, c, d, s: (jnp.maximum(t - np_tiles, 0), 0)),
        ],
        scratch_shapes=[pltpu.VMEM((2, STAGE_ROWS, D_MODEL), BF16), pltpu.SemaphoreType.DMA((2,))],
    )
    return pl.pallas_call(
        _combine_kernel,
        grid_spec=grid_spec,
        out_shape=[
            jax.ShapeDtypeStruct((N_PROMPT, D_MODEL), F32),
            jax.ShapeDtypeStruct((N_SAMPLE, D_MODEL), F32),
        ],
        compiler_params=pltpu.CompilerParams(
            dimension_semantics=("arbitrary",), vmem_limit_bytes=VMEM_LIMIT),
        name="moe_combine",
    )(cnt, dst, src, ys, pcol, h, nf)


def kernel(x_prompt, x_sample, cache_win_k, cache_win_v, state_conv, state_pool, norm_mix, w_in, attn_sinks,
           w_attn_out, conv_w, w_conv_out, w_pool, pool_scale, w_out, norm_ffn, ffn_w_gate, ffn_w_up, ffn_w_down,
           moe_router, moe_w_gate, moe_w_up, moe_w_down, norm_final):
    assert DEPTH == 2 and ffn_w_gate.shape[0] == 1 and moe_w_gate.shape[0] == 1
    norm3 = norm_mix.reshape(DEPTH, 1, D_MODEL)
    scale3 = pool_scale.reshape(DEPTH, 1, D_MODEL)
    w_pool3 = w_pool.reshape(DEPTH, D_POOL, POOL_OUT_GROUP)
    pool_shape = (len(POOL_WINDOWS), POOL_GROUP, POOL_OUT_GROUP)
    mixer_bf16 = [(w_in[0].astype(BF16), w_attn_out[0].astype(BF16), w_conv_out[0].astype(BF16),
                   w_pool[0].astype(BF16), w_out[0].astype(BF16))]
    ffn_casts = ((ffn_w_gate, 0, 32), (ffn_w_up, 0, 32), (ffn_w_down, 0, 16))
    later_casts = (
        (moe_w_gate.reshape(1, N_EXPERTS * D_MODEL, D_EXPERT), 0, CAST_STEPS),
        (moe_w_up.reshape(1, N_EXPERTS * D_MODEL, D_EXPERT), 0, CAST_STEPS),
        (moe_w_down.reshape(1, N_EXPERTS * D_EXPERT, D_MODEL), 0, CAST_STEPS),
        (w_in, 1, 64), (w_attn_out, 1, 32), (w_conv_out, 1, 32), (w_pool3, 1, 32), (w_out, 1, 64))
    kc = cache_win_k.reshape(DEPTH, DEC_BATCH, WINDOW, D_KV)
    vc = cache_win_v.reshape(DEPTH, DEC_BATCH, WINDOW, D_KV)

    xp = x_prompt.reshape(N_PROMPT, D_MODEL)
    xs, xs_row0 = x_sample.reshape(N_SAMPLE, D_MODEL), 0
    p_states, s_states = (), ()
    for l in range(DEPTH):
        sinks = attn_sinks[l]
        wi, wao, wco, wp, wo = mixer_bf16[l]
        mix_weights = (norm3, wi, wao, conv_w, wco, wp, scale3, wo)
        h, p_states, ffn_bf16 = _prompt_mix(l, xp, sinks, mix_weights, p_states, ffn_casts if l == 0 else ())
        h, s_states = _sample_mix(l, xs, xs_row0, sinks, kc, vc, state_conv, state_pool, mix_weights, h, s_states)
        i = l // 2
        nw = norm_ffn[l].reshape(1, D_MODEL)
        if l % 2 == 0:
            xp, (eg, eu, ed, wi, wao, wco, wp, wo) = _ffn_dense(h, nw, *ffn_bf16, later_casts)
            mixer_bf16.append((wi, wao, wco, wp.reshape(pool_shape), wo))
            xs, xs_row0 = xp, N_PROMPT
        else:
            hn, prow, pcol, cnt = _router(h, nw, moe_router[i].T)
            cnt = cnt[:, :, 0].astype(jnp.int32)
            dst, src, tail, tile_expert, n_tiles = _plan(cnt)
            cnt = cnt.reshape(-1)
            xs_sorted = _dispatch(cnt, dst, src, tail, hn, prow)
            ys_sorted = _experts(tile_expert, n_tiles, xs_sorted,
                                 eg.reshape(N_EXPERTS, D_MODEL, D_EXPERT), eu.reshape(N_EXPERTS, D_MODEL, D_EXPERT),
                                 ed.reshape(N_EXPERTS, D_EXPERT, D_MODEL))
            y_prompt, y_sample = _combine(cnt, dst, src, ys_sorted, pcol, h, norm_final.reshape(1, D_MODEL))

    pk, pv, pc, pp = p_states
    sk, sv, sc, sp = s_states
    kv_shape_p = (DEPTH, BATCH, WINDOW, N_KV_HEADS, HEAD_DIM)
    kv_shape_s = (DEPTH, DEC_BATCH, WINDOW, N_KV_HEADS, HEAD_DIM)
    return (y_prompt.reshape(BATCH, SEQ, D_MODEL), y_sample.reshape(DEC_BATCH, DEC_SEQ, D_MODEL),
            pk.reshape(kv_shape_p), pv.reshape(kv_shape_p), pc, pp,
            sk.reshape(kv_shape_s), sv.reshape(kv_shape_s), sc, sp)
```
